```python
import math
import jax
import jax.numpy as jnp
from jax import lax
import numpy as np

D_MODEL = 1024
BATCH = 8
SEQ = 2048
DEPTH = 2
DEC_BATCH = 32
DEC_SEQ = 1
PAST_LEN = 16384
PAGE_SIZE = 128

N_META = 16
EPS = 1e-6
H_A = 8
DK_A = 64
DV_A = 64
CONV_W = 4
CHUNK_A = 64
H_B = 8
DH_B = 64
H_IDX = 8
D_IDX = 64
TOPK_MAX = 256
REL_BUCKETS = 32
REL_MAX_DIST = 1024
H_C = 8
Q_LORA = 384
KV_LORA = 256
QK_NOPE = 64
QK_ROPE = 32
V_C = 128
ROPE_BASE = 10000.0
D_FF = 4 * D_MODEL
Q_BLOCK = 128

QKV_A = H_A * (2 * DK_A + DV_A)
AB_SIZES = (QKV_A, H_A, H_A, H_A * DV_A, H_B * DH_B, H_B * DH_B, H_B * DH_B, H_IDX * D_IDX, H_IDX, D_IDX)
AB_COLS = sum(AB_SIZES)
C_SIZES = (Q_LORA, KV_LORA, QK_ROPE)
C_COLS = sum(C_SIZES)
MLA_SCALE = (QK_NOPE + QK_ROPE) ** -0.5
N_AB = (DEPTH + 1) // 2
N_C = DEPTH // 2

kernel_name = 'hybrid_gdn_dsa_mla_step'


def rmsnorm(x, w):
    xf = x.astype(jnp.float32)
    y = xf * lax.rsqrt(jnp.mean(xf * xf, axis=-1, keepdims=True) + EPS)
    return (y * w.astype(jnp.float32)).astype(x.dtype)


def l2norm(x):
    return x * lax.rsqrt(jnp.sum(x * x, axis=-1, keepdims=True) + EPS)


def split_cols(x, sizes):
    out, off = [], 0
    for s in sizes:
        out.append(x[..., off:off + s])
        off += s
    return out


def to_blocks(a, n_blocks):
    b, t = a.shape[:2]
    pad = n_blocks * Q_BLOCK - t
    a = jnp.pad(a, [(0, 0), (0, pad)] + [(0, 0)] * (a.ndim - 2))
    return jnp.moveaxis(a.reshape((b, n_blocks, Q_BLOCK) + a.shape[2:]), 1, 0)


def from_blocks(a, t):
    a = jnp.moveaxis(a, 0, 1)
    return a.reshape((a.shape[0], -1) + a.shape[3:])[:, :t]


def gather_pages(pool, page_table):
    g = pool[page_table]
    return g.reshape((g.shape[0], g.shape[1] * g.shape[2]) + g.shape[3:])


def gather_rows(pool, page_table, idx):
    db = idx.shape[0]
    page = idx // PAGE_SIZE
    off = idx % PAGE_SIZE
    phys = jnp.take_along_axis(page_table, page.reshape(db, -1), axis=1).reshape(idx.shape)
    return pool[phys, off]


def rel_bucket(dist):
    dist = jnp.maximum(dist, 0)
    max_exact = REL_BUCKETS // 2
    log_ratio = jnp.log(jnp.maximum(dist, 1).astype(jnp.float32) / max_exact) / math.log(REL_MAX_DIST / max_exact)
    large = max_exact + (log_ratio * (REL_BUCKETS - max_exact)).astype(jnp.int32)
    return jnp.where(dist < max_exact, dist, jnp.minimum(large, REL_BUCKETS - 1))


def rope(x, pos):
    half = QK_ROPE // 2
    inv = ROPE_BASE ** (-jnp.arange(half, dtype=jnp.float32) / half)
    ang = pos.astype(jnp.float32)[:, None] * inv[None, :]
    cos = jnp.cos(ang)[:, None, :]
    sin = jnp.sin(ang)[:, None, :]
    xf = x.astype(jnp.float32)
    x1, x2 = xf[..., :half], xf[..., half:]
    return jnp.concatenate([x1 * cos - x2 * sin, x2 * cos + x1 * sin], axis=-1).astype(x.dtype)


def causal_conv(xp, w, t):
    acc = xp[:, 0:t] * w[0]
    for j in range(1, CONV_W):
        acc = acc + xp[:, j:j + t] * w[j]
    return jax.nn.silu(acc)


def sq_relu_mlp(x, w_up, w_down):
    return jnp.square(jax.nn.relu(x @ w_up)) @ w_down


def gdn_prep(qkv, alpha, braw, a_log, dt_bias):
    b, t = qkv.shape[:2]
    qkv = qkv.astype(jnp.float32)
    q, k, v = split_cols(qkv, (H_A * DK_A, H_A * DK_A, H_A * DV_A))
    q = l2norm(q.reshape(b, t, H_A, DK_A)) * DK_A ** -0.5
    k = l2norm(k.reshape(b, t, H_A, DK_A))
    v = v.reshape(b, t, H_A, DV_A)
    g = -jnp.exp(a_log.astype(jnp.float32)) * jax.nn.softplus(alpha.astype(jnp.float32) + dt_bias.astype(jnp.float32))
    beta = jax.nn.sigmoid(braw.astype(jnp.float32))
    return q, k, v, g, beta


def gdn_chunk(s, q, k, v, g, beta):
    c = q.shape[2]
    gcum = jnp.cumsum(g, axis=-1)
    incl = jnp.tril(jnp.ones((c, c), dtype=bool))
    strict = jnp.tril(jnp.ones((c, c), dtype=bool), -1)
    decay = jnp.exp(jnp.where(incl, gcum[..., :, None] - gcum[..., None, :], -jnp.inf))
    kk = jnp.einsum('bhid,bhjd->bhij', k, k)
    a_mat = jnp.where(strict, beta[..., :, None] * kk * decay, 0.0) + jnp.eye(c, dtype=q.dtype)
    rhs = jnp.concatenate([v * beta[..., None], k * (beta * jnp.exp(gcum))[..., None]], axis=-1)
    sol = lax.linalg.triangular_solve(a_mat, rhs, left_side=True, lower=True, unit_diagonal=True)
    u, w = sol[..., :DV_A], sol[..., DV_A:]
    v_new = u - jnp.einsum('bhck,bhkv->bhcv', w, s)
    qk = jnp.einsum('bhid,bhjd->bhij', q, k) * decay
    o = jnp.einsum('bhck,bhkv->bhcv', q * jnp.exp(gcum)[..., None], s) + jnp.einsum('bhij,bhjv->bhiv', qk, v_new)
    g_last = gcum[..., -1:]
    s_new = s * jnp.exp(g_last)[..., None] + jnp.einsum('bhck,bhcv->bhkv', k * jnp.exp(g_last - gcum)[..., None], v_new)
    return s_new, o


def gdn_step(s, xs):
    q, k, v, g, beta = xs
    s = s * jnp.exp(g)[..., None, None]
    delta = (v - jnp.einsum('bhk,bhkv->bhv', k, s)) * beta[..., None]
    s = s + jnp.einsum('bhk,bhv->bhkv', k, delta)
    return s, jnp.einsum('bhk,bhkv->bhv', q, s)


def gdn_prompt(q, k, v, g, beta):
    b, t = q.shape[:2]
    q, k, v, g, beta = (jnp.moveaxis(a, 1, 2) for a in (q, k, v, g, beta))
    s0 = jnp.zeros((b, H_A, DK_A, DV_A), jnp.float32)
    s1, o_meta = gdn_chunk(s0, q[:, :, :N_META], k[:, :, :N_META], v[:, :, :N_META], g[:, :, :N_META], beta[:, :, :N_META])
    n_chunks = (t - N_META) // CHUNK_A

    def to_chunks(a):
        a = a[:, :, N_META:]
        a = a.reshape(a.shape[:2] + (n_chunks, CHUNK_A) + a.shape[3:])
        return jnp.moveaxis(a, 2, 0)

    s_fin, o_chunks = lax.scan(lambda s, xs: gdn_chunk(s, *xs), s1, tuple(to_chunks(a) for a in (q, k, v, g, beta)))
    o_real = jnp.moveaxis(o_chunks, 0, 2).reshape(b, H_A, n_chunks * CHUNK_A, DV_A)
    o = jnp.concatenate([o_meta, o_real], axis=2)
    return jnp.moveaxis(o, 1, 2), s_fin


def gdn_decode(s, q, k, v, g, beta):
    xs = tuple(jnp.moveaxis(a, 1, 0) for a in (q, k, v, g, beta))
    s_fin, o = lax.scan(gdn_step, s.astype(jnp.float32), xs)
    return jnp.moveaxis(o, 0, 1), s_fin


def index_scores(qi, wi, ki):
    s = jax.nn.relu(jnp.einsum('bthd,bsd->bths', qi, ki).astype(jnp.float32) * D_IDX ** -0.5)
    return jnp.einsum('bths,bth->bts', s, wi.astype(jnp.float32) * H_IDX ** -0.5)


def sparse_attend(q, k_sel, v_sel, valid, dist, rel_bias):
    logits = jnp.einsum('bthd,btkhd->bthk', q, k_sel).astype(jnp.float32) * DH_B ** -0.5
    bias = rel_bias[rel_bucket(dist)]
    logits = logits + jnp.moveaxis(bias, -1, -2).astype(jnp.float32)
    logits = jnp.where(valid[:, :, None, :], logits, -jnp.inf)
    p = jax.nn.softmax(logits, axis=-1).astype(v_sel.dtype)
    return jnp.einsum('bthk,btkhd->bthd', p, v_sel)


def gather_seq(rows, idx):
    return jax.vmap(lambda rb, ib: rb[ib])(rows, idx)


def dsa_prompt(q, k, v, qi, wi, ki, rel_bias):
    b, t = q.shape[:2]
    topk = min(TOPK_MAX, SEQ // 4)
    nb = -(-t // Q_BLOCK)
    pos_blocks = jnp.arange(nb * Q_BLOCK, dtype=jnp.int32).reshape(nb, Q_BLOCK)
    key_pos = jnp.arange(t, dtype=jnp.int32)

    def one_block(xs):
        qb, qib, wib, pos = xs
        sc = index_scores(qib, wib, ki)
        sc = jnp.where((key_pos[None, :] <= pos[:, None])[None], sc, -jnp.inf)
        _, idx = lax.top_k(sc, topk)
        valid = idx <= pos[None, :, None]
        dist = pos[None, :, None] - idx
        return sparse_attend(qb, gather_seq(k, idx), gather_seq(v, idx), valid, dist, rel_bias)

    o = lax.map(one_block, (to_blocks(q, nb), to_blocks(qi, nb), to_blocks(wi, nb), pos_blocks))
    return from_blocks(o, t)


def dsa_decode(q, k_new, v_new, qi, wi, ki_new, cache_k, cache_v, cache_ki, page_table, rel_bias):
    db, t = q.shape[:2]
    n_keys = PAST_LEN + t
    topk = min(TOPK_MAX, n_keys // 4)
    ki_all = jnp.concatenate([gather_pages(cache_ki, page_table).astype(ki_new.dtype), ki_new], axis=1)
    pos = PAST_LEN + jnp.arange(t, dtype=jnp.int32)
    sc = index_scores(qi, wi, ki_all)
    key_pos = jnp.arange(n_keys, dtype=jnp.int32)
    sc = jnp.where((key_pos[None, :] <= pos[:, None])[None], sc, -jnp.inf)
    _, idx = lax.top_k(sc, topk)
    valid = idx <= pos[None, :, None]
    in_past = (idx < PAST_LEN)[..., None, None]
    past_idx = jnp.minimum(idx, PAST_LEN - 1)
    new_idx = jnp.clip(idx - PAST_LEN, 0, t - 1)
    k_sel = jnp.where(in_past, gather_rows(cache_k, page_table, past_idx).astype(k_new.dtype), gather_seq(k_new, new_idx))
    v_sel = jnp.where(in_past, gather_rows(cache_v, page_table, past_idx).astype(v_new.dtype), gather_seq(v_new, new_idx))
    dist = pos[None, :, None] - idx
    return sparse_attend(q, k_sel, v_sel, valid, dist, rel_bias)


def mixer_ab(hn, w_in, conv_w, a_log, dt_bias, gnorm, w_out, rel_bias, past):
    b, t, _ = hn.shape
    qkv_a, alpha, braw, gate, q_b, k_b, v_b, qi, wi, ki = split_cols(hn @ w_in, AB_SIZES)
    if past is None:
        conv_buf = jnp.zeros((b, CONV_W - 1, QKV_A), qkv_a.dtype)
    else:
        conv_buf, s_prev, cache_k, cache_v, cache_ki, page_table = past
    xp = jnp.concatenate([conv_buf.astype(qkv_a.dtype), qkv_a], axis=1)
    q, k, v, g, beta = gdn_prep(causal_conv(xp, conv_w, t), alpha, braw, a_log, dt_bias)
    if past is None:
        o_a, s_new = gdn_prompt(q, k, v, g, beta)
    else:
        o_a, s_new = gdn_decode(s_prev, q, k, v, g, beta)
    o_a = rmsnorm(o_a, gnorm) * jax.nn.silu(gate.reshape(b, t, H_A, DV_A).astype(jnp.float32))
    o_a = o_a.reshape(b, t, H_A * DV_A).astype(hn.dtype)
    q_b = q_b.reshape(b, t, H_B, DH_B)
    k_b = k_b.reshape(b, t, H_B, DH_B)
    v_b = v_b.reshape(b, t, H_B, DH_B)
    qi = qi.reshape(b, t, H_IDX, D_IDX)
    if past is None:
        o_b = dsa_prompt(q_b, k_b, v_b, qi, wi, ki, rel_bias)
    else:
        o_b = dsa_decode(q_b, k_b, v_b, qi, wi, ki, cache_k, cache_v, cache_ki, page_table, rel_bias)
    out = jnp.concatenate([o_a, o_b.reshape(b, t, H_B * DH_B).astype(hn.dtype)], axis=-1) @ w_out
    return out, (xp[:, -(CONV_W - 1):], s_new, k_b, v_b, ki)


def mla_core(q_lat, q_pe, c, kpe, visible):
    logits = (jnp.einsum('bthc,bsc->bhts', q_lat, c) + jnp.einsum('bthr,bsr->bhts', q_pe, kpe)).astype(jnp.float32) * MLA_SCALE
    logits = jnp.where(visible[None, None], logits, -jnp.inf)
    p = jax.nn.softmax(logits, axis=-1).astype(c.dtype)
    return jnp.einsum('bhts,bsc->bthc', p, c)


def mla_prompt(q_lat, q_pe, c, kpe):
    t = q_lat.shape[1]
    nb = -(-t // Q_BLOCK)
    pos_blocks = jnp.arange(nb * Q_BLOCK, dtype=jnp.int32).reshape(nb, Q_BLOCK)
    key_pos = jnp.arange(t, dtype=jnp.int32)

    def one_block(xs):
        ql, qp, pos = xs
        return mla_core(ql, qp, c, kpe, key_pos[None, :] <= pos[:, None])

    o = lax.map(one_block, (to_blocks(q_lat, nb), to_blocks(q_pe, nb), pos_blocks))
    return from_blocks(o, t)


def mixer_c(hn, w_in, qnorm, kvnorm, w_uq, w_uk, w_uv, w_out, past):
    b, t, _ = hn.shape
    cq, ckv, kpe = split_cols(hn @ w_in, C_SIZES)
    offset = 0 if past is None else PAST_LEN
    pos = offset + jnp.arange(t, dtype=jnp.int32)
    qf = (rmsnorm(cq, qnorm) @ w_uq).reshape(b, t, H_C, QK_NOPE + QK_ROPE)
    q_pe = rope(qf[..., QK_NOPE:], pos)
    q_lat = jnp.einsum('bthn,chn->bthc', qf[..., :QK_NOPE], w_uk)
    c = rmsnorm(ckv, kvnorm)
    kpe = rope(kpe[:, :, None, :], pos)[:, :, 0, :]
    if past is None:
        o_lat = mla_prompt(q_lat, q_pe, c, kpe)
    else:
        cache_c, cache_kpe, page_table = past
        c_all = jnp.concatenate([gather_pages(cache_c, page_table).astype(c.dtype), c], axis=1)
        kpe_all = jnp.concatenate([gather_pages(cache_kpe, page_table).astype(kpe.dtype), kpe], axis=1)
        key_pos = jnp.arange(PAST_LEN + t, dtype=jnp.int32)
        o_lat = mla_core(q_lat, q_pe, c_all, kpe_all, key_pos[None, :] <= pos[:, None])
    o = jnp.einsum('bthc,chv->bthv', o_lat, w_uv).reshape(b, t, H_C * V_C)
    return o @ w_out, (c, kpe)


def setup_inputs(seed: int = 0) -> dict:
    key = jax.random.key(seed)
    keys = list(jax.random.split(key, 40))

    def nrm(shape, scale=1.0):
        return jax.random.normal(keys.pop(), shape, jnp.float32) * scale

    def gain(shape):
        return 1.0 + nrm(shape, 0.01)

    n_pages = PAST_LEN // PAGE_SIZE
    n_used = DEC_BATCH * n_pages
    n_pool = n_used + n_used // 4
    page_table = jax.random.permutation(keys.pop(), n_pool)[:n_used].reshape(DEC_BATCH, n_pages).astype(jnp.int32)
    dt = jnp.exp(jax.random.uniform(keys.pop(), (N_AB, H_A), jnp.float32, math.log(1e-3), math.log(1e-1)))
    dt_bias_a = dt + jnp.log(-jnp.expm1(-dt))
    a_log = jnp.log(jax.random.uniform(keys.pop(), (N_AB, H_A), jnp.float32, 1.0, 16.0))
    return {
        'x_prompt': nrm((BATCH, SEQ, D_MODEL)),
        'x_sample': nrm((DEC_BATCH, DEC_SEQ, D_MODEL)),
        'state_a_conv': nrm((N_AB, DEC_BATCH, CONV_W - 1, QKV_A)),
        'state_a_ssm': nrm((N_AB, DEC_BATCH, H_A, DK_A, DV_A), 0.1),
        'cache_b_k': nrm((N_AB, n_pool, PAGE_SIZE, H_B, DH_B)),
        'cache_b_v': nrm((N_AB, n_pool, PAGE_SIZE, H_B, DH_B)),
        'cache_b_kidx': nrm((N_AB, n_pool, PAGE_SIZE, D_IDX)),
        'cache_c_latent': nrm((N_C, n_pool, PAGE_SIZE, KV_LORA)),
        'cache_c_kpe': nrm((N_C, n_pool, PAGE_SIZE, QK_ROPE)),
        'page_table': page_table,
        'meta_tokens': nrm((N_META, D_MODEL)),
        'rel_bias': nrm((REL_BUCKETS, H_B), 0.5),
        'norm_mix': gain((DEPTH, D_MODEL)),
        'norm_ffn': gain((DEPTH, D_MODEL)),
        'norm_final': gain((D_MODEL,)),
        'w_in_ab': nrm((N_AB, D_MODEL, AB_COLS), D_MODEL ** -0.5),
        'conv_a': nrm((N_AB, CONV_W, QKV_A), CONV_W ** -0.5),
        'a_log': a_log,
        'dt_bias_a': dt_bias_a,
        'gnorm_a': gain((N_AB, DV_A)),
        'w_out_ab': nrm((N_AB, H_A * DV_A + H_B * DH_B, D_MODEL), (H_A * DV_A + H_B * DH_B) ** -0.5),
        'w_in_c': nrm((N_C, D_MODEL, C_COLS), D_MODEL ** -0.5),
        'qnorm_c': gain((N_C, Q_LORA)),
        'kvnorm_c': gain((N_C, KV_LORA)),
        'w_uq_c': nrm((N_C, Q_LORA, H_C * (QK_NOPE + QK_ROPE)), Q_LORA ** -0.5),
        'w_uk_c': nrm((N_C, KV_LORA, H_C, QK_NOPE), KV_LORA ** -0.5),
        'w_uv_c': nrm((N_C, KV_LORA, H_C, V_C), KV_LORA ** -0.5),
        'w_out_c': nrm((N_C, H_C * V_C, D_MODEL), (H_C * V_C) ** -0.5),
        'w_up': nrm((DEPTH, D_MODEL, D_FF), D_MODEL ** -0.5),
        'w_down': nrm((DEPTH, D_FF, D_MODEL), D_FF ** -0.5),
    }


def reference(x_prompt, x_sample, state_a_conv, state_a_ssm, cache_b_k, cache_b_v, cache_b_kidx,
              cache_c_latent, cache_c_kpe, page_table, meta_tokens, rel_bias, norm_mix, norm_ffn, norm_final,
              w_in_ab, conv_a, a_log, dt_bias_a, gnorm_a, w_out_ab, w_in_c, qnorm_c, kvnorm_c,
              w_uq_c, w_uk_c, w_uv_c, w_out_c, w_up, w_down):

    def trunk(h, sample):
        ab_states, c_states = [], []
        for li in range(DEPTH):
            j = li // 2
            hn = rmsnorm(h, norm_mix[li])
            if li % 2 == 0:
                past = (state_a_conv[j], state_a_ssm[j], cache_b_k[j], cache_b_v[j], cache_b_kidx[j], page_table) if sample else None
                out, st = mixer_ab(hn, w_in_ab[j], conv_a[j], a_log[j], dt_bias_a[j], gnorm_a[j], w_out_ab[j], rel_bias, past)
                ab_states.append(st)
            else:
                past = (cache_c_latent[j], cache_c_kpe[j], page_table) if sample else None
                out, st = mixer_c(hn, w_in_c[j], qnorm_c[j], kvnorm_c[j], w_uq_c[j], w_uk_c[j], w_uv_c[j], w_out_c[j], past)
                c_states.append(st)
            h = h + out.astype(h.dtype)
            h = h + sq_relu_mlp(rmsnorm(h, norm_ffn[li]), w_up[li], w_down[li]).astype(h.dtype)
        y = rmsnorm(h, norm_final)
        ab = [jnp.stack(z) for z in zip(*ab_states)]
        cc = [jnp.stack(z) for z in zip(*c_states)]
        return y, ab, cc

    meta = jnp.broadcast_to(meta_tokens[None].astype(x_prompt.dtype), (x_prompt.shape[0], N_META, D_MODEL))
    y_p, ab_p, c_p = trunk(jnp.concatenate([meta, x_prompt], axis=1), False)
    y_s, ab_s, c_s = trunk(x_sample, True)
    return (y_p[:, N_META:], y_s,
            ab_p[0], ab_p[1], ab_p[2], ab_p[3], ab_p[4], c_p[0], c_p[1],
            ab_s[0], ab_s[1], ab_s[2], ab_s[3], ab_s[4], c_s[0], c_s[1])
```

```python
import functools
import math
import jax
import jax.numpy as jnp
from jax import lax
from jax.experimental import pallas as pl
from jax.experimental.pallas import tpu as pltpu

D_MODEL = 1024
BATCH = 8
SEQ = 2048
DEPTH = 2
DEC_BATCH = 32
DEC_SEQ = 1
PAST_LEN = 16384
PAGE_SIZE = 128

N_META = 16
EPS = 1e-6
H_A = 8
DK_A = 64
DV_A = 64
CONV_W = 4
CHUNK_A = 64
H_B = 8
DH_B = 64
H_IDX = 8
D_IDX = 64
TOPK_MAX = 256
REL_BUCKETS = 32
REL_MAX_DIST = 1024
H_C = 8
Q_LORA = 384
KV_LORA = 256
QK_NOPE = 64
QK_ROPE = 32
V_C = 128
ROPE_BASE = 10000.0
D_FF = 4 * D_MODEL
Q_BLOCK = 128

QKV_A = H_A * (2 * DK_A + DV_A)
AB_SIZES = (QKV_A, H_A, H_A, H_A * DV_A, H_B * DH_B, H_B * DH_B, H_B * DH_B, H_IDX * D_IDX, H_IDX, D_IDX)
AB_COLS = sum(AB_SIZES)
C_SIZES = (Q_LORA, KV_LORA, QK_ROPE)
C_COLS = sum(C_SIZES)
MLA_SCALE = (QK_NOPE + QK_ROPE) ** -0.5
N_AB = (DEPTH + 1) // 2
N_C = DEPTH // 2

BF16 = jnp.bfloat16
F32 = jnp.float32
VMEM_LIMIT = 56 * 1024 * 1024


def rmsnorm(x, w):
    xf = x.astype(jnp.float32)
    y = xf * lax.rsqrt(jnp.mean(xf * xf, axis=-1, keepdims=True) + EPS)
    return (y * w.astype(jnp.float32)).astype(x.dtype)


def l2norm(x):
    return x * lax.rsqrt(jnp.sum(x * x, axis=-1, keepdims=True) + EPS)


def split_cols(x, sizes):
    out, off = [], 0
    for s in sizes:
        out.append(x[..., off:off + s])
        off += s
    return out


def to_blocks(a, n_blocks):
    b, t = a.shape[:2]
    pad = n_blocks * Q_BLOCK - t
    a = jnp.pad(a, [(0, 0), (0, pad)] + [(0, 0)] * (a.ndim - 2))
    return jnp.moveaxis(a.reshape((b, n_blocks, Q_BLOCK) + a.shape[2:]), 1, 0)


def from_blocks(a, t):
    a = jnp.moveaxis(a, 0, 1)
    return a.reshape((a.shape[0], -1) + a.shape[3:])[:, :t]


def gather_pages(pool, page_table):
    g = pool[page_table]
    return g.reshape((g.shape[0], g.shape[1] * g.shape[2]) + g.shape[3:])


def gather_rows(pool, page_table, idx):
    db = idx.shape[0]
    page = idx // PAGE_SIZE
    off = idx % PAGE_SIZE
    phys = jnp.take_along_axis(page_table, page.reshape(db, -1), axis=1).reshape(idx.shape)
    return pool[phys, off]


def rel_bucket(dist):
    dist = jnp.maximum(dist, 0)
    max_exact = REL_BUCKETS // 2
    log_ratio = jnp.log(jnp.maximum(dist, 1).astype(jnp.float32) / max_exact) / math.log(REL_MAX_DIST / max_exact)
    large = max_exact + (log_ratio * (REL_BUCKETS - max_exact)).astype(jnp.int32)
    return jnp.where(dist < max_exact, dist, jnp.minimum(large, REL_BUCKETS - 1))


def rope(x, pos):
    half = QK_ROPE // 2
    inv = ROPE_BASE ** (-jnp.arange(half, dtype=jnp.float32) / half)
    ang = pos.astype(jnp.float32)[:, None] * inv[None, :]
    cos = jnp.cos(ang)[:, None, :]
    sin = jnp.sin(ang)[:, None, :]
    xf = x.astype(jnp.float32)
    x1, x2 = xf[..., :half], xf[..., half:]
    return jnp.concatenate([x1 * cos - x2 * sin, x2 * cos + x1 * sin], axis=-1).astype(x.dtype)


def causal_conv(xp, w, t):
    acc = xp[:, 0:t] * w[0]
    for j in range(1, CONV_W):
        acc = acc + xp[:, j:j + t] * w[j]
    return jax.nn.silu(acc)


def gdn_prep(qkv, alpha, braw, a_log, dt_bias):
    b, t = qkv.shape[:2]
    qkv = qkv.astype(jnp.float32)
    q, k, v = split_cols(qkv, (H_A * DK_A, H_A * DK_A, H_A * DV_A))
    q = l2norm(q.reshape(b, t, H_A, DK_A)) * DK_A ** -0.5
    k = l2norm(k.reshape(b, t, H_A, DK_A))
    v = v.reshape(b, t, H_A, DV_A)
    g = -jnp.exp(a_log.astype(jnp.float32)) * jax.nn.softplus(alpha.astype(jnp.float32) + dt_bias.astype(jnp.float32))
    beta = jax.nn.sigmoid(braw.astype(jnp.float32))
    return q, k, v, g, beta


def gdn_chunk(s, q, k, v, g, beta):
    c = q.shape[2]
    gcum = jnp.cumsum(g, axis=-1)
    incl = jnp.tril(jnp.ones((c, c), dtype=bool))
    strict = jnp.tril(jnp.ones((c, c), dtype=bool), -1)
    decay = jnp.exp(jnp.where(incl, gcum[..., :, None] - gcum[..., None, :], -jnp.inf))
    kk = jnp.einsum('bhid,bhjd->bhij', k, k)
    a_mat = jnp.where(strict, beta[..., :, None] * kk * decay, 0.0) + jnp.eye(c, dtype=q.dtype)
    rhs = jnp.concatenate([v * beta[..., None], k * (beta * jnp.exp(gcum))[..., None]], axis=-1)
    sol = lax.linalg.triangular_solve(a_mat, rhs, left_side=True, lower=True, unit_diagonal=True)
    u, w = sol[..., :DV_A], sol[..., DV_A:]
    v_new = u - jnp.einsum('bhck,bhkv->bhcv', w, s)
    qk = jnp.einsum('bhid,bhjd->bhij', q, k) * decay
    o = jnp.einsum('bhck,bhkv->bhcv', q * jnp.exp(gcum)[..., None], s) + jnp.einsum('bhij,bhjv->bhiv', qk, v_new)
    g_last = gcum[..., -1:]
    s_new = s * jnp.exp(g_last)[..., None] + jnp.einsum('bhck,bhcv->bhkv', k * jnp.exp(g_last - gcum)[..., None], v_new)
    return s_new, o


def gdn_step(s, xs):
    q, k, v, g, beta = xs
    s = s * jnp.exp(g)[..., None, None]
    delta = (v - jnp.einsum('bhk,bhkv->bhv', k, s)) * beta[..., None]
    s = s + jnp.einsum('bhk,bhv->bhkv', k, delta)
    return s, jnp.einsum('bhk,bhkv->bhv', q, s)


def gdn_prompt(q, k, v, g, beta):
    b, t = q.shape[:2]
    q, k, v, g, beta = (jnp.moveaxis(a, 1, 2) for a in (q, k, v, g, beta))
    s0 = jnp.zeros((b, H_A, DK_A, DV_A), jnp.float32)
    s1, o_meta = gdn_chunk(s0, q[:, :, :N_META], k[:, :, :N_META], v[:, :, :N_META], g[:, :, :N_META], beta[:, :, :N_META])
    n_chunks = (t - N_META) // CHUNK_A

    def to_chunks(a):
        a = a[:, :, N_META:]
        a = a.reshape(a.shape[:2] + (n_chunks, CHUNK_A) + a.shape[3:])
        return jnp.moveaxis(a, 2, 0)

    s_fin, o_chunks = lax.scan(lambda s, xs: gdn_chunk(s, *xs), s1, tuple(to_chunks(a) for a in (q, k, v, g, beta)))
    o_real = jnp.moveaxis(o_chunks, 0, 2).reshape(b, H_A, n_chunks * CHUNK_A, DV_A)
    o = jnp.concatenate([o_meta, o_real], axis=2)
    return jnp.moveaxis(o, 1, 2), s_fin


def gdn_decode(s, q, k, v, g, beta):
    xs = tuple(jnp.moveaxis(a, 1, 0) for a in (q, k, v, g, beta))
    s_fin, o = lax.scan(gdn_step, s.astype(jnp.float32), xs)
    return jnp.moveaxis(o, 0, 1), s_fin


def index_scores(qi, wi, ki):
    s = jax.nn.relu(jnp.einsum('bthd,bsd->bths', qi, ki).astype(jnp.float32) * D_IDX ** -0.5)
    return jnp.einsum('bths,bth->bts', s, wi.astype(jnp.float32) * H_IDX ** -0.5)


def sparse_attend(q, k_sel, v_sel, valid, dist, rel_bias):
    logits = jnp.einsum('bthd,btkhd->bthk', q, k_sel).astype(jnp.float32) * DH_B ** -0.5
    bias = rel_bias[rel_bucket(dist)]
    logits = logits + jnp.moveaxis(bias, -1, -2).astype(jnp.float32)
    logits = jnp.where(valid[:, :, None, :], logits, -jnp.inf)
    p = jax.nn.softmax(logits, axis=-1).astype(v_sel.dtype)
    return jnp.einsum('bthk,btkhd->bthd', p, v_sel)


def gather_seq(rows, idx):
    return jax.vmap(lambda rb, ib: rb[ib])(rows, idx)


def dsa_prompt(q, k, v, qi, wi, ki, rel_bias):
    b, t = q.shape[:2]
    topk = min(TOPK_MAX, SEQ // 4)
    nb = -(-t // Q_BLOCK)
    pos_blocks = jnp.arange(nb * Q_BLOCK, dtype=jnp.int32).reshape(nb, Q_BLOCK)
    key_pos = jnp.arange(t, dtype=jnp.int32)

    def one_block(xs):
        qb, qib, wib, pos = xs
        sc = index_scores(qib, wib, ki)
        sc = jnp.where((key_pos[None, :] <= pos[:, None])[None], sc, -jnp.inf)
        _, idx = lax.top_k(sc, topk)
        valid = idx <= pos[None, :, None]
        dist = pos[None, :, None] - idx
        return sparse_attend(qb, gather_seq(k, idx), gather_seq(v, idx), valid, dist, rel_bias)

    o = lax.map(one_block, (to_blocks(q, nb), to_blocks(qi, nb), to_blocks(wi, nb), pos_blocks))
    return from_blocks(o, t)


def dsa_decode(q, k_new, v_new, qi, wi, ki_new, cache_k, cache_v, cache_ki, page_table, rel_bias):
    db, t = q.shape[:2]
    n_keys = PAST_LEN + t
    topk = min(TOPK_MAX, n_keys // 4)
    ki_all = jnp.concatenate([gather_pages(cache_ki, page_table).astype(ki_new.dtype), ki_new], axis=1)
    pos = PAST_LEN + jnp.arange(t, dtype=jnp.int32)
    sc = index_scores(qi, wi, ki_all)
    key_pos = jnp.arange(n_keys, dtype=jnp.int32)
    sc = jnp.where((key_pos[None, :] <= pos[:, None])[None], sc, -jnp.inf)
    _, idx = lax.top_k(sc, topk)
    valid = idx <= pos[None, :, None]
    in_past = (idx < PAST_LEN)[..., None, None]
    past_idx = jnp.minimum(idx, PAST_LEN - 1)
    new_idx = jnp.clip(idx - PAST_LEN, 0, t - 1)
    k_sel = jnp.where(in_past, gather_rows(cache_k, page_table, past_idx).astype(k_new.dtype), gather_seq(k_new, new_idx))
    v_sel = jnp.where(in_past, gather_rows(cache_v, page_table, past_idx).astype(v_new.dtype), gather_seq(v_new, new_idx))
    dist = pos[None, :, None] - idx
    return sparse_attend(q, k_sel, v_sel, valid, dist, rel_bias)


def mixer_ab(hn, w_in, conv_w, a_log, dt_bias, gnorm, w_out, rel_bias, past):
    b, t, _ = hn.shape
    qkv_a, alpha, braw, gate, q_b, k_b, v_b, qi, wi, ki = split_cols(hn @ w_in, AB_SIZES)
    if past is None:
        conv_buf = jnp.zeros((b, CONV_W - 1, QKV_A), qkv_a.dtype)
    else:
        conv_buf, s_prev, cache_k, cache_v, cache_ki, page_table = past
    xp = jnp.concatenate([conv_buf.astype(qkv_a.dtype), qkv_a], axis=1)
    q, k, v, g, beta = gdn_prep(causal_conv(xp, conv_w, t), alpha, braw, a_log, dt_bias)
    if past is None:
        o_a, s_new = gdn_prompt(q, k, v, g, beta)
    else:
        o_a, s_new = gdn_decode(s_prev, q, k, v, g, beta)
    o_a = rmsnorm(o_a, gnorm) * jax.nn.silu(gate.reshape(b, t, H_A, DV_A).astype(jnp.float32))
    o_a = o_a.reshape(b, t, H_A * DV_A).astype(hn.dtype)
    q_b = q_b.reshape(b, t, H_B, DH_B)
    k_b = k_b.reshape(b, t, H_B, DH_B)
    v_b = v_b.reshape(b, t, H_B, DH_B)
    qi = qi.reshape(b, t, H_IDX, D_IDX)
    if past is None:
        o_b = dsa_prompt(q_b, k_b, v_b, qi, wi, ki, rel_bias)
    else:
        o_b = dsa_decode(q_b, k_b, v_b, qi, wi, ki, cache_k, cache_v, cache_ki, page_table, rel_bias)
    out = jnp.concatenate([o_a, o_b.reshape(b, t, H_B * DH_B).astype(hn.dtype)], axis=-1) @ w_out
    return out, (xp[:, -(CONV_W - 1):], s_new, k_b, v_b, ki)


def mla_core(q_lat, q_pe, c, kpe, visible):
    logits = (jnp.einsum('bthc,bsc->bhts', q_lat, c) + jnp.einsum('bthr,bsr->bhts', q_pe, kpe)).astype(jnp.float32) * MLA_SCALE
    logits = jnp.where(visible[None, None], logits, -jnp.inf)
    p = jax.nn.softmax(logits, axis=-1).astype(c.dtype)
    return jnp.einsum('bhts,bsc->bthc', p, c)


def mla_prompt(q_lat, q_pe, c, kpe):
    t = q_lat.shape[1]
    nb = -(-t // Q_BLOCK)
    pos_blocks = jnp.arange(nb * Q_BLOCK, dtype=jnp.int32).reshape(nb, Q_BLOCK)
    key_pos = jnp.arange(t, dtype=jnp.int32)

    def one_block(xs):
        ql, qp, pos = xs
        return mla_core(ql, qp, c, kpe, key_pos[None, :] <= pos[:, None])

    o = lax.map(one_block, (to_blocks(q_lat, nb), to_blocks(q_pe, nb), pos_blocks))
    return from_blocks(o, t)


def mixer_c(hn, w_in, qnorm, kvnorm, w_uq, w_uk, w_uv, w_out, past):
    b, t, _ = hn.shape
    cq, ckv, kpe = split_cols(hn @ w_in, C_SIZES)
    offset = 0 if past is None else PAST_LEN
    pos = offset + jnp.arange(t, dtype=jnp.int32)
    qf = (rmsnorm(cq, qnorm) @ w_uq).reshape(b, t, H_C, QK_NOPE + QK_ROPE)
    q_pe = rope(qf[..., QK_NOPE:], pos)
    q_lat = jnp.einsum('bthn,chn->bthc', qf[..., :QK_NOPE], w_uk)
    c = rmsnorm(ckv, kvnorm)
    kpe = rope(kpe[:, :, None, :], pos)[:, :, 0, :]
    if past is None:
        o_lat = mla_prompt(q_lat, q_pe, c, kpe)
    else:
        cache_c, cache_kpe, page_table = past
        c_all = jnp.concatenate([gather_pages(cache_c, page_table).astype(c.dtype), c], axis=1)
        kpe_all = jnp.concatenate([gather_pages(cache_kpe, page_table).astype(kpe.dtype), kpe], axis=1)
        key_pos = jnp.arange(PAST_LEN + t, dtype=jnp.int32)
        o_lat = mla_core(q_lat, q_pe, c_all, kpe_all, key_pos[None, :] <= pos[:, None])
    o = jnp.einsum('bthc,chv->bthv', o_lat, w_uv).reshape(b, t, H_C * V_C)
    return o @ w_out, (c, kpe)


def _rms(x, w):
    return x * lax.rsqrt(jnp.mean(x * x, axis=-1, keepdims=True) + EPS) * w


def _proj_mlp_kernel(n_proj, final, *refs):
    h_ref = refs[0]
    a_refs = refs[1:1 + n_proj]
    w_refs = refs[1 + n_proj:1 + 2 * n_proj]
    nw_ref, wup_ref, wdn_ref = refs[1 + 2 * n_proj:4 + 2 * n_proj]
    pos = 4 + 2 * n_proj
    nf_ref = refs[pos] if final else None
    pos += 1 if final else 0
    o_ref, h1_s, xn_s, acc_s = refs[pos:pos + 4]
    f = pl.program_id(1)

    @pl.when(f == 0)
    def _():
        h1 = h_ref[...]
        for a_ref, w_ref in zip(a_refs, w_refs):
            h1 = h1 + jnp.dot(a_ref[...].astype(BF16), w_ref[...], preferred_element_type=F32)
        h1_s[...] = h1
        xn_s[...] = _rms(h1, nw_ref[...]).astype(BF16)
        acc_s[...] = jnp.zeros_like(acc_s)

    u = jnp.dot(xn_s[...], wup_ref[...], preferred_element_type=F32)
    u = jnp.square(jnp.maximum(u, 0.0)).astype(BF16)
    acc_s[...] += jnp.dot(u, wdn_ref[...], preferred_element_type=F32)

    @pl.when(f == pl.num_programs(1) - 1)
    def _():
        h2 = h1_s[...] + acc_s[...]
        if final:
            h2 = _rms(h2, nf_ref[...])
        o_ref[...] = h2


def proj_mlp(h, projs, norm_w, w_up, w_down, norm_final=None, tm=512, tf=1024):
    n, d = h.shape
    ff = w_up.shape[1]
    tm = min(tm, n)
    n_proj = len(projs)
    final = norm_final is not None
    row = lambda i, f: (i, 0)
    const = lambda i, f: (0, 0)
    in_specs = [pl.BlockSpec((tm, d), row)]
    in_specs += [pl.BlockSpec((tm, a.shape[1]), row) for a, _ in projs]
    in_specs += [pl.BlockSpec(w.shape, const) for _, w in projs]
    in_specs += [pl.BlockSpec((1, d), const), pl.BlockSpec((d, tf), lambda i, f: (0, f)),
                 pl.BlockSpec((tf, d), lambda i, f: (f, 0))]
    args = [h] + [a for a, _ in projs] + [w for _, w in projs] + [norm_w.reshape(1, d), w_up, w_down]
    if final:
        in_specs.append(pl.BlockSpec((1, d), const))
        args.append(norm_final.reshape(1, d))
    return pl.pallas_call(
        functools.partial(_proj_mlp_kernel, n_proj, final),
        grid=(pl.cdiv(n, tm), ff // tf),
        in_specs=in_specs,
        out_specs=pl.BlockSpec((tm, d), row),
        out_shape=jax.ShapeDtypeStruct((n, d), F32),
        scratch_shapes=[pltpu.VMEM((tm, d), F32), pltpu.VMEM((tm, d), BF16), pltpu.VMEM((tm, d), F32)],
        compiler_params=pltpu.CompilerParams(dimension_semantics=("parallel", "arbitrary"),
                                             vmem_limit_bytes=VMEM_LIMIT),
        name="proj_mlp",
    )(*args)


def kernel(x_prompt, x_sample, state_a_conv, state_a_ssm, cache_b_k, cache_b_v, cache_b_kidx,
           cache_c_latent, cache_c_kpe, page_table, meta_tokens, rel_bias, norm_mix, norm_ffn, norm_final,
           w_in_ab, conv_a, a_log, dt_bias_a, gnorm_a, w_out_ab, w_in_c, qnorm_c, kvnorm_c,
           w_uq_c, w_uk_c, w_uv_c, w_out_c, w_up, w_down):
    w_up_b = w_up.astype(BF16)
    w_down_b = w_down.astype(BF16)

    def trunk(h, sample):
        ab_states, c_states = [], []
        b, t, _ = h.shape
        for li in range(DEPTH):
            j = li // 2
            hn = rmsnorm(h, norm_mix[li])
            if li % 2 == 0:
                past = (state_a_conv[j], state_a_ssm[j], cache_b_k[j], cache_b_v[j], cache_b_kidx[j], page_table) if sample else None
                out, st = mixer_ab(hn, w_in_ab[j], conv_a[j], a_log[j], dt_bias_a[j], gnorm_a[j], w_out_ab[j], rel_bias, past)
                ab_states.append(st)
            else:
                past = (cache_c_latent[j], cache_c_kpe[j], page_table) if sample else None
                out, st = mixer_c(hn, w_in_c[j], qnorm_c[j], kvnorm_c[j], w_uq_c[j], w_uk_c[j], w_uv_c[j], w_out_c[j], past)
                c_states.append(st)
            h = h + out.astype(h.dtype)
            last = li == DEPTH - 1
            h = proj_mlp(h.reshape(b * t, D_MODEL), [], norm_ffn[li], w_up_b[li], w_down_b[li],
                         norm_final if last else None).reshape(b, t, D_MODEL)
        ab = [jnp.stack(z) for z in zip(*ab_states)]
        cc = [jnp.stack(z) for z in zip(*c_states)]
        return h, ab, cc

    meta = jnp.broadcast_to(meta_tokens[None].astype(x_prompt.dtype), (x_prompt.shape[0], N_META, D_MODEL))
    y_p, ab_p, c_p = trunk(jnp.concatenate([meta, x_prompt], axis=1), False)
    y_s, ab_s, c_s = trunk(x_sample, True)
    return (y_p[:, N_META:], y_s,
            ab_p[0], ab_p[1], ab_p[2], ab_p[3], ab_p[4], c_p[0], c_p[1],
            ab_s[0], ab_s[1], ab_s[2], ab_s[3], ab_s[4], c_s[0], c_s[1])
```

```python
import functools
import math
import jax
import jax.numpy as jnp
from jax import lax
from jax.experimental import pallas as pl
from jax.experimental.pallas import tpu as pltpu

D_MODEL = 1024
BATCH = 8
SEQ = 2048
DEPTH = 2
DEC_BATCH = 32
DEC_SEQ = 1
PAST_LEN = 16384
PAGE_SIZE = 128

N_META = 16
EPS = 1e-6
H_A = 8
DK_A = 64
DV_A = 64
CONV_W = 4
H_B = 8
DH_B = 64
H_IDX = 8
D_IDX = 64
TOPK_MAX = 256
REL_BUCKETS = 32
REL_MAX_DIST = 1024
H_C = 8
Q_LORA = 384
KV_LORA = 256
QK_NOPE = 64
QK_ROPE = 32
V_C = 128
ROPE_BASE = 10000.0
D_FF = 4 * D_MODEL
Q_BLOCK = 128

QKV_A = H_A * (2 * DK_A + DV_A)
AB_SIZES = (QKV_A, H_A, H_A, H_A * DV_A, H_B * DH_B, H_B * DH_B, H_B * DH_B, H_IDX * D_IDX, H_IDX, D_IDX)
AB_COLS = sum(AB_SIZES)
C_SIZES = (Q_LORA, KV_LORA, QK_ROPE)
C_COLS = sum(C_SIZES)
MLA_SCALE = (QK_NOPE + QK_ROPE) ** -0.5
N_AB = (DEPTH + 1) // 2
N_C = DEPTH // 2

BF16 = jnp.bfloat16
F32 = jnp.float32
LANE = 128
VMEM_LIMIT = 56 * 1024 * 1024
INT_MIN = -(2 ** 31)
NT_DIMS = (((1,), (1,)), ((), ()))
TN_DIMS = (((0,), (0,)), ((), ()))
HI = lax.Precision.HIGHEST
HD = H_A * DK_A
HALF = QK_ROPE // 2

COL_QKV, COL_GATE, COL_QB, COL_KB, COL_VB, COL_QI, COL_SMALL = 0, 1536, 2048, 2560, 3072, 3584, 4096
AB_COLS_P = COL_SMALL + LANE
SM_ALPHA, SM_BRAW, SM_WI, SM_KI = 0, 8, 16, 64
CC_Q, CC_KV, CC_K1, CC_K2, C_COLS_P = 0, 384, 640, 768, 896
KVW = KV_LORA + 2 * LANE
N_BIAS_TILES = 9


def rmsnorm(x, w):
    xf = x.astype(jnp.float32)
    y = xf * lax.rsqrt(jnp.mean(xf * xf, axis=-1, keepdims=True) + EPS)
    return (y * w.astype(jnp.float32)).astype(x.dtype)


def l2norm(x):
    return x * lax.rsqrt(jnp.sum(x * x, axis=-1, keepdims=True) + EPS)


def split_cols(x, sizes):
    out, off = [], 0
    for s in sizes:
        out.append(x[..., off:off + s])
        off += s
    return out


def gather_pages(pool, page_table):
    g = pool[page_table]
    return g.reshape((g.shape[0], g.shape[1] * g.shape[2]) + g.shape[3:])


def gather_rows(pool, page_table, idx):
    db = idx.shape[0]
    page = idx // PAGE_SIZE
    off = idx % PAGE_SIZE
    phys = jnp.take_along_axis(page_table, page.reshape(db, -1), axis=1).reshape(idx.shape)
    return pool[phys, off]


def rel_bucket(dist):
    dist = jnp.maximum(dist, 0)
    max_exact = REL_BUCKETS // 2
    log_ratio = jnp.log(jnp.maximum(dist, 1).astype(jnp.float32) / max_exact) / math.log(REL_MAX_DIST / max_exact)
    large = max_exact + (log_ratio * (REL_BUCKETS - max_exact)).astype(jnp.int32)
    return jnp.where(dist < max_exact, dist, jnp.minimum(large, REL_BUCKETS - 1))


def rope(x, pos):
    inv = ROPE_BASE ** (-jnp.arange(HALF, dtype=jnp.float32) / HALF)
    ang = pos.astype(jnp.float32)[:, None] * inv[None, :]
    cos = jnp.cos(ang)[:, None, :]
    sin = jnp.sin(ang)[:, None, :]
    xf = x.astype(jnp.float32)
    x1, x2 = xf[..., :HALF], xf[..., HALF:]
    return jnp.concatenate([x1 * cos - x2 * sin, x2 * cos + x1 * sin], axis=-1).astype(x.dtype)


def causal_conv(xp, w, t):
    acc = xp[:, 0:t] * w[0]
    for j in range(1, CONV_W):
        acc = acc + xp[:, j:j + t] * w[j]
    return jax.nn.silu(acc)


def gdn_prep(qkv, alpha, braw, a_log, dt_bias):
    b, t = qkv.shape[:2]
    qkv = qkv.astype(jnp.float32)
    q, k, v = split_cols(qkv, (H_A * DK_A, H_A * DK_A, H_A * DV_A))
    q = l2norm(q.reshape(b, t, H_A, DK_A)) * DK_A ** -0.5
    k = l2norm(k.reshape(b, t, H_A, DK_A))
    v = v.reshape(b, t, H_A, DV_A)
    g = -jnp.exp(a_log.astype(jnp.float32)) * jax.nn.softplus(alpha.astype(jnp.float32) + dt_bias.astype(jnp.float32))
    beta = jax.nn.sigmoid(braw.astype(jnp.float32))
    return q, k, v, g, beta


def gdn_step(s, xs):
    q, k, v, g, beta = xs
    s = s * jnp.exp(g)[..., None, None]
    delta = (v - jnp.einsum('bhk,bhkv->bhv', k, s)) * beta[..., None]
    s = s + jnp.einsum('bhk,bhv->bhkv', k, delta)
    return s, jnp.einsum('bhk,bhkv->bhv', q, s)


def gdn_decode(s, q, k, v, g, beta):
    xs = tuple(jnp.moveaxis(a, 1, 0) for a in (q, k, v, g, beta))
    s_fin, o = lax.scan(gdn_step, s.astype(jnp.float32), xs)
    return jnp.moveaxis(o, 0, 1), s_fin


def index_scores(qi, wi, ki):
    s = jax.nn.relu(jnp.einsum('bthd,bsd->bths', qi, ki).astype(jnp.float32) * D_IDX ** -0.5)
    return jnp.einsum('bths,bth->bts', s, wi.astype(jnp.float32) * H_IDX ** -0.5)


def sparse_attend(q, k_sel, v_sel, valid, dist, rel_bias):
    logits = jnp.einsum('bthd,btkhd->bthk', q, k_sel).astype(jnp.float32) * DH_B ** -0.5
    bias = rel_bias[rel_bucket(dist)]
    logits = logits + jnp.moveaxis(bias, -1, -2).astype(jnp.float32)
    logits = jnp.where(valid[:, :, None, :], logits, -jnp.inf)
    p = jax.nn.softmax(logits, axis=-1).astype(v_sel.dtype)
    return jnp.einsum('bthk,btkhd->bthd', p, v_sel)


def gather_seq(rows, idx):
    return jax.vmap(lambda rb, ib: rb[ib])(rows, idx)


def dsa_decode(q, k_new, v_new, qi, wi, ki_new, cache_k, cache_v, cache_ki, page_table, rel_bias):
    db, t = q.shape[:2]
    n_keys = PAST_LEN + t
    topk = min(TOPK_MAX, n_keys // 4)
    ki_all = jnp.concatenate([gather_pages(cache_ki, page_table).astype(ki_new.dtype), ki_new], axis=1)
    pos = PAST_LEN + jnp.arange(t, dtype=jnp.int32)
    sc = index_scores(qi, wi, ki_all)
    key_pos = jnp.arange(n_keys, dtype=jnp.int32)
    sc = jnp.where((key_pos[None, :] <= pos[:, None])[None], sc, -jnp.inf)
    _, idx = lax.top_k(sc, topk)
    valid = idx <= pos[None, :, None]
    in_past = (idx < PAST_LEN)[..., None, None]
    past_idx = jnp.minimum(idx, PAST_LEN - 1)
    new_idx = jnp.clip(idx - PAST_LEN, 0, t - 1)
    k_sel = jnp.where(in_past, gather_rows(cache_k, page_table, past_idx).astype(k_new.dtype), gather_seq(k_new, new_idx))
    v_sel = jnp.where(in_past, gather_rows(cache_v, page_table, past_idx).astype(v_new.dtype), gather_seq(v_new, new_idx))
    dist = pos[None, :, None] - idx
    return sparse_attend(q, k_sel, v_sel, valid, dist, rel_bias)


def mixer_ab_decode(hn, w_in, conv_w, a_log, dt_bias, gnorm, w_out, rel_bias, past):
    b, t, _ = hn.shape
    qkv_a, alpha, braw, gate, q_b, k_b, v_b, qi, wi, ki = split_cols(hn @ w_in, AB_SIZES)
    conv_buf, s_prev, cache_k, cache_v, cache_ki, page_table = past
    xp = jnp.concatenate([conv_buf.astype(qkv_a.dtype), qkv_a], axis=1)
    q, k, v, g, beta = gdn_prep(causal_conv(xp, conv_w, t), alpha, braw, a_log, dt_bias)
    o_a, s_new = gdn_decode(s_prev, q, k, v, g, beta)
    o_a = rmsnorm(o_a, gnorm) * jax.nn.silu(gate.reshape(b, t, H_A, DV_A).astype(jnp.float32))
    o_a = o_a.reshape(b, t, H_A * DV_A).astype(hn.dtype)
    q_b = q_b.reshape(b, t, H_B, DH_B)
    k_b = k_b.reshape(b, t, H_B, DH_B)
    v_b = v_b.reshape(b, t, H_B, DH_B)
    qi = qi.reshape(b, t, H_IDX, D_IDX)
    o_b = dsa_decode(q_b, k_b, v_b, qi, wi, ki, cache_k, cache_v, cache_ki, page_table, rel_bias)
    out = jnp.concatenate([o_a, o_b.reshape(b, t, H_B * DH_B).astype(hn.dtype)], axis=-1) @ w_out
    return out, (xp[:, -(CONV_W - 1):], s_new, k_b, v_b, ki)


def mla_core(q_lat, q_pe, c, kpe, visible):
    logits = (jnp.einsum('bthc,bsc->bhts', q_lat, c) + jnp.einsum('bthr,bsr->bhts', q_pe, kpe)).astype(jnp.float32) * MLA_SCALE
    logits = jnp.where(visible[None, None], logits, -jnp.inf)
    p = jax.nn.softmax(logits, axis=-1).astype(c.dtype)
    return jnp.einsum('bhts,bsc->bthc', p, c)


def mixer_c_decode(hn, w_in, qnorm, kvnorm, w_uq, w_uk, w_uv, w_out, past):
    b, t, _ = hn.shape
    cq, ckv, kpe = split_cols(hn @ w_in, C_SIZES)
    pos = PAST_LEN + jnp.arange(t, dtype=jnp.int32)
    qf = (rmsnorm(cq, qnorm) @ w_uq).reshape(b, t, H_C, QK_NOPE + QK_ROPE)
    q_pe = rope(qf[..., QK_NOPE:], pos)
    q_lat = jnp.einsum('bthn,chn->bthc', qf[..., :QK_NOPE], w_uk)
    c = rmsnorm(ckv, kvnorm)
    kpe = rope(kpe[:, :, None, :], pos)[:, :, 0, :]
    cache_c, cache_kpe, page_table = past
    c_all = jnp.concatenate([gather_pages(cache_c, page_table).astype(c.dtype), c], axis=1)
    kpe_all = jnp.concatenate([gather_pages(cache_kpe, page_table).astype(kpe.dtype), kpe], axis=1)
    key_pos = jnp.arange(PAST_LEN + t, dtype=jnp.int32)
    o_lat = mla_core(q_lat, q_pe, c_all, kpe_all, key_pos[None, :] <= pos[:, None])
    o = jnp.einsum('bthc,chv->bthv', o_lat, w_uv).reshape(b, t, H_C * V_C)
    return o @ w_out, (c, kpe)


def _rms(x, w):
    return x * lax.rsqrt(jnp.mean(x * x, axis=-1, keepdims=True) + EPS) * w


def _hdot(a, b):
    return jnp.dot(a, b, precision=HI, preferred_element_type=F32)


def _dot_exact_rhs(a, b_bf16):
    a_hi = a.astype(BF16)
    a_lo = (a - a_hi.astype(F32)).astype(BF16)
    return jnp.dot(a_hi, b_bf16, preferred_element_type=F32) + jnp.dot(a_lo, b_bf16, preferred_element_type=F32)


def _causal_variants(nq):
    n_var = min(4, nq)
    his = [nq - (n_var - 1 - v) * (nq // n_var) for v in range(n_var)]
    return [(hi, hi * LANE) for hi in his]


def _dispatch_variants(variants, body, *args):
    i = pl.program_id(1)
    lo = 0
    for hi, lc in variants:
        pl.when((i >= lo) & (i < hi))(functools.partial(body, lc, *args))
        lo = hi


def _rms_linear_kernel(x_ref, nw_ref, w_ref, o_ref):
    xn = _rms(x_ref[...], nw_ref[...]).astype(BF16)
    o_ref[...] = jnp.dot(xn, w_ref[...], preferred_element_type=F32)


def rms_linear(x, norm_w, w, tm=256):
    n, d = x.shape
    m = w.shape[1]
    tm = min(tm, n)
    return pl.pallas_call(
        _rms_linear_kernel,
        grid=(pl.cdiv(n, tm),),
        in_specs=[pl.BlockSpec((tm, d), lambda i: (i, 0)), pl.BlockSpec((1, d), lambda i: (0, 0)),
                  pl.BlockSpec((d, m), lambda i: (0, 0))],
        out_specs=pl.BlockSpec((tm, m), lambda i: (i, 0)),
        out_shape=jax.ShapeDtypeStruct((n, m), F32),
        compiler_params=pltpu.CompilerParams(dimension_semantics=("parallel",), vmem_limit_bytes=VMEM_LIMIT),
        name="rms_linear",
    )(x, norm_w.reshape(1, d), w)


def _proj_mlp_kernel(n_proj, n_heads, final, *refs):
    refs = list(refs)
    h_ref = refs.pop(0)
    a_refs = [refs.pop(0) for _ in range(n_proj)]
    w_refs = [refs.pop(0) for _ in range(n_proj)]
    hw_ref = refs.pop(0) if n_heads else None
    nw_ref, wup_ref, wdn_ref = refs.pop(0), refs.pop(0), refs.pop(0)
    nf_ref = refs.pop(0) if final else None
    o_ref, h1_s, xn_s, acc_s = refs
    f = pl.program_id(1)

    @pl.when(f == 0)
    def _():
        h1 = h_ref[...]
        for a_ref, w_ref in zip(a_refs, w_refs):
            a = a_ref[...]
            if n_heads:
                kh = a.shape[1] // n_heads
                a = jnp.concatenate([jnp.dot(a[:, hh * kh:(hh + 1) * kh].astype(BF16), hw_ref[hh],
                                             preferred_element_type=F32) for hh in range(n_heads)], axis=1)
            h1 = h1 + jnp.dot(a.astype(BF16), w_ref[...], preferred_element_type=F32)
        h1_s[...] = h1
        xn_s[...] = _rms(h1, nw_ref[...]).astype(BF16)
        acc_s[...] = jnp.zeros_like(acc_s)

    u = jnp.dot(xn_s[...], wup_ref[...], preferred_element_type=F32)
    u = jnp.square(jnp.maximum(u, 0.0)).astype(BF16)
    acc_s[...] += jnp.dot(u, wdn_ref[...], preferred_element_type=F32)

    @pl.when(f == pl.num_programs(1) - 1)
    def _():
        h2 = h1_s[...] + acc_s[...]
        if final:
            h2 = _rms(h2, nf_ref[...])
        o_ref[...] = h2


def proj_mlp(h, projs, norm_w, w_up, w_down, norm_final=None, head_w=None, tm=512, tf=1024):
    n, d = h.shape
    ff = w_up.shape[1]
    tm = min(tm, n)
    final = norm_final is not None
    n_heads = 0 if head_w is None else head_w.shape[0]
    row = lambda i, f: (i, 0)
    const = lambda i, f: (0, 0)
    in_specs = [pl.BlockSpec((tm, d), row)]
    in_specs += [pl.BlockSpec((tm, a.shape[1]), row) for a, _ in projs]
    in_specs += [pl.BlockSpec(w.shape, const) for _, w in projs]
    args = [h] + [a for a, _ in projs] + [w for _, w in projs]
    if n_heads:
        in_specs.append(pl.BlockSpec(head_w.shape, lambda i, f: (0, 0, 0)))
        args.append(head_w)
    in_specs += [pl.BlockSpec((1, d), const), pl.BlockSpec((d, tf), lambda i, f: (0, f)),
                 pl.BlockSpec((tf, d), lambda i, f: (f, 0))]
    args += [norm_w.reshape(1, d), w_up, w_down]
    if final:
        in_specs.append(pl.BlockSpec((1, d), const))
        args.append(norm_final.reshape(1, d))
    return pl.pallas_call(
        functools.partial(_proj_mlp_kernel, len(projs), n_heads, final),
        grid=(pl.cdiv(n, tm), ff // tf),
        in_specs=in_specs,
        out_specs=pl.BlockSpec((tm, d), row),
        out_shape=jax.ShapeDtypeStruct((n, d), F32),
        scratch_shapes=[pltpu.VMEM((tm, d), F32), pltpu.VMEM((tm, d), BF16), pltpu.VMEM((tm, d), F32)],
        compiler_params=pltpu.CompilerParams(dimension_semantics=("parallel", "arbitrary"),
                                             vmem_limit_bytes=VMEM_LIMIT),
        name="proj_mlp",
    )(*args)


def _gdn_kernel(c_sz, pad, qkv_ref, small_ref, gate_ref, convw_ref, alog_ref, dtb_ref, gnorm_ref,
                o_ref, sfin_ref, s_s, prev_s):
    c = pl.program_id(1)

    @pl.when(c == 0)
    def _():
        s_s[...] = jnp.zeros_like(s_s)
        prev_s[...] = jnp.zeros_like(prev_s)

    row = lax.broadcasted_iota(jnp.int32, (c_sz, 1), 0)
    valid = (row + c * c_sz) >= pad
    x = jnp.where(valid, qkv_ref[0], 0.0)
    prevx = prev_s[...]
    acc = x * convw_ref[CONV_W - 1:CONV_W, :]
    for sft in range(1, CONV_W):
        shifted = jnp.where(row < sft, pltpu.roll(prevx, sft, 0), pltpu.roll(x, sft, 0))
        acc = acc + shifted * convw_ref[CONV_W - 1 - sft:CONV_W - sft, :]
    prev_s[...] = x
    y = acc * jax.nn.sigmoid(acc)
    q, k, v = y[:, :HD], y[:, HD:2 * HD], y[:, 2 * HD:]

    gi = lax.broadcasted_iota(jnp.int32, (HD, HD), 0) // DK_A
    gj = lax.broadcasted_iota(jnp.int32, (HD, HD), 1) // DK_A
    gmat = jnp.where(gi == gj, 1.0, 0.0).astype(BF16)
    q = q * lax.rsqrt(_dot_exact_rhs(q * q, gmat) + EPS) * (DK_A ** -0.5)
    k = k * lax.rsqrt(_dot_exact_rhs(k * k, gmat) + EPS)

    sm = small_ref[0]
    zs = sm + dtb_ref[...]
    g = -jnp.exp(alog_ref[...]) * (jnp.maximum(zs, 0.0) + jnp.log(1.0 + jnp.exp(-jnp.abs(zs))))
    g = jnp.where(valid, g, 0.0)
    beta = jnp.where(valid, jax.nn.sigmoid(sm), 0.0)
    ti = lax.broadcasted_iota(jnp.int32, (c_sz, c_sz), 0)
    tj = lax.broadcasted_iota(jnp.int32, (c_sz, c_sz), 1)
    incl = ti >= tj
    strict = ti > tj
    gc = _hdot(jnp.where(incl, 1.0, 0.0), g)
    gct = gc.T
    ei = lax.broadcasted_iota(jnp.int32, (LANE, HD), 0)
    ej = lax.broadcasted_iota(jnp.int32, (LANE, HD), 1) // DK_A
    gc_w = _hdot(gc, jnp.where(ei == ej + SM_ALPHA, 1.0, 0.0))
    beta_w = _dot_exact_rhs(beta, jnp.where(ei == ej + SM_BRAW, 1.0, 0.0).astype(BF16))
    gl_w = gc_w[c_sz - 1:c_sz, :]
    eg_w = jnp.exp(gc_w)
    q_s = q * eg_w
    k_b = k * (beta_w * eg_w)
    v_b = v * beta_w
    k_l = k * jnp.exp(gl_w - gc_w)
    s_old = s_s[...]
    n_fac = max(1, (c_sz - 1).bit_length())
    outs, s_upd = [], []
    for h in range(H_A):
        sl = slice(h * DK_A, (h + 1) * DK_A)
        kh, qh = k[:, sl], q[:, sl]
        diff = gc[:, SM_ALPHA + h:SM_ALPHA + h + 1] - gct[SM_ALPHA + h:SM_ALPHA + h + 1, :]
        decay = jnp.exp(jnp.where(incl, diff, -1e30))
        kk = lax.dot_general(kh, kh, NT_DIMS, precision=HI, preferred_element_type=F32)
        m = -jnp.where(strict, beta[:, SM_BRAW + h:SM_BRAW + h + 1] * kk * decay, 0.0)
        xs = jnp.concatenate([v_b[:, sl], k_b[:, sl]], axis=1)
        for s in range(n_fac):
            xs = xs + _hdot(m, xs)
            if s < n_fac - 1:
                m = _hdot(m, m)
        sh = s_old[:, sl]
        v_new = xs[:, :DV_A] - _hdot(xs[:, DV_A:], sh)
        qk = lax.dot_general(qh, kh, NT_DIMS, precision=HI, preferred_element_type=F32) * decay
        outs.append(_hdot(q_s[:, sl], sh) + _hdot(qk, v_new))
        s_upd.append(lax.dot_general(k_l[:, sl], v_new, TN_DIMS, precision=HI, preferred_element_type=F32))
    s_new = s_old * jnp.exp(gl_w) + jnp.concatenate(s_upd, axis=1)
    s_s[...] = s_new
    o = jnp.concatenate(outs, axis=1)
    ms = _dot_exact_rhs(o * o, gmat) * (1.0 / DV_A)
    gate = gate_ref[0]
    o_ref[0] = (o * lax.rsqrt(ms + EPS) * gnorm_ref[...] * (gate * jax.nn.sigmoid(gate))).astype(o_ref.dtype)

    @pl.when(c == pl.num_programs(1) - 1)
    def _():
        for h in range(H_A):
            sfin_ref[0, h] = s_new[:, h * DV_A:(h + 1) * DV_A]


def gdn_prompt(proj, conv_w, a_log, dt_bias, gnorm, pad, c_sz=LANE):
    b, l, _ = proj.shape
    alog_p = jnp.zeros((1, LANE), F32).at[0, SM_ALPHA:SM_ALPHA + H_A].set(a_log)
    dtb_p = jnp.zeros((1, LANE), F32).at[0, SM_ALPHA:SM_ALPHA + H_A].set(dt_bias)
    gn_t = jnp.tile(gnorm, H_A).reshape(1, HD)
    blk = lambda w, col: pl.BlockSpec((1, c_sz, w), lambda bi, ci: (bi, ci, col // w))
    const = lambda shp: pl.BlockSpec(shp, lambda bi, ci: (0, 0))
    return pl.pallas_call(
        functools.partial(_gdn_kernel, c_sz, pad),
        grid=(b, l // c_sz),
        in_specs=[blk(QKV_A, COL_QKV), blk(LANE, COL_SMALL), blk(HD, COL_GATE), const((CONV_W, QKV_A)),
                  const((1, LANE)), const((1, LANE)), const((1, HD))],
        out_specs=[pl.BlockSpec((1, c_sz, HD), lambda bi, ci: (bi, ci, 0)),
                   pl.BlockSpec((1, H_A, DK_A, DV_A), lambda bi, ci: (bi, 0, 0, 0))],
        out_shape=[jax.ShapeDtypeStruct((b, l, HD), BF16), jax.ShapeDtypeStruct((b, H_A, DK_A, DV_A), F32)],
        scratch_shapes=[pltpu.VMEM((DK_A, HD), F32), pltpu.VMEM((c_sz, QKV_A), F32)],
        compiler_params=pltpu.CompilerParams(dimension_semantics=("parallel", "arbitrary"),
                                             vmem_limit_bytes=VMEM_LIMIT),
        name="gdn_prompt",
    )(proj, proj, proj, conv_w, alog_p, dtb_p, gn_t)


def bias_tiles(rel_bias):
    c = jnp.arange(LANE, dtype=jnp.int32)[:, None]
    r = jnp.arange(LANE, dtype=jnp.int32)[None, :]
    d = jnp.arange(N_BIAS_TILES, dtype=jnp.int32)[:, None, None] * LANE + r - c
    return jnp.moveaxis(rel_bias[rel_bucket(d)], -1, 0).astype(F32)


def _select_topk(sc, valid, krow, topk, idx_bits):
    bits = pltpu.bitcast(sc + 0.0, jnp.int32)
    key = bits ^ ((bits >> 31) & 0x7FFFFFFF)
    key = jnp.where(valid, key, INT_MIN)
    kf = jnp.float32(topk)

    def count(m):
        return jnp.sum(jnp.where(m, 1.0, 0.0), axis=0, keepdims=True)

    t0 = jnp.where(count(key >= 0) >= kf, 0, INT_MIN).astype(jnp.int32)

    def tbody(n, t):
        cand = t | jnp.left_shift(jnp.int32(1), 30 - n)
        return jnp.where(count(key >= cand) >= kf, cand, t)

    t = lax.fori_loop(0, 31, tbody, t0)
    gt = key > t
    eq = key == t
    m = kf - count(gt)

    def jbody(n, j):
        cand = j | jnp.left_shift(jnp.int32(1), idx_bits - 1 - n)
        return jnp.where(count(eq & (krow < cand)) <= m, cand, j)

    j = lax.fori_loop(0, idx_bits, jbody, jnp.zeros_like(t))
    return valid & (gt | (eq & (krow < j)))


def _dsa_prompt_body(lc, pad, topk, q_ref, qi_ref, smq_ref, k_ref, v_ref, smk_ref, bias_ref, o_ref):
    i = pl.program_id(1)
    nkb = lc // LANE
    ki = smk_ref[0, :lc, SM_KI:SM_KI + D_IDX].astype(BF16)
    smt = smq_ref[0].T
    sc = jnp.zeros((lc, LANE), F32)
    for h in range(H_IDX):
        qih = qi_ref[0, :, h * D_IDX:(h + 1) * D_IDX].astype(BF16)
        d = lax.dot_general(ki, qih, NT_DIMS, preferred_element_type=F32)
        w = smt[SM_WI + h:SM_WI + h + 1, :] * (H_IDX ** -0.5)
        sc = sc + jnp.maximum(d * (D_IDX ** -0.5), 0.0) * w
    krow = lax.broadcasted_iota(jnp.int32, (lc, LANE), 0)
    qrow = i * LANE + lax.broadcasted_iota(jnp.int32, (lc, LANE), 1)
    valid = (krow <= qrow) & (krow >= pad)
    sel = _select_topk(sc, valid, krow, topk, max(1, (lc - 1).bit_length()))
    outs = []
    for h in range(H_B):
        sl = slice(h * DH_B, (h + 1) * DH_B)
        kh = k_ref[0, :lc, sl].astype(BF16)
        qh = q_ref[0, :, sl].astype(BF16)
        lg = lax.dot_general(kh, qh, NT_DIMS, preferred_element_type=F32) * (DH_B ** -0.5)
        bias = jnp.concatenate([bias_ref[h, jnp.clip(i - jb, 0, N_BIAS_TILES - 1)] for jb in range(nkb)], axis=0)
        lg = jnp.where(sel, lg + bias, -1e30)
        mx = jnp.max(lg, axis=0, keepdims=True)
        p = jnp.exp(lg - mx)
        p = (p * (1.0 / jnp.sum(p, axis=0, keepdims=True))).astype(BF16)
        vh = v_ref[0, :lc, sl].astype(BF16)
        outs.append(lax.dot_general(p, vh, TN_DIMS, preferred_element_type=F32))
    o_ref[0] = jnp.concatenate(outs, axis=1).astype(o_ref.dtype)


def _dsa_prompt_kernel(variants, pad, topk, *refs):
    _dispatch_variants(variants, _dsa_prompt_body, pad, topk, *refs)


def dsa_prompt(proj, btiles, pad, topk):
    b, l, _ = proj.shape
    nq = l // LANE
    qblk = lambda w, col: pl.BlockSpec((1, LANE, w), lambda bi, i: (bi, i, col // w))
    full = lambda w, col: pl.BlockSpec((1, l, w), lambda bi, i: (bi, 0, col // w))
    return pl.pallas_call(
        functools.partial(_dsa_prompt_kernel, _causal_variants(nq), pad, topk),
        grid=(b, nq),
        in_specs=[qblk(HD, COL_QB), qblk(HD, COL_QI), qblk(LANE, COL_SMALL),
                  full(HD, COL_KB), full(HD, COL_VB), full(LANE, COL_SMALL),
                  pl.BlockSpec(btiles.shape, lambda bi, i: (0, 0, 0, 0))],
        out_specs=pl.BlockSpec((1, LANE, HD), lambda bi, i: (bi, i, 0)),
        out_shape=jax.ShapeDtypeStruct((b, l, HD), BF16),
        compiler_params=pltpu.CompilerParams(dimension_semantics=("parallel", "arbitrary"),
                                             vmem_limit_bytes=VMEM_LIMIT),
        name="dsa_prompt",
    )(proj, proj, proj, proj, proj, proj, btiles)


def _mla_prep_kernel(h_ref, nm_ref, win_ref, qn_ref, kvn_ref, wuq_ref, wuk_ref, cos_ref, sin_ref, q_ref, kv_ref):
    x = _rms(h_ref[0], nm_ref[...]).astype(BF16)
    p = jnp.dot(x, win_ref[...], preferred_element_type=F32)
    cqn = _rms(p[:, CC_Q:CC_KV], qn_ref[...]).astype(BF16)
    qf = jnp.dot(cqn, wuq_ref[...], preferred_element_type=F32)
    cos, sin = cos_ref[...], sin_ref[...]
    q1, q2 = qf[:, H_C * LANE:(H_C + 1) * LANE], qf[:, (H_C + 1) * LANE:]
    q1r = q1 * cos - q2 * sin
    q2r = q2 * cos + q1 * sin
    head_of_lane = lax.broadcasted_iota(jnp.int32, (1, LANE), 1) // HALF
    for h in range(H_C):
        qlat = jnp.dot(qf[:, h * LANE:(h + 1) * LANE].astype(BF16), wuk_ref[h], preferred_element_type=F32)
        mine = head_of_lane == h
        q_ref[0, h] = jnp.concatenate([qlat, jnp.where(mine, q1r, 0.0), jnp.where(mine, q2r, 0.0)],
                                      axis=1).astype(q_ref.dtype)
    c = _rms(p[:, CC_KV:CC_K1], kvn_ref[...])
    k1, k2 = p[:, CC_K1:CC_K2], p[:, CC_K2:]
    kv_ref[0] = jnp.concatenate([c, k1 * cos - k2 * sin, k2 * cos + k1 * sin], axis=1)


def mla_prep(h, norm_w, w_in_p, qnorm, kvnorm, w_uq_p, w_uk_p, cos, sin):
    b, l, d = h.shape
    tm = LANE
    const2 = lambda shp: pl.BlockSpec(shp, lambda bi, i: (0, 0))
    return pl.pallas_call(
        _mla_prep_kernel,
        grid=(b, l // tm),
        in_specs=[pl.BlockSpec((1, tm, d), lambda bi, i: (bi, i, 0)), const2((1, d)), const2(w_in_p.shape),
                  const2((1, Q_LORA)), const2((1, KV_LORA)), const2(w_uq_p.shape),
                  pl.BlockSpec(w_uk_p.shape, lambda bi, i: (0, 0, 0)),
                  pl.BlockSpec((tm, LANE), lambda bi, i: (i, 0)), pl.BlockSpec((tm, LANE), lambda bi, i: (i, 0))],
        out_specs=[pl.BlockSpec((1, H_C, tm, KVW), lambda bi, i: (bi, 0, i, 0)),
                   pl.BlockSpec((1, tm, KVW), lambda bi, i: (bi, i, 0))],
        out_shape=[jax.ShapeDtypeStruct((b, H_C, l, KVW), BF16), jax.ShapeDtypeStruct((b, l, KVW), F32)],
        compiler_params=pltpu.CompilerParams(dimension_semantics=("parallel", "arbitrary"),
                                             vmem_limit_bytes=VMEM_LIMIT),
        name="mla_prep",
    )(h, norm_w.reshape(1, d), w_in_p, qnorm.reshape(1, -1), kvnorm.reshape(1, -1), w_uq_p, w_uk_p, cos, sin)


def _mla_body(lc, pad, q_ref, kv_ref, o_ref):
    i = pl.program_id(1)
    q = q_ref[0].reshape(H_C * LANE, KVW)
    kb = kv_ref[0, :lc, :].astype(BF16)
    s = lax.dot_general(q, kb, NT_DIMS, preferred_element_type=F32) * MLA_SCALE
    s = s.reshape(H_C, LANE, lc)
    krow = lax.broadcasted_iota(jnp.int32, (LANE, lc), 1)
    qrow = i * LANE + lax.broadcasted_iota(jnp.int32, (LANE, lc), 0)
    ok = (krow <= qrow) & (krow >= pad)
    s = jnp.where(ok[None], s, -1e30)
    mx = jnp.max(s, axis=-1, keepdims=True)
    p = jnp.exp(s - mx)
    p = (p * (1.0 / jnp.sum(p, axis=-1, keepdims=True))).astype(BF16).reshape(H_C * LANE, lc)
    o = jnp.dot(p, kb[:, :KV_LORA], preferred_element_type=F32).astype(o_ref.dtype)
    for h in range(H_C):
        o_ref[0, :, h * KV_LORA:(h + 1) * KV_LORA] = o[h * LANE:(h + 1) * LANE]


def _mla_kernel(variants, pad, *refs):
    _dispatch_variants(variants, _mla_body, pad, *refs)


def mla_prompt(q, kv, pad):
    b, _, l, _ = q.shape
    nq = l // LANE
    return pl.pallas_call(
        functools.partial(_mla_kernel, _causal_variants(nq), pad),
        grid=(b, nq),
        in_specs=[pl.BlockSpec((1, H_C, LANE, KVW), lambda bi, i: (bi, 0, i, 0)),
                  pl.BlockSpec((1, l, KVW), lambda bi, i: (bi, 0, 0))],
        out_specs=pl.BlockSpec((1, LANE, H_C * KV_LORA), lambda bi, i: (bi, i, 0)),
        out_shape=jax.ShapeDtypeStruct((b, l, H_C * KV_LORA), BF16),
        compiler_params=pltpu.CompilerParams(dimension_semantics=("parallel", "arbitrary"),
                                             vmem_limit_bytes=VMEM_LIMIT),
        name="mla_prompt",
    )(q, kv)


def _prep_w_ab(w):
    qkv_a, alpha, braw, gate, q_b, k_b, v_b, qi, wi, ki = split_cols(w, AB_SIZES)
    z = jnp.zeros((w.shape[0], SM_KI - SM_WI - H_IDX), w.dtype)
    return jnp.concatenate([qkv_a, gate, q_b, k_b, v_b, qi, alpha, braw, wi, z, ki], axis=1).astype(BF16)


def _prep_w_c(w_in, w_uq, w_uk):
    cq, ckv, kpe = split_cols(w_in, C_SIZES)
    w_in_p = jnp.concatenate([cq, ckv, jnp.tile(kpe[:, :HALF], (1, H_C)), jnp.tile(kpe[:, HALF:], (1, H_C))], axis=1)
    wq = w_uq.reshape(Q_LORA, H_C, QK_NOPE + QK_ROPE)
    nope = jnp.pad(wq[:, :, :QK_NOPE], ((0, 0), (0, 0), (0, LANE - QK_NOPE))).reshape(Q_LORA, H_C * LANE)
    r1 = wq[:, :, QK_NOPE:QK_NOPE + HALF].reshape(Q_LORA, H_C * HALF)
    r2 = wq[:, :, QK_NOPE + HALF:].reshape(Q_LORA, H_C * HALF)
    w_uq_p = jnp.concatenate([nope, r1, r2], axis=1)
    w_uk_p = jnp.pad(jnp.transpose(w_uk, (1, 2, 0)), ((0, 0), (0, LANE - QK_NOPE), (0, 0)))
    return w_in_p.astype(BF16), w_uq_p.astype(BF16), w_uk_p.astype(BF16)


def _rope_tables(pos):
    inv = ROPE_BASE ** (-jnp.arange(HALF, dtype=jnp.float32) / HALF)
    ang = pos.astype(jnp.float32)[:, None] * inv[None, :]
    return jnp.tile(jnp.cos(ang), (1, LANE // HALF)), jnp.tile(jnp.sin(ang), (1, LANE // HALF))


def kernel(x_prompt, x_sample, state_a_conv, state_a_ssm, cache_b_k, cache_b_v, cache_b_kidx,
           cache_c_latent, cache_c_kpe, page_table, meta_tokens, rel_bias, norm_mix, norm_ffn, norm_final,
           w_in_ab, conv_a, a_log, dt_bias_a, gnorm_a, w_out_ab, w_in_c, qnorm_c, kvnorm_c,
           w_uq_c, w_uk_c, w_uv_c, w_out_c, w_up, w_down):
    w_up_b = w_up.astype(BF16)
    w_down_b = w_down.astype(BF16)
    w_out_ab_b = w_out_ab.astype(BF16)
    w_out_c_b = w_out_c.astype(BF16)
    w_uv_b = jnp.transpose(w_uv_c, (0, 2, 1, 3)).astype(BF16)

    b, seq, d = x_prompt.shape
    lp = N_META + seq
    pad = (-lp) % LANE
    l = pad + lp
    n = b * l
    topk = min(TOPK_MAX, SEQ // 4)
    meta = jnp.broadcast_to(meta_tokens[None].astype(x_prompt.dtype), (b, N_META, d))
    h = jnp.concatenate([jnp.zeros((b, pad, d), x_prompt.dtype), meta, x_prompt], axis=1).reshape(n, d)
    cos_p, sin_p = _rope_tables(jnp.arange(l, dtype=jnp.int32) - pad)
    a_conv_p, a_ssm_p, b_k_p, b_v_p, b_kidx_p, c_lat_p, c_kpe_p = [], [], [], [], [], [], []
    for li in range(DEPTH):
        j = li // 2
        last = li == DEPTH - 1
        if li % 2 == 0:
            proj = rms_linear(h, norm_mix[li], _prep_w_ab(w_in_ab[j])).reshape(b, l, AB_COLS_P)
            o_a, s_fin = gdn_prompt(proj, conv_a[j], a_log[j], dt_bias_a[j], gnorm_a[j], pad)
            o_b = dsa_prompt(proj, bias_tiles(rel_bias), pad, topk)
            projs = [(o_a.reshape(n, HD), w_out_ab_b[j, :HD]), (o_b.reshape(n, HD), w_out_ab_b[j, HD:])]
            head_w = None
            a_conv_p.append(proj[:, l - (CONV_W - 1):, COL_QKV:COL_QKV + QKV_A])
            a_ssm_p.append(s_fin)
            b_k_p.append(proj[:, pad:, COL_KB:COL_KB + HD].reshape(b, lp, H_B, DH_B))
            b_v_p.append(proj[:, pad:, COL_VB:COL_VB + HD].reshape(b, lp, H_B, DH_B))
            b_kidx_p.append(proj[:, pad:, COL_SMALL + SM_KI:COL_SMALL + SM_KI + D_IDX])
        else:
            w_in_p, w_uq_p, w_uk_p = _prep_w_c(w_in_c[j], w_uq_c[j], w_uk_c[j])
            q_all, kv = mla_prep(h.reshape(b, l, d), norm_mix[li], w_in_p, qnorm_c[j], kvnorm_c[j], w_uq_p, w_uk_p,
                                 cos_p, sin_p)
            o_lat = mla_prompt(q_all, kv, pad)
            projs = [(o_lat.reshape(n, H_C * KV_LORA), w_out_c_b[j])]
            head_w = w_uv_b[j]
            c_lat_p.append(kv[:, pad:, :KV_LORA])
            c_kpe_p.append(jnp.concatenate([kv[:, pad:, KV_LORA:KV_LORA + HALF],
                                            kv[:, pad:, KV_LORA + LANE:KV_LORA + LANE + HALF]], axis=-1))
        h = proj_mlp(h, projs, norm_ffn[li], w_up_b[li], w_down_b[li], norm_final if last else None, head_w)
    y_p = h.reshape(b, l, d)[:, pad + N_META:]

    hs = x_sample
    bs, ts, _ = hs.shape
    ab_states, c_states = [], []
    for li in range(DEPTH):
        j = li // 2
        last = li == DEPTH - 1
        hn = rmsnorm(hs, norm_mix[li])
        if li % 2 == 0:
            past = (state_a_conv[j], state_a_ssm[j], cache_b_k[j], cache_b_v[j], cache_b_kidx[j], page_table)
            out, st = mixer_ab_decode(hn, w_in_ab[j], conv_a[j], a_log[j], dt_bias_a[j], gnorm_a[j], w_out_ab[j], rel_bias, past)
            ab_states.append(st)
        else:
            past = (cache_c_latent[j], cache_c_kpe[j], page_table)
            out, st = mixer_c_decode(hn, w_in_c[j], qnorm_c[j], kvnorm_c[j], w_uq_c[j], w_uk_c[j], w_uv_c[j], w_out_c[j], past)
            c_states.append(st)
        hs = hs + out.astype(hs.dtype)
        hs = proj_mlp(hs.reshape(bs * ts, d), [], norm_ffn[li], w_up_b[li], w_down_b[li],
                      norm_final if last else None).reshape(bs, ts, d)
    ab_s = [jnp.stack(z) for z in zip(*ab_states)]
    c_s = [jnp.stack(z) for z in zip(*c_states)]

    stack = jnp.stack
    return (y_p, hs,
            stack(a_conv_p), stack(a_ssm_p), stack(b_k_p), stack(b_v_p), stack(b_kidx_p), stack(c_lat_p), stack(c_kpe_p),
            ab_s[0], ab_s[1], ab_s[2], ab_s[3], ab_s[4], c_s[0], c_s[1])
```

```python
import functools
import math
import jax
import jax.numpy as jnp
from jax import lax
from jax.experimental import pallas as pl
from jax.experimental.pallas import tpu as pltpu

D_MODEL = 1024
BATCH = 8
SEQ = 2048
DEPTH = 2
DEC_BATCH = 32
DEC_SEQ = 1
PAST_LEN = 16384
PAGE_SIZE = 128

N_META = 16
EPS = 1e-6
H_A = 8
DK_A = 64
DV_A = 64
CONV_W = 4
H_B = 8
DH_B = 64
H_IDX = 8
D_IDX = 64
TOPK_MAX = 256
REL_BUCKETS = 32
REL_MAX_DIST = 1024
H_C = 8
Q_LORA = 384
KV_LORA = 256
QK_NOPE = 64
QK_ROPE = 32
V_C = 128
ROPE_BASE = 10000.0
D_FF = 4 * D_MODEL
Q_BLOCK = 128

QKV_A = H_A * (2 * DK_A + DV_A)
AB_SIZES = (QKV_A, H_A, H_A, H_A * DV_A, H_B * DH_B, H_B * DH_B, H_B * DH_B, H_IDX * D_IDX, H_IDX, D_IDX)
AB_COLS = sum(AB_SIZES)
C_SIZES = (Q_LORA, KV_LORA, QK_ROPE)
C_COLS = sum(C_SIZES)
MLA_SCALE = (QK_NOPE + QK_ROPE) ** -0.5
N_AB = (DEPTH + 1) // 2
N_C = DEPTH // 2

BF16 = jnp.bfloat16
F32 = jnp.float32
LANE = 128
VMEM_LIMIT = 56 * 1024 * 1024
INT_MIN = -(2 ** 31)
NT_DIMS = (((1,), (1,)), ((), ()))
TN_DIMS = (((0,), (0,)), ((), ()))
HI = lax.Precision.HIGHEST
HD = H_A * DK_A
HALF = QK_ROPE // 2

COL_QKV, COL_GATE, COL_QB, COL_KB, COL_VB, COL_QI, COL_SMALL = 0, 1536, 2048, 2560, 3072, 3584, 4096
AB_COLS_P = COL_SMALL + LANE
SM_ALPHA, SM_BRAW, SM_WI, SM_KI = 0, 8, 16, 64
CC_Q, CC_KV, CC_K1, CC_K2, C_COLS_P = 0, 384, 640, 768, 896
KVW = KV_LORA + 2 * LANE
N_BIAS_TILES = 9


def rmsnorm(x, w):
    xf = x.astype(jnp.float32)
    y = xf * lax.rsqrt(jnp.mean(xf * xf, axis=-1, keepdims=True) + EPS)
    return (y * w.astype(jnp.float32)).astype(x.dtype)


def l2norm(x):
    return x * lax.rsqrt(jnp.sum(x * x, axis=-1, keepdims=True) + EPS)


def split_cols(x, sizes):
    out, off = [], 0
    for s in sizes:
        out.append(x[..., off:off + s])
        off += s
    return out


def gather_pages(pool, page_table):
    g = pool[page_table]
    return g.reshape((g.shape[0], g.shape[1] * g.shape[2]) + g.shape[3:])


def gather_rows(pool, page_table, idx):
    db = idx.shape[0]
    page = idx // PAGE_SIZE
    off = idx % PAGE_SIZE
    phys = jnp.take_along_axis(page_table, page.reshape(db, -1), axis=1).reshape(idx.shape)
    return pool[phys, off]


def rel_bucket(dist):
    dist = jnp.maximum(dist, 0)
    max_exact = REL_BUCKETS // 2
    log_ratio = jnp.log(jnp.maximum(dist, 1).astype(jnp.float32) / max_exact) / math.log(REL_MAX_DIST / max_exact)
    large = max_exact + (log_ratio * (REL_BUCKETS - max_exact)).astype(jnp.int32)
    return jnp.where(dist < max_exact, dist, jnp.minimum(large, REL_BUCKETS - 1))


def rope(x, pos):
    inv = ROPE_BASE ** (-jnp.arange(HALF, dtype=jnp.float32) / HALF)
    ang = pos.astype(jnp.float32)[:, None] * inv[None, :]
    cos = jnp.cos(ang)[:, None, :]
    sin = jnp.sin(ang)[:, None, :]
    xf = x.astype(jnp.float32)
    x1, x2 = xf[..., :HALF], xf[..., HALF:]
    return jnp.concatenate([x1 * cos - x2 * sin, x2 * cos + x1 * sin], axis=-1).astype(x.dtype)


def causal_conv(xp, w, t):
    acc = xp[:, 0:t] * w[0]
    for j in range(1, CONV_W):
        acc = acc + xp[:, j:j + t] * w[j]
    return jax.nn.silu(acc)


def gdn_prep(qkv, alpha, braw, a_log, dt_bias):
    b, t = qkv.shape[:2]
    qkv = qkv.astype(jnp.float32)
    q, k, v = split_cols(qkv, (H_A * DK_A, H_A * DK_A, H_A * DV_A))
    q = l2norm(q.reshape(b, t, H_A, DK_A)) * DK_A ** -0.5
    k = l2norm(k.reshape(b, t, H_A, DK_A))
    v = v.reshape(b, t, H_A, DV_A)
    g = -jnp.exp(a_log.astype(jnp.float32)) * jax.nn.softplus(alpha.astype(jnp.float32) + dt_bias.astype(jnp.float32))
    beta = jax.nn.sigmoid(braw.astype(jnp.float32))
    return q, k, v, g, beta


def gdn_step(s, xs):
    q, k, v, g, beta = xs
    s = s * jnp.exp(g)[..., None, None]
    delta = (v - jnp.einsum('bhk,bhkv->bhv', k, s)) * beta[..., None]
    s = s + jnp.einsum('bhk,bhv->bhkv', k, delta)
    return s, jnp.einsum('bhk,bhkv->bhv', q, s)


def gdn_decode(s, q, k, v, g, beta):
    xs = tuple(jnp.moveaxis(a, 1, 0) for a in (q, k, v, g, beta))
    s_fin, o = lax.scan(gdn_step, s.astype(jnp.float32), xs)
    return jnp.moveaxis(o, 0, 1), s_fin


def index_scores(qi, wi, ki):
    s = jax.nn.relu(jnp.einsum('bthd,bsd->bths', qi, ki).astype(jnp.float32) * D_IDX ** -0.5)
    return jnp.einsum('bths,bth->bts', s, wi.astype(jnp.float32) * H_IDX ** -0.5)


def sparse_attend(q, k_sel, v_sel, valid, dist, rel_bias):
    logits = jnp.einsum('bthd,btkhd->bthk', q, k_sel).astype(jnp.float32) * DH_B ** -0.5
    bias = rel_bias[rel_bucket(dist)]
    logits = logits + jnp.moveaxis(bias, -1, -2).astype(jnp.float32)
    logits = jnp.where(valid[:, :, None, :], logits, -jnp.inf)
    p = jax.nn.softmax(logits, axis=-1).astype(v_sel.dtype)
    return jnp.einsum('bthk,btkhd->bthd', p, v_sel)


def gather_seq(rows, idx):
    return jax.vmap(lambda rb, ib: rb[ib])(rows, idx)


def dsa_decode(q, k_new, v_new, qi, wi, ki_new, cache_k, cache_v, cache_ki, page_table, rel_bias):
    db, t = q.shape[:2]
    n_keys = PAST_LEN + t
    topk = min(TOPK_MAX, n_keys // 4)
    ki_all = jnp.concatenate([gather_pages(cache_ki, page_table).astype(ki_new.dtype), ki_new], axis=1)
    pos = PAST_LEN + jnp.arange(t, dtype=jnp.int32)
    sc = index_scores(qi, wi, ki_all)
    key_pos = jnp.arange(n_keys, dtype=jnp.int32)
    sc = jnp.where((key_pos[None, :] <= pos[:, None])[None], sc, -jnp.inf)
    _, idx = lax.top_k(sc, topk)
    valid = idx <= pos[None, :, None]
    in_past = (idx < PAST_LEN)[..., None, None]
    past_idx = jnp.minimum(idx, PAST_LEN - 1)
    new_idx = jnp.clip(idx - PAST_LEN, 0, t - 1)
    k_sel = jnp.where(in_past, gather_rows(cache_k, page_table, past_idx).astype(k_new.dtype), gather_seq(k_new, new_idx))
    v_sel = jnp.where(in_past, gather_rows(cache_v, page_table, past_idx).astype(v_new.dtype), gather_seq(v_new, new_idx))
    dist = pos[None, :, None] - idx
    return sparse_attend(q, k_sel, v_sel, valid, dist, rel_bias)


def mixer_ab_decode(hn, w_in, conv_w, a_log, dt_bias, gnorm, w_out, rel_bias, past):
    b, t, _ = hn.shape
    qkv_a, alpha, braw, gate, q_b, k_b, v_b, qi, wi, ki = split_cols(hn @ w_in, AB_SIZES)
    conv_buf, s_prev, cache_k, cache_v, cache_ki, page_table = past
    xp = jnp.concatenate([conv_buf.astype(qkv_a.dtype), qkv_a], axis=1)
    q, k, v, g, beta = gdn_prep(causal_conv(xp, conv_w, t), alpha, braw, a_log, dt_bias)
    o_a, s_new = gdn_decode(s_prev, q, k, v, g, beta)
    o_a = rmsnorm(o_a, gnorm) * jax.nn.silu(gate.reshape(b, t, H_A, DV_A).astype(jnp.float32))
    o_a = o_a.reshape(b, t, H_A * DV_A).astype(hn.dtype)
    q_b = q_b.reshape(b, t, H_B, DH_B)
    k_b = k_b.reshape(b, t, H_B, DH_B)
    v_b = v_b.reshape(b, t, H_B, DH_B)
    qi = qi.reshape(b, t, H_IDX, D_IDX)
    o_b = dsa_decode(q_b, k_b, v_b, qi, wi, ki, cache_k, cache_v, cache_ki, page_table, rel_bias)
    out = jnp.concatenate([o_a, o_b.reshape(b, t, H_B * DH_B).astype(hn.dtype)], axis=-1) @ w_out
    return out, (xp[:, -(CONV_W - 1):], s_new, k_b, v_b, ki)


def mla_core(q_lat, q_pe, c, kpe, visible):
    logits = (jnp.einsum('bthc,bsc->bhts', q_lat, c) + jnp.einsum('bthr,bsr->bhts', q_pe, kpe)).astype(jnp.float32) * MLA_SCALE
    logits = jnp.where(visible[None, None], logits, -jnp.inf)
    p = jax.nn.softmax(logits, axis=-1).astype(c.dtype)
    return jnp.einsum('bhts,bsc->bthc', p, c)


def mixer_c_decode(hn, w_in, qnorm, kvnorm, w_uq, w_uk, w_uv, w_out, past):
    b, t, _ = hn.shape
    cq, ckv, kpe = split_cols(hn @ w_in, C_SIZES)
    pos = PAST_LEN + jnp.arange(t, dtype=jnp.int32)
    qf = (rmsnorm(cq, qnorm) @ w_uq).reshape(b, t, H_C, QK_NOPE + QK_ROPE)
    q_pe = rope(qf[..., QK_NOPE:], pos)
    q_lat = jnp.einsum('bthn,chn->bthc', qf[..., :QK_NOPE], w_uk)
    c = rmsnorm(ckv, kvnorm)
    kpe = rope(kpe[:, :, None, :], pos)[:, :, 0, :]
    cache_c, cache_kpe, page_table = past
    c_all = jnp.concatenate([gather_pages(cache_c, page_table).astype(c.dtype), c], axis=1)
    kpe_all = jnp.concatenate([gather_pages(cache_kpe, page_table).astype(kpe.dtype), kpe], axis=1)
    key_pos = jnp.arange(PAST_LEN + t, dtype=jnp.int32)
    o_lat = mla_core(q_lat, q_pe, c_all, kpe_all, key_pos[None, :] <= pos[:, None])
    o = jnp.einsum('bthc,chv->bthv', o_lat, w_uv).reshape(b, t, H_C * V_C)
    return o @ w_out, (c, kpe)


def _rms(x, w):
    return x * lax.rsqrt(jnp.mean(x * x, axis=-1, keepdims=True) + EPS) * w


def _hdot(a, b):
    return jnp.dot(a, b, precision=HI, preferred_element_type=F32)


def _dot_exact_rhs(a, b_bf16):
    a_hi = a.astype(BF16)
    a_lo = (a - a_hi.astype(F32)).astype(BF16)
    return jnp.dot(a_hi, b_bf16, preferred_element_type=F32) + jnp.dot(a_lo, b_bf16, preferred_element_type=F32)


def _split2(a):
    hi = a.astype(BF16)
    return hi, (a - hi.astype(F32)).astype(BF16)


def _dot3(ap, bp):
    (ah, al), (bh, bl) = ap, bp
    return (jnp.dot(ah, bh, preferred_element_type=F32) + jnp.dot(al, bh, preferred_element_type=F32)
            + jnp.dot(ah, bl, preferred_element_type=F32))


def _causal_variants(nq):
    n_var = min(4, nq)
    his = [nq - (n_var - 1 - v) * (nq // n_var) for v in range(n_var)]
    return [(hi, hi * LANE) for hi in his]


def _dispatch_variants(variants, body, *args):
    i = pl.program_id(1)
    lo = 0
    for hi, lc in variants:
        pl.when((i >= lo) & (i < hi))(functools.partial(body, lc, *args))
        lo = hi


def _rms_linear_kernel(x_ref, nw_ref, w_ref, o_ref):
    xn = _rms(x_ref[...], nw_ref[...]).astype(BF16)
    o_ref[...] = jnp.dot(xn, w_ref[...], preferred_element_type=F32)


def rms_linear(x, norm_w, w, tm=256):
    n, d = x.shape
    m = w.shape[1]
    tm = min(tm, n)
    return pl.pallas_call(
        _rms_linear_kernel,
        grid=(pl.cdiv(n, tm),),
        in_specs=[pl.BlockSpec((tm, d), lambda i: (i, 0)), pl.BlockSpec((1, d), lambda i: (0, 0)),
                  pl.BlockSpec((d, m), lambda i: (0, 0))],
        out_specs=pl.BlockSpec((tm, m), lambda i: (i, 0)),
        out_shape=jax.ShapeDtypeStruct((n, m), F32),
        compiler_params=pltpu.CompilerParams(dimension_semantics=("parallel",), vmem_limit_bytes=VMEM_LIMIT),
        name="rms_linear",
    )(x, norm_w.reshape(1, d), w)


def _proj_mlp_kernel(n_proj, n_heads, final, *refs):
    refs = list(refs)
    h_ref = refs.pop(0)
    a_refs = [refs.pop(0) for _ in range(n_proj)]
    w_refs = [refs.pop(0) for _ in range(n_proj)]
    hw_ref = refs.pop(0) if n_heads else None
    nw_ref, wup_ref, wdn_ref = refs.pop(0), refs.pop(0), refs.pop(0)
    nf_ref = refs.pop(0) if final else None
    o_ref, h1_s, xn_s, acc_s = refs
    f = pl.program_id(1)

    @pl.when(f == 0)
    def _():
        h1 = h_ref[...]
        for a_ref, w_ref in zip(a_refs, w_refs):
            a = a_ref[...]
            if n_heads:
                kh = a.shape[1] // n_heads
                a = jnp.concatenate([jnp.dot(a[:, hh * kh:(hh + 1) * kh].astype(BF16), hw_ref[hh],
                                             preferred_element_type=F32) for hh in range(n_heads)], axis=1)
            h1 = h1 + jnp.dot(a.astype(BF16), w_ref[...], preferred_element_type=F32)
        h1_s[...] = h1
        xn_s[...] = _rms(h1, nw_ref[...]).astype(BF16)
        acc_s[...] = jnp.zeros_like(acc_s)

    u = jnp.dot(xn_s[...], wup_ref[...], preferred_element_type=F32)
    u = jnp.square(jnp.maximum(u, 0.0)).astype(BF16)
    acc_s[...] += jnp.dot(u, wdn_ref[...], preferred_element_type=F32)

    @pl.when(f == pl.num_programs(1) - 1)
    def _():
        h2 = h1_s[...] + acc_s[...]
        if final:
            h2 = _rms(h2, nf_ref[...])
        o_ref[...] = h2


def proj_mlp(h, projs, norm_w, w_up, w_down, norm_final=None, head_w=None, tm=512, tf=1024):
    n, d = h.shape
    ff = w_up.shape[1]
    tm = min(tm, n)
    final = norm_final is not None
    n_heads = 0 if head_w is None else head_w.shape[0]
    row = lambda i, f: (i, 0)
    const = lambda i, f: (0, 0)
    in_specs = [pl.BlockSpec((tm, d), row)]
    in_specs += [pl.BlockSpec((tm, a.shape[1]), row) for a, _ in projs]
    in_specs += [pl.BlockSpec(w.shape, const) for _, w in projs]
    args = [h] + [a for a, _ in projs] + [w for _, w in projs]
    if n_heads:
        in_specs.append(pl.BlockSpec(head_w.shape, lambda i, f: (0, 0, 0)))
        args.append(head_w)
    in_specs += [pl.BlockSpec((1, d), const), pl.BlockSpec((d, tf), lambda i, f: (0, f)),
                 pl.BlockSpec((tf, d), lambda i, f: (f, 0))]
    args += [norm_w.reshape(1, d), w_up, w_down]
    if final:
        in_specs.append(pl.BlockSpec((1, d), const))
        args.append(norm_final.reshape(1, d))
    return pl.pallas_call(
        functools.partial(_proj_mlp_kernel, len(projs), n_heads, final),
        grid=(pl.cdiv(n, tm), ff // tf),
        in_specs=in_specs,
        out_specs=pl.BlockSpec((tm, d), row),
        out_shape=jax.ShapeDtypeStruct((n, d), F32),
        scratch_shapes=[pltpu.VMEM((tm, d), F32), pltpu.VMEM((tm, d), BF16), pltpu.VMEM((tm, d), F32)],
        compiler_params=pltpu.CompilerParams(dimension_semantics=("parallel", "arbitrary"),
                                             vmem_limit_bytes=VMEM_LIMIT),
        name="proj_mlp",
    )(*args)


def _gdn_kernel(c_sz, pad, qkv_ref, small_ref, gate_ref, convw_ref, alog_ref, dtb_ref, gnorm_ref,
                o_ref, sfin_ref, s_s, prev_s):
    c = pl.program_id(1)

    @pl.when(c == 0)
    def _():
        s_s[...] = jnp.zeros_like(s_s)
        prev_s[...] = jnp.zeros_like(prev_s)

    row = lax.broadcasted_iota(jnp.int32, (c_sz, 1), 0)
    valid = (row + c * c_sz) >= pad
    x = jnp.where(valid, qkv_ref[0], 0.0)
    prevx = prev_s[...]
    acc = x * convw_ref[CONV_W - 1:CONV_W, :]
    for sft in range(1, CONV_W):
        shifted = jnp.where(row < sft, pltpu.roll(prevx, sft, 0), pltpu.roll(x, sft, 0))
        acc = acc + shifted * convw_ref[CONV_W - 1 - sft:CONV_W - sft, :]
    prev_s[...] = x
    y = acc * jax.nn.sigmoid(acc)
    q, k, v = y[:, :HD], y[:, HD:2 * HD], y[:, 2 * HD:]

    gi = lax.broadcasted_iota(jnp.int32, (HD, HD), 0) // DK_A
    gj = lax.broadcasted_iota(jnp.int32, (HD, HD), 1) // DK_A
    gmat = jnp.where(gi == gj, 1.0, 0.0).astype(BF16)
    q = q * lax.rsqrt(_dot_exact_rhs(q * q, gmat) + EPS) * (DK_A ** -0.5)
    k = k * lax.rsqrt(_dot_exact_rhs(k * k, gmat) + EPS)

    sm = small_ref[0]
    zs = sm + dtb_ref[...]
    g = -jnp.exp(alog_ref[...]) * (jnp.maximum(zs, 0.0) + jnp.log(1.0 + jnp.exp(-jnp.abs(zs))))
    g = jnp.where(valid, g, 0.0)
    beta = jnp.where(valid, jax.nn.sigmoid(sm), 0.0)
    ti = lax.broadcasted_iota(jnp.int32, (c_sz, c_sz), 0)
    tj = lax.broadcasted_iota(jnp.int32, (c_sz, c_sz), 1)
    incl = ti >= tj
    strict = ti > tj
    gc = _hdot(jnp.where(incl, 1.0, 0.0), g)
    gct = gc.T
    ei = lax.broadcasted_iota(jnp.int32, (LANE, HD), 0)
    ej = lax.broadcasted_iota(jnp.int32, (LANE, HD), 1) // DK_A
    gc_w = _hdot(gc, jnp.where(ei == ej + SM_ALPHA, 1.0, 0.0))
    beta_w = _dot_exact_rhs(beta, jnp.where(ei == ej + SM_BRAW, 1.0, 0.0).astype(BF16))
    gl_w = gc_w[c_sz - 1:c_sz, :]
    eg_w = jnp.exp(gc_w)
    q_s = q * eg_w
    k_b = k * (beta_w * eg_w)
    v_b = v * beta_w
    k_l = k * jnp.exp(gl_w - gc_w)
    s_old = s_s[...]
    s_old_b = s_old.astype(BF16)
    n_fac = max(1, (c_sz - 1).bit_length())
    sls = [slice(h * DK_A, (h + 1) * DK_A) for h in range(H_A)]
    k16, q16 = k.astype(BF16), q.astype(BF16)
    decays, ms, xss = [], [], []
    for h, sl in enumerate(sls):
        diff = gc[:, SM_ALPHA + h:SM_ALPHA + h + 1] - gct[SM_ALPHA + h:SM_ALPHA + h + 1, :]
        decay = jnp.exp(jnp.where(incl, diff, -1e30))
        kk = lax.dot_general(k16[:, sl], k16[:, sl], NT_DIMS, preferred_element_type=F32)
        ms.append(-jnp.where(strict, beta[:, SM_BRAW + h:SM_BRAW + h + 1] * kk * decay, 0.0))
        xss.append(jnp.concatenate([v_b[:, sl], k_b[:, sl]], axis=1))
        decays.append(decay)
    for s in range(n_fac):
        msp = [_split2(m) for m in ms]
        xss = [x + _dot3(mp, _split2(x)) for x, mp in zip(xss, msp)]
        if s < n_fac - 1:
            ms = [_dot3(mp, mp) for mp in msp]
    outs, s_upd = [], []
    for h, sl in enumerate(sls):
        xs = xss[h]
        sh = s_old_b[:, sl]
        v_new = xs[:, :DV_A] - jnp.dot(xs[:, DV_A:].astype(BF16), sh, preferred_element_type=F32)
        v_new_b = v_new.astype(BF16)
        qk = lax.dot_general(q16[:, sl], k16[:, sl], NT_DIMS, preferred_element_type=F32) * decays[h]
        outs.append(jnp.dot(q_s[:, sl].astype(BF16), sh, preferred_element_type=F32)
                    + jnp.dot(qk.astype(BF16), v_new_b, preferred_element_type=F32))
        s_upd.append(lax.dot_general(k_l[:, sl].astype(BF16), v_new_b, TN_DIMS, preferred_element_type=F32))
    s_new = s_old * jnp.exp(gl_w) + jnp.concatenate(s_upd, axis=1)
    s_s[...] = s_new
    o = jnp.concatenate(outs, axis=1)
    ms = _dot_exact_rhs(o * o, gmat) * (1.0 / DV_A)
    gate = gate_ref[0]
    o_ref[0] = (o * lax.rsqrt(ms + EPS) * gnorm_ref[...] * (gate * jax.nn.sigmoid(gate))).astype(o_ref.dtype)

    @pl.when(c == pl.num_programs(1) - 1)
    def _():
        for h in range(H_A):
            sfin_ref[0, h] = s_new[:, h * DV_A:(h + 1) * DV_A]


def gdn_prompt(proj, conv_w, a_log, dt_bias, gnorm, pad, c_sz=LANE):
    b, l, _ = proj.shape
    alog_p = jnp.zeros((1, LANE), F32).at[0, SM_ALPHA:SM_ALPHA + H_A].set(a_log)
    dtb_p = jnp.zeros((1, LANE), F32).at[0, SM_ALPHA:SM_ALPHA + H_A].set(dt_bias)
    gn_t = jnp.tile(gnorm, H_A).reshape(1, HD)
    blk = lambda w, col: pl.BlockSpec((1, c_sz, w), lambda bi, ci: (bi, ci, col // w))
    const = lambda shp: pl.BlockSpec(shp, lambda bi, ci: (0, 0))
    return pl.pallas_call(
        functools.partial(_gdn_kernel, c_sz, pad),
        grid=(b, l // c_sz),
        in_specs=[blk(QKV_A, COL_QKV), blk(LANE, COL_SMALL), blk(HD, COL_GATE), const((CONV_W, QKV_A)),
                  const((1, LANE)), const((1, LANE)), const((1, HD))],
        out_specs=[pl.BlockSpec((1, c_sz, HD), lambda bi, ci: (bi, ci, 0)),
                   pl.BlockSpec((1, H_A, DK_A, DV_A), lambda bi, ci: (bi, 0, 0, 0))],
        out_shape=[jax.ShapeDtypeStruct((b, l, HD), BF16), jax.ShapeDtypeStruct((b, H_A, DK_A, DV_A), F32)],
        scratch_shapes=[pltpu.VMEM((DK_A, HD), F32), pltpu.VMEM((c_sz, QKV_A), F32)],
        compiler_params=pltpu.CompilerParams(dimension_semantics=("parallel", "arbitrary"),
                                             vmem_limit_bytes=VMEM_LIMIT),
        name="gdn_prompt",
    )(proj, proj, proj, conv_w, alog_p, dtb_p, gn_t)


def bias_tiles(rel_bias):
    c = jnp.arange(LANE, dtype=jnp.int32)[:, None]
    r = jnp.arange(LANE, dtype=jnp.int32)[None, :]
    d = jnp.arange(N_BIAS_TILES, dtype=jnp.int32)[:, None, None] * LANE + r - c
    return jnp.moveaxis(rel_bias[rel_bucket(d)], -1, 0).astype(F32)


def _select_topk(sc, valid, krow, topk, idx_bits):
    bits = pltpu.bitcast(sc + 0.0, jnp.int32)
    key = bits ^ ((bits >> 31) & 0x7FFFFFFF)
    key = jnp.where(valid, key, INT_MIN)
    kf = jnp.float32(topk)

    def count(m):
        return jnp.sum(jnp.where(m, 1.0, 0.0), axis=0, keepdims=True)

    t0 = jnp.where(count(key >= 0) >= kf, 0, INT_MIN).astype(jnp.int32)

    def tbody(n, t):
        cand = t | jnp.left_shift(jnp.int32(1), 30 - n)
        return jnp.where(count(key >= cand) >= kf, cand, t)

    t = lax.fori_loop(0, 31, tbody, t0)
    gt = key > t
    eq = key == t
    m = kf - count(gt)

    def jbody(n, j):
        cand = j | jnp.left_shift(jnp.int32(1), idx_bits - 1 - n)
        return jnp.where(count(eq & (krow < cand)) <= m, cand, j)

    j = lax.fori_loop(0, idx_bits, jbody, jnp.zeros_like(t))
    return valid & (gt | (eq & (krow < j)))


def _dsa_prompt_body(lc, pad, topk, q_ref, qi_ref, smq_ref, k_ref, v_ref, smk_ref, bias_ref, o_ref):
    i = pl.program_id(1)
    nkb = lc // LANE
    ki = smk_ref[0, :lc, SM_KI:SM_KI + D_IDX].astype(BF16)
    smt = smq_ref[0].T
    sc = jnp.zeros((lc, LANE), F32)
    for h in range(H_IDX):
        qih = qi_ref[0, :, h * D_IDX:(h + 1) * D_IDX].astype(BF16)
        d = lax.dot_general(ki, qih, NT_DIMS, preferred_element_type=F32)
        w = smt[SM_WI + h:SM_WI + h + 1, :] * (H_IDX ** -0.5)
        sc = sc + jnp.maximum(d * (D_IDX ** -0.5), 0.0) * w
    krow = lax.broadcasted_iota(jnp.int32, (lc, LANE), 0)
    qrow = i * LANE + lax.broadcasted_iota(jnp.int32, (lc, LANE), 1)
    valid = (krow <= qrow) & (krow >= pad)
    sel = _select_topk(sc, valid, krow, topk, max(1, (lc - 1).bit_length()))
    outs = []
    for h in range(H_B):
        sl = slice(h * DH_B, (h + 1) * DH_B)
        kh = k_ref[0, :lc, sl].astype(BF16)
        qh = q_ref[0, :, sl].astype(BF16)
        lg = lax.dot_general(kh, qh, NT_DIMS, preferred_element_type=F32) * (DH_B ** -0.5)
        bias = jnp.concatenate([bias_ref[h, jnp.clip(i - jb, 0, N_BIAS_TILES - 1)] for jb in range(nkb)], axis=0)
        lg = jnp.where(sel, lg + bias, -1e30)
        mx = jnp.max(lg, axis=0, keepdims=True)
        p = jnp.exp(lg - mx)
        p = (p * (1.0 / jnp.sum(p, axis=0, keepdims=True))).astype(BF16)
        vh = v_ref[0, :lc, sl].astype(BF16)
        outs.append(lax.dot_general(p, vh, TN_DIMS, preferred_element_type=F32))
    o_ref[0] = jnp.concatenate(outs, axis=1).astype(o_ref.dtype)


def _dsa_prompt_kernel(variants, pad, topk, *refs):
    _dispatch_variants(variants, _dsa_prompt_body, pad, topk, *refs)


def dsa_prompt(proj, btiles, pad, topk):
    b, l, _ = proj.shape
    nq = l // LANE
    qblk = lambda w, col: pl.BlockSpec((1, LANE, w), lambda bi, i: (bi, i, col // w))
    full = lambda w, col: pl.BlockSpec((1, l, w), lambda bi, i: (bi, 0, col // w))
    return pl.pallas_call(
        functools.partial(_dsa_prompt_kernel, _causal_variants(nq), pad, topk),
        grid=(b, nq),
        in_specs=[qblk(HD, COL_QB), qblk(HD, COL_QI), qblk(LANE, COL_SMALL),
                  full(HD, COL_KB), full(HD, COL_VB), full(LANE, COL_SMALL),
                  pl.BlockSpec(btiles.shape, lambda bi, i: (0, 0, 0, 0))],
        out_specs=pl.BlockSpec((1, LANE, HD), lambda bi, i: (bi, i, 0)),
        out_shape=jax.ShapeDtypeStruct((b, l, HD), BF16),
        compiler_params=pltpu.CompilerParams(dimension_semantics=("parallel", "arbitrary"),
                                             vmem_limit_bytes=VMEM_LIMIT),
        name="dsa_prompt",
    )(proj, proj, proj, proj, proj, proj, btiles)


def _mla_prep_kernel(h_ref, nm_ref, win_ref, qn_ref, kvn_ref, wuq_ref, wuk_ref, cos_ref, sin_ref, q_ref, kv_ref):
    x = _rms(h_ref[0], nm_ref[...]).astype(BF16)
    p = jnp.dot(x, win_ref[...], preferred_element_type=F32)
    cqn = _rms(p[:, CC_Q:CC_KV], qn_ref[...]).astype(BF16)
    qf = jnp.dot(cqn, wuq_ref[...], preferred_element_type=F32)
    cos, sin = cos_ref[...], sin_ref[...]
    q1, q2 = qf[:, H_C * LANE:(H_C + 1) * LANE], qf[:, (H_C + 1) * LANE:]
    q1r = q1 * cos - q2 * sin
    q2r = q2 * cos + q1 * sin
    head_of_lane = lax.broadcasted_iota(jnp.int32, (1, LANE), 1) // HALF
    for h in range(H_C):
        qlat = jnp.dot(qf[:, h * LANE:(h + 1) * LANE].astype(BF16), wuk_ref[h], preferred_element_type=F32)
        mine = head_of_lane == h
        q_ref[0, h] = jnp.concatenate([qlat, jnp.where(mine, q1r, 0.0), jnp.where(mine, q2r, 0.0)],
                                      axis=1).astype(q_ref.dtype)
    c = _rms(p[:, CC_KV:CC_K1], kvn_ref[...])
    k1, k2 = p[:, CC_K1:CC_K2], p[:, CC_K2:]
    kv_ref[0] = jnp.concatenate([c, k1 * cos - k2 * sin, k2 * cos + k1 * sin], axis=1)


def mla_prep(h, norm_w, w_in_p, qnorm, kvnorm, w_uq_p, w_uk_p, cos, sin):
    b, l, d = h.shape
    tm = min(LANE, l)
    const2 = lambda shp: pl.BlockSpec(shp, lambda bi, i: (0, 0))
    return pl.pallas_call(
        _mla_prep_kernel,
        grid=(b, l // tm),
        in_specs=[pl.BlockSpec((1, tm, d), lambda bi, i: (bi, i, 0)), const2((1, d)), const2(w_in_p.shape),
                  const2((1, Q_LORA)), const2((1, KV_LORA)), const2(w_uq_p.shape),
                  pl.BlockSpec(w_uk_p.shape, lambda bi, i: (0, 0, 0)),
                  pl.BlockSpec((tm, LANE), lambda bi, i: (i, 0)), pl.BlockSpec((tm, LANE), lambda bi, i: (i, 0))],
        out_specs=[pl.BlockSpec((1, H_C, tm, KVW), lambda bi, i: (bi, 0, i, 0)),
                   pl.BlockSpec((1, tm, KVW), lambda bi, i: (bi, i, 0))],
        out_shape=[jax.ShapeDtypeStruct((b, H_C, l, KVW), BF16), jax.ShapeDtypeStruct((b, l, KVW), F32)],
        compiler_params=pltpu.CompilerParams(dimension_semantics=("parallel", "arbitrary"),
                                             vmem_limit_bytes=VMEM_LIMIT),
        name="mla_prep",
    )(h, norm_w.reshape(1, d), w_in_p, qnorm.reshape(1, -1), kvnorm.reshape(1, -1), w_uq_p, w_uk_p, cos, sin)


def _mla_body(lc, pad, q_ref, kv_ref, o_ref):
    i = pl.program_id(1)
    q = q_ref[0].reshape(H_C * LANE, KVW)
    kb = kv_ref[0, :lc, :].astype(BF16)
    s = lax.dot_general(q, kb, NT_DIMS, preferred_element_type=F32) * MLA_SCALE
    s = s.reshape(H_C, LANE, lc)
    krow = lax.broadcasted_iota(jnp.int32, (LANE, lc), 1)
    qrow = i * LANE + lax.broadcasted_iota(jnp.int32, (LANE, lc), 0)
    ok = (krow <= qrow) & (krow >= pad)
    s = jnp.where(ok[None], s, -1e30)
    mx = jnp.max(s, axis=-1, keepdims=True)
    p = jnp.exp(s - mx)
    p = (p * (1.0 / jnp.sum(p, axis=-1, keepdims=True))).astype(BF16).reshape(H_C * LANE, lc)
    o = jnp.dot(p, kb[:, :KV_LORA], preferred_element_type=F32).astype(o_ref.dtype)
    for h in range(H_C):
        o_ref[0, :, h * KV_LORA:(h + 1) * KV_LORA] = o[h * LANE:(h + 1) * LANE]


def _mla_kernel(variants, pad, *refs):
    _dispatch_variants(variants, _mla_body, pad, *refs)


def mla_prompt(q, kv, pad):
    b, _, l, _ = q.shape
    nq = l // LANE
    return pl.pallas_call(
        functools.partial(_mla_kernel, _causal_variants(nq), pad),
        grid=(b, nq),
        in_specs=[pl.BlockSpec((1, H_C, LANE, KVW), lambda bi, i: (bi, 0, i, 0)),
                  pl.BlockSpec((1, l, KVW), lambda bi, i: (bi, 0, 0))],
        out_specs=pl.BlockSpec((1, LANE, H_C * KV_LORA), lambda bi, i: (bi, i, 0)),
        out_shape=jax.ShapeDtypeStruct((b, l, H_C * KV_LORA), BF16),
        compiler_params=pltpu.CompilerParams(dimension_semantics=("parallel", "arbitrary"),
                                             vmem_limit_bytes=VMEM_LIMIT),
        name="mla_prompt",
    )(q, kv)


PAGES_PER_STEP = 8


def _sortable(x):
    bits = pltpu.bitcast(x + 0.0, jnp.int32)
    return bits ^ ((bits >> 31) & 0x7FFFFFFF)


def _gdn_decode_kernel(x_ref, conv_ref, small_ref, gate_ref, s_ref, convw_ref, alog_ref, dtb_ref, gnorm_ref,
                       o_ref, convo_ref, so_ref):
    nb = x_ref.shape[0]
    x = x_ref[...]
    cb = conv_ref[...]
    acc = x * convw_ref[CONV_W - 1:CONV_W, :]
    for j in range(CONV_W - 1):
        acc = acc + cb[:, j, :] * convw_ref[j:j + 1, :]
    for j in range(CONV_W - 2):
        convo_ref[:, j, :] = cb[:, j + 1, :]
    convo_ref[:, CONV_W - 2, :] = x
    y = acc * jax.nn.sigmoid(acc)
    q, k, v = y[:, :HD], y[:, HD:2 * HD], y[:, 2 * HD:]
    gi = lax.broadcasted_iota(jnp.int32, (HD, HD), 0) // DK_A
    gj = lax.broadcasted_iota(jnp.int32, (HD, HD), 1) // DK_A
    gmat = jnp.where(gi == gj, 1.0, 0.0).astype(BF16)
    q = q * lax.rsqrt(_dot_exact_rhs(q * q, gmat) + EPS) * (DK_A ** -0.5)
    k = k * lax.rsqrt(_dot_exact_rhs(k * k, gmat) + EPS)
    sm = small_ref[...]
    zs = sm + dtb_ref[...]
    g = -jnp.exp(alog_ref[...]) * (jnp.maximum(zs, 0.0) + jnp.log(1.0 + jnp.exp(-jnp.abs(zs))))
    eg = jnp.exp(g)
    beta = jax.nn.sigmoid(sm)
    rowid = lax.broadcasted_iota(jnp.int32, (nb, 1), 0)

    def body(bb, o_acc):
        mine = rowid == bb
        outs = []
        for h in range(H_A):
            sl = slice(h * DK_A, (h + 1) * DK_A)
            s = s_ref[bb, h]
            egh = eg[:, SM_ALPHA + h:SM_ALPHA + h + 1]
            eg_b = jnp.sum(jnp.where(mine, egh, 0.0), axis=0, keepdims=True)
            delta = (v[:, sl] - _hdot(k[:, sl], s) * egh) * beta[:, SM_BRAW + h:SM_BRAW + h + 1]
            s_new = s * eg_b + lax.dot_general(jnp.where(mine, k[:, sl], 0.0), jnp.where(mine, delta, 0.0), TN_DIMS,
                                               precision=HI, preferred_element_type=F32)
            so_ref[bb, h] = s_new
            outs.append(_hdot(q[:, sl], s_new))
        return jnp.where(mine, jnp.concatenate(outs, axis=1), o_acc)

    o = lax.fori_loop(0, nb, body, jnp.zeros((nb, HD), F32))
    ms = _dot_exact_rhs(o * o, gmat) * (1.0 / DV_A)
    gate = gate_ref[...]
    o_ref[...] = (o * lax.rsqrt(ms + EPS) * gnorm_ref[...] * (gate * jax.nn.sigmoid(gate))).astype(o_ref.dtype)


def gdn_decode_step(proj, conv_state, ssm_state, conv_w, a_log, dt_bias, gnorm, nb=8):
    b = proj.shape[0]
    alog_p = jnp.zeros((1, LANE), F32).at[0, SM_ALPHA:SM_ALPHA + H_A].set(a_log)
    dtb_p = jnp.zeros((1, LANE), F32).at[0, SM_ALPHA:SM_ALPHA + H_A].set(dt_bias)
    gn_t = jnp.tile(gnorm, H_A).reshape(1, HD)
    blk = lambda w, col: pl.BlockSpec((nb, w), lambda i: (i, col // w))
    const = lambda shp: pl.BlockSpec(shp, lambda i: (0, 0))
    cspec = pl.BlockSpec((nb, CONV_W - 1, QKV_A), lambda i: (i, 0, 0))
    sspec = pl.BlockSpec((nb, H_A, DK_A, DV_A), lambda i: (i, 0, 0, 0))
    return pl.pallas_call(
        _gdn_decode_kernel,
        grid=(b // nb,),
        in_specs=[blk(QKV_A, COL_QKV), cspec, blk(LANE, COL_SMALL), blk(HD, COL_GATE), sspec,
                  const((CONV_W, QKV_A)), const((1, LANE)), const((1, LANE)), const((1, HD))],
        out_specs=[pl.BlockSpec((nb, HD), lambda i: (i, 0)), cspec, sspec],
        out_shape=[jax.ShapeDtypeStruct((b, HD), BF16), jax.ShapeDtypeStruct(conv_state.shape, F32),
                   jax.ShapeDtypeStruct(ssm_state.shape, F32)],
        compiler_params=pltpu.CompilerParams(dimension_semantics=("parallel",), vmem_limit_bytes=VMEM_LIMIT),
        name="gdn_decode",
    )(proj, conv_state, proj, proj, ssm_state, conv_w, alog_p, dtb_p, gn_t)


def _rows_to_heads(row, width):
    return jnp.concatenate([row[:, h * width:(h + 1) * width] for h in range(row.shape[1] // width)], axis=0)


def _dsa_select_kernel(topk, n_pages, pt_ref, qi_ref, small_ref, *rest):
    g_pages = PAGES_PER_STEP
    ki_refs = rest[:g_pages]
    mask_ref, mnew_ref, sc_s = rest[g_pages:]
    j = pl.program_id(1)
    qi8 = _rows_to_heads(qi_ref[0], D_IDX).astype(BF16)
    sm = small_ref[0]
    pick = (lax.broadcasted_iota(jnp.int32, (H_IDX, LANE), 1)
            == lax.broadcasted_iota(jnp.int32, (H_IDX, LANE), 0) + SM_WI)
    w8 = jnp.sum(jnp.where(pick, sm, 0.0), axis=1, keepdims=True) * (H_IDX ** -0.5)
    for g in range(g_pages):
        d = lax.dot_general(qi8, ki_refs[g][0].astype(BF16), NT_DIMS, preferred_element_type=F32)
        sc_s[pl.ds(j * g_pages + g, 1), :] = jnp.sum(jnp.maximum(d * (D_IDX ** -0.5), 0.0) * w8, axis=0, keepdims=True)

    @pl.when(j == pl.num_programs(1) - 1)
    def _():
        ki_new = sm[:, SM_KI:SM_KI + D_IDX].astype(BF16).astype(F32)
        d_new = jnp.sum(qi8.astype(F32) * ki_new, axis=1, keepdims=True)
        s_new = jnp.sum(jnp.maximum(d_new * (D_IDX ** -0.5), 0.0) * w8, axis=0, keepdims=True)
        key = _sortable(sc_s[...])
        key_n = _sortable(s_new)
        idx = lax.broadcasted_iota(jnp.int32, key.shape, 0) * PAGE_SIZE + lax.broadcasted_iota(jnp.int32, key.shape, 1)
        idx_n = n_pages * PAGE_SIZE
        kf = jnp.float32(topk)

        def count(m, mn):
            c = jnp.sum(jnp.sum(jnp.where(m, 1.0, 0.0), axis=0, keepdims=True), axis=1, keepdims=True)
            return c + jnp.where(mn, 1.0, 0.0)

        t0 = jnp.where(count(key >= 0, key_n >= 0) >= kf, 0, INT_MIN).astype(jnp.int32)

        def tbody(n, t):
            cand = t | jnp.left_shift(jnp.int32(1), 30 - n)
            return jnp.where(count(key >= cand, key_n >= cand) >= kf, cand, t)

        t = lax.fori_loop(0, 31, tbody, t0)
        m = kf - count(key > t, key_n > t)
        idx_bits = (n_pages * PAGE_SIZE).bit_length()

        def jbody(n, jj):
            cand = jj | jnp.left_shift(jnp.int32(1), idx_bits - 1 - n)
            return jnp.where(count((key == t) & (idx < cand), (key_n == t) & (idx_n < cand)) <= m, cand, jj)

        jj = lax.fori_loop(0, idx_bits, jbody, jnp.zeros_like(t))
        mask_ref[0] = jnp.where((key > t) | ((key == t) & (idx < jj)), 1.0, 0.0)
        sel_n = (key_n > t) | ((key_n == t) & (idx_n < jj))
        mnew_ref[0] = jnp.broadcast_to(jnp.where(sel_n, 1.0, 0.0), (1, LANE))


def dsa_decode_select(proj3, cache_ki, page_table, topk):
    b = proj3.shape[0]
    n_pages = page_table.shape[1]
    g_pages = PAGES_PER_STEP
    page_spec = lambda g: pl.BlockSpec((1, PAGE_SIZE, D_IDX), lambda bi, j, pt: (pt[bi, j * g_pages + g], 0, 0))
    grid_spec = pltpu.PrefetchScalarGridSpec(
        num_scalar_prefetch=1,
        grid=(b, n_pages // g_pages),
        in_specs=[pl.BlockSpec((1, 1, HD), lambda bi, j, pt: (bi, 0, COL_QI // HD)),
                  pl.BlockSpec((1, 1, LANE), lambda bi, j, pt: (bi, 0, COL_SMALL // LANE))]
                 + [page_spec(g) for g in range(g_pages)],
        out_specs=[pl.BlockSpec((1, n_pages, PAGE_SIZE), lambda bi, j, pt: (bi, 0, 0)),
                   pl.BlockSpec((1, 1, LANE), lambda bi, j, pt: (bi, 0, 0))],
        scratch_shapes=[pltpu.VMEM((n_pages, PAGE_SIZE), F32)],
    )
    return pl.pallas_call(
        functools.partial(_dsa_select_kernel, topk, n_pages),
        grid_spec=grid_spec,
        out_shape=[jax.ShapeDtypeStruct((b, n_pages, PAGE_SIZE), F32), jax.ShapeDtypeStruct((b, 1, LANE), F32)],
        compiler_params=pltpu.CompilerParams(dimension_semantics=("parallel", "arbitrary"),
                                             vmem_limit_bytes=VMEM_LIMIT),
        name="dsa_decode_select",
    )(page_table, proj3, proj3, *([cache_ki] * g_pages))


def _head_block_mask():
    return lax.broadcasted_iota(jnp.int32, (H_B, HD), 1) // DH_B == lax.broadcasted_iota(jnp.int32, (H_B, HD), 0)


def _online_update(lg, ok, m_s, l_s):
    lg = jnp.where(ok, lg, -1e30)
    m_new = jnp.maximum(m_s[...], jnp.max(lg, axis=1, keepdims=True))
    alpha = jnp.exp(m_s[...] - m_new)
    p = jnp.where(ok, jnp.exp(lg - m_new), 0.0)
    l_s[...] = l_s[...] * alpha + jnp.sum(p, axis=1, keepdims=True)
    m_s[...] = m_new
    return p, alpha


def _dsa_attend_kernel(pt_ref, q_ref, kn_ref, vn_ref, mask_ref, mnew_ref, bias_ref, bias0_ref, *rest):
    g_pages = PAGES_PER_STEP
    k_refs, v_refs = rest[:g_pages], rest[g_pages:2 * g_pages]
    o_ref, qbd_s, m_s, l_s, acc_s = rest[2 * g_pages:]
    j = pl.program_id(1)
    bd = _head_block_mask()

    @pl.when(j == 0)
    def _():
        qbd_s[...] = jnp.where(bd, jnp.broadcast_to(q_ref[0], (H_B, HD)), 0.0).astype(BF16)
        m_s[...] = jnp.full_like(m_s, -1e30)
        l_s[...] = jnp.zeros_like(l_s)
        acc_s[...] = jnp.zeros_like(acc_s)

    for g in range(g_pages):
        page = j * g_pages + g
        lg = lax.dot_general(qbd_s[...], k_refs[g][0].astype(BF16), NT_DIMS, preferred_element_type=F32)
        lg = lg * (DH_B ** -0.5) + bias_ref[page]
        ok = mask_ref[0, pl.ds(page, 1), :] > 0.0
        p, alpha = _online_update(lg, ok, m_s, l_s)
        acc_s[...] = acc_s[...] * alpha + jnp.dot(p.astype(BF16), v_refs[g][0].astype(BF16), preferred_element_type=F32)

    @pl.when(j == pl.num_programs(1) - 1)
    def _():
        kn = kn_ref[0].astype(BF16).astype(F32)
        lgn = jnp.sum(qbd_s[...].astype(F32) * kn, axis=1, keepdims=True) * (DH_B ** -0.5) + bias0_ref[:, 0:1]
        okn = mnew_ref[0][:, 0:1] > 0.0
        pn, alpha = _online_update(lgn, okn, m_s, l_s)
        acc = acc_s[...] * alpha + pn.astype(BF16).astype(F32) * vn_ref[0].astype(BF16).astype(F32)
        o = acc * (1.0 / l_s[...])
        o_ref[0] = jnp.sum(jnp.where(bd, o, 0.0), axis=0, keepdims=True).astype(o_ref.dtype)


def dsa_decode_attend(proj3, mask, mnew, cache_k, cache_v, page_table, bias_dec, bias0):
    b = proj3.shape[0]
    n_pages = page_table.shape[1]
    g_pages = PAGES_PER_STEP
    page_spec = lambda g: pl.BlockSpec((1, PAGE_SIZE, HD), lambda bi, j, pt: (pt[bi, j * g_pages + g], 0, 0))
    row = lambda col: pl.BlockSpec((1, 1, HD), lambda bi, j, pt: (bi, 0, col // HD))
    grid_spec = pltpu.PrefetchScalarGridSpec(
        num_scalar_prefetch=1,
        grid=(b, n_pages // g_pages),
        in_specs=[row(COL_QB), row(COL_KB), row(COL_VB),
                  pl.BlockSpec((1, n_pages, PAGE_SIZE), lambda bi, j, pt: (bi, 0, 0)),
                  pl.BlockSpec((1, 1, LANE), lambda bi, j, pt: (bi, 0, 0)),
                  pl.BlockSpec(bias_dec.shape, lambda bi, j, pt: (0, 0, 0)),
                  pl.BlockSpec(bias0.shape, lambda bi, j, pt: (0, 0))]
                 + [page_spec(g) for g in range(g_pages)] * 2,
        out_specs=pl.BlockSpec((1, 1, HD), lambda bi, j, pt: (bi, 0, 0)),
        scratch_shapes=[pltpu.VMEM((H_B, HD), BF16), pltpu.VMEM((H_B, 1), F32), pltpu.VMEM((H_B, 1), F32),
                        pltpu.VMEM((H_B, HD), F32)],
    )
    return pl.pallas_call(
        _dsa_attend_kernel,
        grid_spec=grid_spec,
        out_shape=jax.ShapeDtypeStruct((b, 1, HD), BF16),
        compiler_params=pltpu.CompilerParams(dimension_semantics=("parallel", "arbitrary"),
                                             vmem_limit_bytes=VMEM_LIMIT),
        name="dsa_decode_attend",
    )(page_table, proj3, proj3, proj3, mask, mnew, bias_dec, bias0, *([cache_k] * g_pages), *([cache_v] * g_pages))


def decode_bias(rel_bias, n_pages):
    key_pos = jnp.arange(n_pages * PAGE_SIZE, dtype=jnp.int32)
    bias = rel_bias[rel_bucket(n_pages * PAGE_SIZE - key_pos)]
    bias_dec = jnp.transpose(bias.reshape(n_pages, PAGE_SIZE, H_B), (0, 2, 1)).astype(F32)
    bias0 = jnp.broadcast_to(rel_bias[rel_bucket(jnp.zeros((), jnp.int32))].astype(F32)[:, None], (H_B, LANE))
    return bias_dec, bias0


def _mla_decode_kernel(pt_ref, q_ref, kvn_ref, exp_ref, *rest):
    g_pages = PAGES_PER_STEP
    c_refs, kpe_refs = rest[:g_pages], rest[g_pages:2 * g_pages]
    o_ref, m_s, l_s, acc_s = rest[2 * g_pages:]
    j = pl.program_id(1)

    @pl.when(j == 0)
    def _():
        m_s[...] = jnp.full_like(m_s, -1e30)
        l_s[...] = jnp.zeros_like(l_s)
        acc_s[...] = jnp.zeros_like(acc_s)

    q = q_ref[0]
    yes = jnp.full((1, 1), True)
    for g in range(g_pages):
        cb = c_refs[g][0].astype(BF16)
        kpe_e = jnp.dot(kpe_refs[g][0].astype(BF16), exp_ref[...], preferred_element_type=F32).astype(BF16)
        kfull = jnp.concatenate([cb, kpe_e], axis=1)
        lg = lax.dot_general(q, kfull, NT_DIMS, preferred_element_type=F32) * MLA_SCALE
        p, alpha = _online_update(lg, yes, m_s, l_s)
        acc_s[...] = acc_s[...] * alpha + jnp.dot(p.astype(BF16), cb, preferred_element_type=F32)

    @pl.when(j == pl.num_programs(1) - 1)
    def _():
        kvn = kvn_ref[0].astype(BF16).astype(F32)
        lgn = jnp.sum(q.astype(F32) * kvn, axis=1, keepdims=True) * MLA_SCALE
        pn, alpha = _online_update(lgn, yes, m_s, l_s)
        acc = acc_s[...] * alpha + pn.astype(BF16).astype(F32) * kvn[:, :KV_LORA]
        o = acc * (1.0 / l_s[...])
        o_ref[0] = jnp.concatenate([o[h:h + 1] for h in range(H_C)], axis=1).astype(o_ref.dtype)


def mla_decode(q, kvn, cache_c, cache_kpe, page_table):
    b = q.shape[0]
    n_pages = page_table.shape[1]
    g_pages = PAGES_PER_STEP
    src = jnp.concatenate([jnp.tile(jnp.arange(HALF), H_C), HALF + jnp.tile(jnp.arange(HALF), H_C)])
    expand = (jnp.arange(QK_ROPE)[:, None] == src[None, :]).astype(BF16)
    cspec = lambda g: pl.BlockSpec((1, PAGE_SIZE, KV_LORA), lambda bi, j, pt: (pt[bi, j * g_pages + g], 0, 0))
    kspec = lambda g: pl.BlockSpec((1, PAGE_SIZE, QK_ROPE), lambda bi, j, pt: (pt[bi, j * g_pages + g], 0, 0))
    grid_spec = pltpu.PrefetchScalarGridSpec(
        num_scalar_prefetch=1,
        grid=(b, n_pages // g_pages),
        in_specs=[pl.BlockSpec((1, H_C, KVW), lambda bi, j, pt: (bi, 0, 0)),
                  pl.BlockSpec((1, 1, KVW), lambda bi, j, pt: (bi, 0, 0)),
                  pl.BlockSpec(expand.shape, lambda bi, j, pt: (0, 0))]
                 + [cspec(g) for g in range(g_pages)] + [kspec(g) for g in range(g_pages)],
        out_specs=pl.BlockSpec((1, 1, H_C * KV_LORA), lambda bi, j, pt: (bi, 0, 0)),
        scratch_shapes=[pltpu.VMEM((H_C, 1), F32), pltpu.VMEM((H_C, 1), F32), pltpu.VMEM((H_C, KV_LORA), F32)],
    )
    return pl.pallas_call(
        _mla_decode_kernel,
        grid_spec=grid_spec,
        out_shape=jax.ShapeDtypeStruct((b, 1, H_C * KV_LORA), BF16),
        compiler_params=pltpu.CompilerParams(dimension_semantics=("parallel", "arbitrary"),
                                             vmem_limit_bytes=VMEM_LIMIT),
        name="mla_decode",
    )(page_table, q, kvn, expand, *([cache_c] * g_pages), *([cache_kpe] * g_pages))


def _prep_w_ab(w):
    qkv_a, alpha, braw, gate, q_b, k_b, v_b, qi, wi, ki = split_cols(w, AB_SIZES)
    z = jnp.zeros((w.shape[0], SM_KI - SM_WI - H_IDX), w.dtype)
    return jnp.concatenate([qkv_a, gate, q_b, k_b, v_b, qi, alpha, braw, wi, z, ki], axis=1).astype(BF16)


def _prep_w_c(w_in, w_uq, w_uk):
    cq, ckv, kpe = split_cols(w_in, C_SIZES)
    w_in_p = jnp.concatenate([cq, ckv, jnp.tile(kpe[:, :HALF], (1, H_C)), jnp.tile(kpe[:, HALF:], (1, H_C))], axis=1)
    wq = w_uq.reshape(Q_LORA, H_C, QK_NOPE + QK_ROPE)
    nope = jnp.pad(wq[:, :, :QK_NOPE], ((0, 0), (0, 0), (0, LANE - QK_NOPE))).reshape(Q_LORA, H_C * LANE)
    r1 = wq[:, :, QK_NOPE:QK_NOPE + HALF].reshape(Q_LORA, H_C * HALF)
    r2 = wq[:, :, QK_NOPE + HALF:].reshape(Q_LORA, H_C * HALF)
    w_uq_p = jnp.concatenate([nope, r1, r2], axis=1)
    w_uk_p = jnp.pad(jnp.transpose(w_uk, (1, 2, 0)), ((0, 0), (0, LANE - QK_NOPE), (0, 0)))
    return w_in_p.astype(BF16), w_uq_p.astype(BF16), w_uk_p.astype(BF16)


def _rope_tables(pos):
    inv = ROPE_BASE ** (-jnp.arange(HALF, dtype=jnp.float32) / HALF)
    ang = pos.astype(jnp.float32)[:, None] * inv[None, :]
    return jnp.tile(jnp.cos(ang), (1, LANE // HALF)), jnp.tile(jnp.sin(ang), (1, LANE // HALF))


def kernel(x_prompt, x_sample, state_a_conv, state_a_ssm, cache_b_k, cache_b_v, cache_b_kidx,
           cache_c_latent, cache_c_kpe, page_table, meta_tokens, rel_bias, norm_mix, norm_ffn, norm_final,
           w_in_ab, conv_a, a_log, dt_bias_a, gnorm_a, w_out_ab, w_in_c, qnorm_c, kvnorm_c,
           w_uq_c, w_uk_c, w_uv_c, w_out_c, w_up, w_down):
    w_up_b = w_up.astype(BF16)
    w_down_b = w_down.astype(BF16)
    w_out_ab_b = w_out_ab.astype(BF16)
    w_out_c_b = w_out_c.astype(BF16)
    w_uv_b = jnp.transpose(w_uv_c, (0, 2, 1, 3)).astype(BF16)

    b, seq, d = x_prompt.shape
    lp = N_META + seq
    pad = (-lp) % LANE
    l = pad + lp
    n = b * l
    topk = min(TOPK_MAX, SEQ // 4)
    meta = jnp.broadcast_to(meta_tokens[None].astype(x_prompt.dtype), (b, N_META, d))
    h = jnp.concatenate([jnp.zeros((b, pad, d), x_prompt.dtype), meta, x_prompt], axis=1).reshape(n, d)
    cos_p, sin_p = _rope_tables(jnp.arange(l, dtype=jnp.int32) - pad)
    a_conv_p, a_ssm_p, b_k_p, b_v_p, b_kidx_p, c_lat_p, c_kpe_p = [], [], [], [], [], [], []
    for li in range(DEPTH):
        j = li // 2
        last = li == DEPTH - 1
        if li % 2 == 0:
            proj = rms_linear(h, norm_mix[li], _prep_w_ab(w_in_ab[j])).reshape(b, l, AB_COLS_P)
            o_a, s_fin = gdn_prompt(proj, conv_a[j], a_log[j], dt_bias_a[j], gnorm_a[j], pad)
            o_b = dsa_prompt(proj, bias_tiles(rel_bias), pad, topk)
            projs = [(o_a.reshape(n, HD), w_out_ab_b[j, :HD]), (o_b.reshape(n, HD), w_out_ab_b[j, HD:])]
            head_w = None
            a_conv_p.append(proj[:, l - (CONV_W - 1):, COL_QKV:COL_QKV + QKV_A])
            a_ssm_p.append(s_fin)
            b_k_p.append(proj[:, pad:, COL_KB:COL_KB + HD].reshape(b, lp, H_B, DH_B))
            b_v_p.append(proj[:, pad:, COL_VB:COL_VB + HD].reshape(b, lp, H_B, DH_B))
            b_kidx_p.append(proj[:, pad:, COL_SMALL + SM_KI:COL_SMALL + SM_KI + D_IDX])
        else:
            w_in_p, w_uq_p, w_uk_p = _prep_w_c(w_in_c[j], w_uq_c[j], w_uk_c[j])
            q_all, kv = mla_prep(h.reshape(b, l, d), norm_mix[li], w_in_p, qnorm_c[j], kvnorm_c[j], w_uq_p, w_uk_p,
                                 cos_p, sin_p)
            o_lat = mla_prompt(q_all, kv, pad)
            projs = [(o_lat.reshape(n, H_C * KV_LORA), w_out_c_b[j])]
            head_w = w_uv_b[j]
            c_lat_p.append(kv[:, pad:, :KV_LORA])
            c_kpe_p.append(jnp.concatenate([kv[:, pad:, KV_LORA:KV_LORA + HALF],
                                            kv[:, pad:, KV_LORA + LANE:KV_LORA + LANE + HALF]], axis=-1))
        h = proj_mlp(h, projs, norm_ffn[li], w_up_b[li], w_down_b[li], norm_final if last else None, head_w)
    y_p = h.reshape(b, l, d)[:, pad + N_META:]

    bs = x_sample.shape[0]
    n_pages = page_table.shape[1]
    n_pool = cache_b_k.shape[1]
    past = n_pages * PAGE_SIZE
    topk_s = min(TOPK_MAX, (past + DEC_SEQ) // 4)
    hs = x_sample.reshape(bs, d)
    cos_s, sin_s = _rope_tables(jnp.full((bs,), past, jnp.int32))
    a_conv_s, a_ssm_s, b_k_s, b_v_s, b_kidx_s, c_lat_s, c_kpe_s = [], [], [], [], [], [], []
    for li in range(DEPTH):
        j = li // 2
        last = li == DEPTH - 1
        if li % 2 == 0:
            proj = rms_linear(hs, norm_mix[li], _prep_w_ab(w_in_ab[j]))
            proj3 = proj.reshape(bs, 1, AB_COLS_P)
            o_a, conv_new, s_new = gdn_decode_step(proj, state_a_conv[j], state_a_ssm[j], conv_a[j], a_log[j],
                                                   dt_bias_a[j], gnorm_a[j])
            mask, mnew = dsa_decode_select(proj3, cache_b_kidx[j], page_table, topk_s)
            bias_dec, bias0 = decode_bias(rel_bias, n_pages)
            o_b = dsa_decode_attend(proj3, mask, mnew, cache_b_k[j].reshape(n_pool, PAGE_SIZE, HD),
                                    cache_b_v[j].reshape(n_pool, PAGE_SIZE, HD), page_table, bias_dec, bias0)
            projs = [(o_a, w_out_ab_b[j, :HD]), (o_b.reshape(bs, HD), w_out_ab_b[j, HD:])]
            head_w = None
            a_conv_s.append(conv_new)
            a_ssm_s.append(s_new)
            b_k_s.append(proj[:, COL_KB:COL_KB + HD].reshape(bs, DEC_SEQ, H_B, DH_B))
            b_v_s.append(proj[:, COL_VB:COL_VB + HD].reshape(bs, DEC_SEQ, H_B, DH_B))
            b_kidx_s.append(proj[:, COL_SMALL + SM_KI:COL_SMALL + SM_KI + D_IDX].reshape(bs, DEC_SEQ, D_IDX))
        else:
            w_in_p, w_uq_p, w_uk_p = _prep_w_c(w_in_c[j], w_uq_c[j], w_uk_c[j])
            q_all, kv = mla_prep(hs.reshape(1, bs, d), norm_mix[li], w_in_p, qnorm_c[j], kvnorm_c[j], w_uq_p, w_uk_p,
                                 cos_s, sin_s)
            o_lat = mla_decode(jnp.transpose(q_all[0], (1, 0, 2)), kv.reshape(bs, 1, KVW), cache_c_latent[j],
                               cache_c_kpe[j], page_table)
            projs = [(o_lat.reshape(bs, H_C * KV_LORA), w_out_c_b[j])]
            head_w = w_uv_b[j]
            c_lat_s.append(kv[0, :, :KV_LORA].reshape(bs, DEC_SEQ, KV_LORA))
            c_kpe_s.append(jnp.concatenate([kv[0, :, KV_LORA:KV_LORA + HALF],
                                            kv[0, :, KV_LORA + LANE:KV_LORA + LANE + HALF]], axis=-1).reshape(bs, DEC_SEQ, QK_ROPE))
        hs = proj_mlp(hs, projs, norm_ffn[li], w_up_b[li], w_down_b[li], norm_final if last else None, head_w)
    y_s = hs.reshape(bs, DEC_SEQ, d)

    stack = jnp.stack
    return (y_p, y_s,
            stack(a_conv_p), stack(a_ssm_p), stack(b_k_p), stack(b_v_p), stack(b_kidx_p), stack(c_lat_p), stack(c_kpe_p),
            stack(a_conv_s), stack(a_ssm_s), stack(b_k_s), stack(b_v_s), stack(b_kidx_s), stack(c_lat_s), stack(c_kpe_s))
```

```python
import functools
import math
import jax
import jax.numpy as jnp
from jax import lax
from jax.experimental import pallas as pl
from jax.experimental.pallas import tpu as pltpu

D_MODEL = 1024
BATCH = 8
SEQ = 2048
DEPTH = 2
DEC_BATCH = 32
DEC_SEQ = 1
PAST_LEN = 16384
PAGE_SIZE = 128

N_META = 16
EPS = 1e-6
H_A = 8
DK_A = 64
DV_A = 64
CONV_W = 4
H_B = 8
DH_B = 64
H_IDX = 8
D_IDX = 64
TOPK_MAX = 256
REL_BUCKETS = 32
REL_MAX_DIST = 1024
H_C = 8
Q_LORA = 384
KV_LORA = 256
QK_NOPE = 64
QK_ROPE = 32
V_C = 128
ROPE_BASE = 10000.0
D_FF = 4 * D_MODEL

QKV_A = H_A * (2 * DK_A + DV_A)
AB_SIZES = (QKV_A, H_A, H_A, H_A * DV_A, H_B * DH_B, H_B * DH_B, H_B * DH_B, H_IDX * D_IDX, H_IDX, D_IDX)
C_SIZES = (Q_LORA, KV_LORA, QK_ROPE)
MLA_SCALE = (QK_NOPE + QK_ROPE) ** -0.5

BF16 = jnp.bfloat16
F32 = jnp.float32
LANE = 128
VMEM_LIMIT = 56 * 1024 * 1024
INT_MIN = -(2 ** 31)
NT_DIMS = (((1,), (1,)), ((), ()))
TN_DIMS = (((0,), (0,)), ((), ()))
HI = lax.Precision.HIGHEST
HD = H_A * DK_A
HALF = QK_ROPE // 2

COL_QKV, COL_GATE, COL_QB, COL_KB, COL_VB, COL_QI, COL_SMALL = 0, 1536, 2048, 2560, 3072, 3584, 4096
AB_COLS_P = COL_SMALL + LANE
SM_ALPHA, SM_BRAW, SM_WI, SM_KI = 0, 8, 16, 64
CC_Q, CC_KV, CC_K1, CC_K2, C_COLS_P = 0, 384, 640, 768, 896
KVW = KV_LORA + 2 * LANE
N_BIAS_TILES = 9
DEC_PAGES = 16


def split_cols(x, sizes):
    out, off = [], 0
    for s in sizes:
        out.append(x[..., off:off + s])
        off += s
    return out


def rel_bucket(dist):
    dist = jnp.maximum(dist, 0)
    max_exact = REL_BUCKETS // 2
    log_ratio = jnp.log(jnp.maximum(dist, 1).astype(jnp.float32) / max_exact) / math.log(REL_MAX_DIST / max_exact)
    large = max_exact + (log_ratio * (REL_BUCKETS - max_exact)).astype(jnp.int32)
    return jnp.where(dist < max_exact, dist, jnp.minimum(large, REL_BUCKETS - 1))


def _rms(x, w):
    return x * lax.rsqrt(jnp.mean(x * x, axis=-1, keepdims=True) + EPS) * w


def _hdot(a, b):
    return jnp.dot(a, b, precision=HI, preferred_element_type=F32)


def _dot_exact_rhs(a, b_bf16):
    a_hi = a.astype(BF16)
    a_lo = (a - a_hi.astype(F32)).astype(BF16)
    return jnp.dot(a_hi, b_bf16, preferred_element_type=F32) + jnp.dot(a_lo, b_bf16, preferred_element_type=F32)


def _split2(a):
    hi = a.astype(BF16)
    return hi, (a - hi.astype(F32)).astype(BF16)


def _dot3(ap, bp):
    (ah, al), (bh, bl) = ap, bp
    return (jnp.dot(ah, bh, preferred_element_type=F32) + jnp.dot(al, bh, preferred_element_type=F32)
            + jnp.dot(ah, bl, preferred_element_type=F32))


def _sortable(x):
    bits = pltpu.bitcast(x + 0.0, jnp.int32)
    return bits ^ ((bits >> 31) & 0x7FFFFFFF)


def _rows_to_heads(row, width):
    return jnp.concatenate([row[:, h * width:(h + 1) * width] for h in range(row.shape[1] // width)], axis=0)


def _col_reduce(x, op, groups=8):
    l, w = x.shape
    if l % (8 * groups) == 0:
        x = op(op(x.reshape(groups, l // (8 * groups), 8, w), axis=1), axis=0)
    return op(x, axis=0, keepdims=True)


def _causal_variants(nq):
    n_var = min(4, nq)
    his = [nq - (n_var - 1 - v) * (nq // n_var) for v in range(n_var)]
    return [(hi, hi * LANE) for hi in his]


def _dispatch_variants(variants, body, *args):
    i = pl.program_id(1)
    lo = 0
    for hi, lc in variants:
        pl.when((i >= lo) & (i < hi))(functools.partial(body, lc, *args))
        lo = hi


def _rms_linear_kernel(x_ref, nw_ref, w_ref, o_ref):
    xn = _rms(x_ref[...], nw_ref[...]).astype(BF16)
    o_ref[...] = jnp.dot(xn, w_ref[...], preferred_element_type=F32)


def rms_linear(x, norm_w, w, tm=256):
    n, d = x.shape
    m = w.shape[1]
    tm = min(tm, n)
    return pl.pallas_call(
        _rms_linear_kernel,
        grid=(pl.cdiv(n, tm),),
        in_specs=[pl.BlockSpec((tm, d), lambda i: (i, 0)), pl.BlockSpec((1, d), lambda i: (0, 0)),
                  pl.BlockSpec((d, m), lambda i: (0, 0))],
        out_specs=pl.BlockSpec((tm, m), lambda i: (i, 0)),
        out_shape=jax.ShapeDtypeStruct((n, m), F32),
        compiler_params=pltpu.CompilerParams(dimension_semantics=("parallel",), vmem_limit_bytes=VMEM_LIMIT),
        name="rms_linear",
    )(x, norm_w.reshape(1, d), w)


def _proj_mlp_kernel(n_proj, n_heads, final, *refs):
    refs = list(refs)
    h_ref = refs.pop(0)
    a_refs = [refs.pop(0) for _ in range(n_proj)]
    w_refs = [refs.pop(0) for _ in range(n_proj)]
    hw_ref = refs.pop(0) if n_heads else None
    nw_ref, wup_ref, wdn_ref = refs.pop(0), refs.pop(0), refs.pop(0)
    nf_ref = refs.pop(0) if final else None
    o_ref, h1_s, xn_s, acc_s = refs
    f = pl.program_id(1)

    @pl.when(f == 0)
    def _():
        h1 = h_ref[...]
        for a_ref, w_ref in zip(a_refs, w_refs):
            a = a_ref[...]
            if n_heads:
                kh = a.shape[1] // n_heads
                a = jnp.concatenate([jnp.dot(a[:, hh * kh:(hh + 1) * kh].astype(BF16), hw_ref[hh],
                                             preferred_element_type=F32) for hh in range(n_heads)], axis=1)
            h1 = h1 + jnp.dot(a.astype(BF16), w_ref[...], preferred_element_type=F32)
        h1_s[...] = h1
        xn_s[...] = _rms(h1, nw_ref[...]).astype(BF16)
        acc_s[...] = jnp.zeros_like(acc_s)

    u = jnp.dot(xn_s[...], wup_ref[...], preferred_element_type=F32)
    u = jnp.square(jnp.maximum(u, 0.0)).astype(BF16)
    acc_s[...] += jnp.dot(u, wdn_ref[...], preferred_element_type=F32)

    @pl.when(f == pl.num_programs(1) - 1)
    def _():
        h2 = h1_s[...] + acc_s[...]
        if final:
            h2 = _rms(h2, nf_ref[...])
        o_ref[...] = h2


def proj_mlp(h, projs, norm_w, w_up, w_down, norm_final=None, head_w=None, tm=512, tf=1024):
    n, d = h.shape
    ff = w_up.shape[1]
    tm = min(tm, n)
    final = norm_final is not None
    n_heads = 0 if head_w is None else head_w.shape[0]
    row = lambda i, f: (i, 0)
    const = lambda i, f: (0, 0)
    in_specs = [pl.BlockSpec((tm, d), row)]
    in_specs += [pl.BlockSpec((tm, a.shape[1]), row) for a, _ in projs]
    in_specs += [pl.BlockSpec(w.shape, const) for _, w in projs]
    args = [h] + [a for a, _ in projs] + [w for _, w in projs]
    if n_heads:
        in_specs.append(pl.BlockSpec(head_w.shape, lambda i, f: (0, 0, 0)))
        args.append(head_w)
    in_specs += [pl.BlockSpec((1, d), const), pl.BlockSpec((d, tf), lambda i, f: (0, f)),
                 pl.BlockSpec((tf, d), lambda i, f: (f, 0))]
    args += [norm_w.reshape(1, d), w_up, w_down]
    if final:
        in_specs.append(pl.BlockSpec((1, d), const))
        args.append(norm_final.reshape(1, d))
    return pl.pallas_call(
        functools.partial(_proj_mlp_kernel, len(projs), n_heads, final),
        grid=(pl.cdiv(n, tm), ff // tf),
        in_specs=in_specs,
        out_specs=pl.BlockSpec((tm, d), row),
        out_shape=jax.ShapeDtypeStruct((n, d), F32),
        scratch_shapes=[pltpu.VMEM((tm, d), F32), pltpu.VMEM((tm, d), BF16), pltpu.VMEM((tm, d), F32)],
        compiler_params=pltpu.CompilerParams(dimension_semantics=("parallel", "arbitrary"),
                                             vmem_limit_bytes=VMEM_LIMIT),
        name="proj_mlp",
    )(*args)


def _gdn_gates(sm, alog_ref, dtb_ref):
    zs = sm + dtb_ref[...]
    g = -jnp.exp(alog_ref[...]) * (jnp.maximum(zs, 0.0) + jnp.log(1.0 + jnp.exp(-jnp.abs(zs))))
    return g, jax.nn.sigmoid(sm)


def _head_group_matrix():
    gi = lax.broadcasted_iota(jnp.int32, (HD, HD), 0) // DK_A
    gj = lax.broadcasted_iota(jnp.int32, (HD, HD), 1) // DK_A
    return jnp.where(gi == gj, 1.0, 0.0).astype(BF16)


def _gdn_kernel(c_sz, pad, qkv_ref, small_ref, gate_ref, convw_ref, alog_ref, dtb_ref, gnorm_ref,
                o_ref, sfin_ref, s_s, prev_s):
    c = pl.program_id(1)

    @pl.when(c == 0)
    def _():
        s_s[...] = jnp.zeros_like(s_s)
        prev_s[...] = jnp.zeros_like(prev_s)

    row = lax.broadcasted_iota(jnp.int32, (c_sz, 1), 0)
    valid = (row + c * c_sz) >= pad
    x = jnp.where(valid, qkv_ref[0], 0.0)
    prevx = prev_s[...]
    acc = x * convw_ref[CONV_W - 1:CONV_W, :]
    for sft in range(1, CONV_W):
        shifted = jnp.where(row < sft, pltpu.roll(prevx, sft, 0), pltpu.roll(x, sft, 0))
        acc = acc + shifted * convw_ref[CONV_W - 1 - sft:CONV_W - sft, :]
    prev_s[...] = x
    y = acc * jax.nn.sigmoid(acc)
    q, k, v = y[:, :HD], y[:, HD:2 * HD], y[:, 2 * HD:]
    gmat = _head_group_matrix()
    q = q * lax.rsqrt(_dot_exact_rhs(q * q, gmat) + EPS) * (DK_A ** -0.5)
    k = k * lax.rsqrt(_dot_exact_rhs(k * k, gmat) + EPS)

    g, beta = _gdn_gates(small_ref[0], alog_ref, dtb_ref)
    g = jnp.where(valid, g, 0.0)
    beta = jnp.where(valid, beta, 0.0)
    ti = lax.broadcasted_iota(jnp.int32, (c_sz, c_sz), 0)
    tj = lax.broadcasted_iota(jnp.int32, (c_sz, c_sz), 1)
    incl = ti >= tj
    strict = ti > tj
    gc = _hdot(jnp.where(incl, 1.0, 0.0), g)
    gct = gc.T
    ei = lax.broadcasted_iota(jnp.int32, (LANE, HD), 0)
    ej = lax.broadcasted_iota(jnp.int32, (LANE, HD), 1) // DK_A
    gc_w = _hdot(gc, jnp.where(ei == ej + SM_ALPHA, 1.0, 0.0))
    beta_w = _dot_exact_rhs(beta, jnp.where(ei == ej + SM_BRAW, 1.0, 0.0).astype(BF16))
    gl_w = gc_w[c_sz - 1:c_sz, :]
    eg_w = jnp.exp(gc_w)
    q_s = q * eg_w
    k_b = k * (beta_w * eg_w)
    v_b = v * beta_w
    k_l = k * jnp.exp(gl_w - gc_w)
    s_old = s_s[...]
    s_old_b = s_old.astype(BF16)
    n_fac = max(1, (c_sz - 1).bit_length())
    sls = [slice(h * DK_A, (h + 1) * DK_A) for h in range(H_A)]
    k16, q16 = k.astype(BF16), q.astype(BF16)
    decays, ms, xss = [], [], []
    for h, sl in enumerate(sls):
        diff = gc[:, SM_ALPHA + h:SM_ALPHA + h + 1] - gct[SM_ALPHA + h:SM_ALPHA + h + 1, :]
        decay = jnp.exp(jnp.where(incl, diff, -1e30))
        kk = lax.dot_general(k16[:, sl], k16[:, sl], NT_DIMS, preferred_element_type=F32)
        ms.append(-jnp.where(strict, beta[:, SM_BRAW + h:SM_BRAW + h + 1] * kk * decay, 0.0))
        xss.append(jnp.concatenate([v_b[:, sl], k_b[:, sl]], axis=1))
        decays.append(decay)
    for s in range(n_fac):
        msp = [_split2(m) for m in ms]
        xss = [x + _dot3(mp, _split2(x)) for x, mp in zip(xss, msp)]
        if s < n_fac - 1:
            ms = [_dot3(mp, mp) for mp in msp]
    outs, s_upd = [], []
    for h, sl in enumerate(sls):
        xs = xss[h]
        sh = s_old_b[:, sl]
        v_new = xs[:, :DV_A] - jnp.dot(xs[:, DV_A:].astype(BF16), sh, preferred_element_type=F32)
        v_new_b = v_new.astype(BF16)
        qk = lax.dot_general(q16[:, sl], k16[:, sl], NT_DIMS, preferred_element_type=F32) * decays[h]
        outs.append(jnp.dot(q_s[:, sl].astype(BF16), sh, preferred_element_type=F32)
                    + jnp.dot(qk.astype(BF16), v_new_b, preferred_element_type=F32))
        s_upd.append(lax.dot_general(k_l[:, sl].astype(BF16), v_new_b, TN_DIMS, preferred_element_type=F32))
    s_new = s_old * jnp.exp(gl_w) + jnp.concatenate(s_upd, axis=1)
    s_s[...] = s_new
    o = jnp.concatenate(outs, axis=1)
    ms_o = _dot_exact_rhs(o * o, gmat) * (1.0 / DV_A)
    gate = gate_ref[0]
    o_ref[0] = (o * lax.rsqrt(ms_o + EPS) * gnorm_ref[...] * (gate * jax.nn.sigmoid(gate))).astype(o_ref.dtype)

    @pl.when(c == pl.num_programs(1) - 1)
    def _():
        for h in range(H_A):
            sfin_ref[0, h] = s_new[:, h * DV_A:(h + 1) * DV_A]


def _gdn_params(a_log, dt_bias, gnorm):
    alog_p = jnp.zeros((1, LANE), F32).at[0, SM_ALPHA:SM_ALPHA + H_A].set(a_log)
    dtb_p = jnp.zeros((1, LANE), F32).at[0, SM_ALPHA:SM_ALPHA + H_A].set(dt_bias)
    return alog_p, dtb_p, jnp.tile(gnorm, H_A).reshape(1, HD)


def gdn_prompt(proj, conv_w, a_log, dt_bias, gnorm, pad, c_sz=LANE):
    b, l, _ = proj.shape
    blk = lambda w, col: pl.BlockSpec((1, c_sz, w), lambda bi, ci: (bi, ci, col // w))
    const = lambda shp: pl.BlockSpec(shp, lambda bi, ci: (0, 0))
    return pl.pallas_call(
        functools.partial(_gdn_kernel, c_sz, pad),
        grid=(b, l // c_sz),
        in_specs=[blk(QKV_A, COL_QKV), blk(LANE, COL_SMALL), blk(HD, COL_GATE), const((CONV_W, QKV_A)),
                  const((1, LANE)), const((1, LANE)), const((1, HD))],
        out_specs=[pl.BlockSpec((1, c_sz, HD), lambda bi, ci: (bi, ci, 0)),
                   pl.BlockSpec((1, H_A, DK_A, DV_A), lambda bi, ci: (bi, 0, 0, 0))],
        out_shape=[jax.ShapeDtypeStruct((b, l, HD), BF16), jax.ShapeDtypeStruct((b, H_A, DK_A, DV_A), F32)],
        scratch_shapes=[pltpu.VMEM((DK_A, HD), F32), pltpu.VMEM((c_sz, QKV_A), F32)],
        compiler_params=pltpu.CompilerParams(dimension_semantics=("parallel", "arbitrary"),
                                             vmem_limit_bytes=VMEM_LIMIT),
        name="gdn_prompt",
    )(proj, proj, proj, conv_w, *_gdn_params(a_log, dt_bias, gnorm))


def bias_tiles(rel_bias):
    d = jnp.arange(-(LANE - 1), N_BIAS_TILES * LANE, dtype=jnp.int32)
    tab = rel_bias[rel_bucket(d)].astype(F32).T
    tiles = []
    for dl in range(N_BIAS_TILES):
        w = jnp.pad(tab[:, dl * LANE:dl * LANE + 2 * LANE - 1], ((0, 0), (0, 1)))
        sh = jnp.tile(w, (1, LANE))[:, :LANE * (2 * LANE - 1)].reshape(-1, LANE, 2 * LANE - 1)
        tiles.append(sh[:, :, LANE - 1:])
    return jnp.stack(tiles, axis=1)


def _select_topk(sc, valid, krow, topk, idx_bits):
    key = jnp.where(valid, _sortable(sc), INT_MIN)
    kf = jnp.float32(topk)

    def count(m):
        return _col_reduce(jnp.where(m, 1.0, 0.0), jnp.sum)

    t0 = jnp.where(count(key >= 0) >= kf, 0, INT_MIN).astype(jnp.int32)

    def tbody(n, t):
        cand = t | jnp.left_shift(jnp.int32(1), 30 - n)
        return jnp.where(count(key >= cand) >= kf, cand, t)

    t = lax.fori_loop(0, 31, tbody, t0)
    gt = key > t
    eq = key == t
    m = kf - count(gt)

    def jbody(n, j):
        cand = j | jnp.left_shift(jnp.int32(1), idx_bits - 1 - n)
        return jnp.where(count(eq & (krow < cand)) <= m, cand, j)

    j = lax.fori_loop(0, idx_bits, jbody, jnp.zeros_like(t))
    return valid & (gt | (eq & (krow < j)))


def _dsa_prompt_body(lc, pad, topk, q_ref, qi_ref, smq_ref, k_ref, v_ref, smk_ref, bias_ref, o_ref):
    i = pl.program_id(1)
    nkb = lc // LANE
    ki = smk_ref[0, :lc, SM_KI:SM_KI + D_IDX].astype(BF16)
    smt = smq_ref[0].T
    sc = jnp.zeros((lc, LANE), F32)
    for h in range(H_IDX):
        qih = qi_ref[0, :, h * D_IDX:(h + 1) * D_IDX].astype(BF16)
        d = lax.dot_general(ki, qih, NT_DIMS, preferred_element_type=F32)
        w = smt[SM_WI + h:SM_WI + h + 1, :] * (H_IDX ** -0.5)
        sc = sc + jnp.maximum(d * (D_IDX ** -0.5), 0.0) * w
    krow = lax.broadcasted_iota(jnp.int32, (lc, LANE), 0)
    qrow = i * LANE + lax.broadcasted_iota(jnp.int32, (lc, LANE), 1)
    valid = (krow <= qrow) & (krow >= pad)
    sel = _select_topk(sc, valid, krow, topk, max(1, (lc - 1).bit_length()))
    outs = []
    for h in range(H_B):
        sl = slice(h * DH_B, (h + 1) * DH_B)
        kh = k_ref[0, :lc, sl].astype(BF16)
        qh = q_ref[0, :, sl].astype(BF16)
        lg = lax.dot_general(kh, qh, NT_DIMS, preferred_element_type=F32) * (DH_B ** -0.5)
        bias = jnp.concatenate([bias_ref[h, jnp.clip(i - jb, 0, N_BIAS_TILES - 1)] for jb in range(nkb)], axis=0)
        lg = jnp.where(sel, lg + bias, -1e30)
        p = jnp.exp(lg - _col_reduce(lg, jnp.max))
        p = (p * (1.0 / _col_reduce(p, jnp.sum))).astype(BF16)
        vh = v_ref[0, :lc, sl].astype(BF16)
        outs.append(lax.dot_general(p, vh, TN_DIMS, preferred_element_type=F32))
    o_ref[0] = jnp.concatenate(outs, axis=1).astype(o_ref.dtype)


def _dsa_prompt_kernel(variants, pad, topk, *refs):
    _dispatch_variants(variants, _dsa_prompt_body, pad, topk, *refs)


def dsa_prompt(proj, btiles, pad, topk):
    b, l, _ = proj.shape
    nq = l // LANE
    qblk = lambda w, col: pl.BlockSpec((1, LANE, w), lambda bi, i: (bi, i, col // w))
    full = lambda w, col: pl.BlockSpec((1, l, w), lambda bi, i: (bi, 0, col // w))
    return pl.pallas_call(
        functools.partial(_dsa_prompt_kernel, _causal_variants(nq), pad, topk),
        grid=(b, nq),
        in_specs=[qblk(HD, COL_QB), qblk(HD, COL_QI), qblk(LANE, COL_SMALL),
                  full(HD, COL_KB), full(HD, COL_VB), full(LANE, COL_SMALL),
                  pl.BlockSpec(btiles.shape, lambda bi, i: (0, 0, 0, 0))],
        out_specs=pl.BlockSpec((1, LANE, HD), lambda bi, i: (bi, i, 0)),
        out_shape=jax.ShapeDtypeStruct((b, l, HD), BF16),
        compiler_params=pltpu.CompilerParams(dimension_semantics=("parallel", "arbitrary"),
                                             vmem_limit_bytes=VMEM_LIMIT),
        name="dsa_prompt",
    )(proj, proj, proj, proj, proj, proj, btiles)


def _mla_prep_kernel(h_ref, nm_ref, win_ref, qn_ref, kvn_ref, wuq_ref, wuk_ref, cos_ref, sin_ref, q_ref, kv_ref):
    x = _rms(h_ref[0], nm_ref[...]).astype(BF16)
    p = jnp.dot(x, win_ref[...], preferred_element_type=F32)
    cqn = _rms(p[:, CC_Q:CC_KV], qn_ref[...]).astype(BF16)
    qf = jnp.dot(cqn, wuq_ref[...], preferred_element_type=F32)
    cos, sin = cos_ref[...], sin_ref[...]
    q1, q2 = qf[:, H_C * LANE:(H_C + 1) * LANE], qf[:, (H_C + 1) * LANE:]
    q1r = q1 * cos - q2 * sin
    q2r = q2 * cos + q1 * sin
    head_of_lane = lax.broadcasted_iota(jnp.int32, (1, LANE), 1) // HALF
    for h in range(H_C):
        qlat = jnp.dot(qf[:, h * LANE:(h + 1) * LANE].astype(BF16), wuk_ref[h], preferred_element_type=F32)
        mine = head_of_lane == h
        q_ref[0, h] = jnp.concatenate([qlat, jnp.where(mine, q1r, 0.0), jnp.where(mine, q2r, 0.0)],
                                      axis=1).astype(q_ref.dtype)
    c = _rms(p[:, CC_KV:CC_K1], kvn_ref[...])
    k1, k2 = p[:, CC_K1:CC_K2], p[:, CC_K2:]
    kv_ref[0] = jnp.concatenate([c, k1 * cos - k2 * sin, k2 * cos + k1 * sin], axis=1)


def mla_prep(h, norm_w, w_in_p, qnorm, kvnorm, w_uq_p, w_uk_p, cos, sin):
    b, l, d = h.shape
    tm = min(LANE, l)
    const2 = lambda shp: pl.BlockSpec(shp, lambda bi, i: (0, 0))
    return pl.pallas_call(
        _mla_prep_kernel,
        grid=(b, l // tm),
        in_specs=[pl.BlockSpec((1, tm, d), lambda bi, i: (bi, i, 0)), const2((1, d)), const2(w_in_p.shape),
                  const2((1, Q_LORA)), const2((1, KV_LORA)), const2(w_uq_p.shape),
                  pl.BlockSpec(w_uk_p.shape, lambda bi, i: (0, 0, 0)),
                  pl.BlockSpec((tm, LANE), lambda bi, i: (i, 0)), pl.BlockSpec((tm, LANE), lambda bi, i: (i, 0))],
        out_specs=[pl.BlockSpec((1, H_C, tm, KVW), lambda bi, i: (bi, 0, i, 0)),
                   pl.BlockSpec((1, tm, KVW), lambda bi, i: (bi, i, 0))],
        out_shape=[jax.ShapeDtypeStruct((b, H_C, l, KVW), BF16), jax.ShapeDtypeStruct((b, l, KVW), F32)],
        compiler_params=pltpu.CompilerParams(dimension_semantics=("parallel", "arbitrary"),
                                             vmem_limit_bytes=VMEM_LIMIT),
        name="mla_prep",
    )(h, norm_w.reshape(1, d), w_in_p, qnorm.reshape(1, -1), kvnorm.reshape(1, -1), w_uq_p, w_uk_p, cos, sin)


def _mla_body(lc, pad, q_ref, kv_ref, o_ref):
    i = pl.program_id(1)
    q = q_ref[0].reshape(H_C * LANE, KVW)
    kb = kv_ref[0, :lc, :].astype(BF16)
    s = lax.dot_general(q, kb, NT_DIMS, preferred_element_type=F32) * MLA_SCALE
    s = s.reshape(H_C, LANE, lc)
    krow = lax.broadcasted_iota(jnp.int32, (LANE, lc), 1)
    qrow = i * LANE + lax.broadcasted_iota(jnp.int32, (LANE, lc), 0)
    ok = (krow <= qrow) & (krow >= pad)
    s = jnp.where(ok[None], s, -1e30)
    mx = jnp.max(s, axis=-1, keepdims=True)
    p = jnp.exp(s - mx)
    p = (p * (1.0 / jnp.sum(p, axis=-1, keepdims=True))).astype(BF16).reshape(H_C * LANE, lc)
    o = jnp.dot(p, kb[:, :KV_LORA], preferred_element_type=F32).astype(o_ref.dtype)
    for h in range(H_C):
        o_ref[0, :, h * KV_LORA:(h + 1) * KV_LORA] = o[h * LANE:(h + 1) * LANE]


def _mla_kernel(variants, pad, *refs):
    _dispatch_variants(variants, _mla_body, pad, *refs)


def mla_prompt(q, kv, pad):
    b, _, l, _ = q.shape
    nq = l // LANE
    return pl.pallas_call(
        functools.partial(_mla_kernel, _causal_variants(nq), pad),
        grid=(b, nq),
        in_specs=[pl.BlockSpec((1, H_C, LANE, KVW), lambda bi, i: (bi, 0, i, 0)),
                  pl.BlockSpec((1, l, KVW), lambda bi, i: (bi, 0, 0))],
        out_specs=pl.BlockSpec((1, LANE, H_C * KV_LORA), lambda bi, i: (bi, i, 0)),
        out_shape=jax.ShapeDtypeStruct((b, l, H_C * KV_LORA), BF16),
        compiler_params=pltpu.CompilerParams(dimension_semantics=("parallel", "arbitrary"),
                                             vmem_limit_bytes=VMEM_LIMIT),
        name="mla_prompt",
    )(q, kv)


def _gdn_decode_kernel(x_ref, conv_ref, small_ref, gate_ref, s_ref, convw_ref, alog_ref, dtb_ref, gnorm_ref,
                       o_ref, convo_ref, so_ref):
    nb = x_ref.shape[0]
    x = x_ref[...]
    cb = conv_ref[0]
    acc = x * convw_ref[CONV_W - 1:CONV_W, :]
    for j in range(CONV_W - 1):
        acc = acc + cb[:, j, :] * convw_ref[j:j + 1, :]
    for j in range(CONV_W - 2):
        convo_ref[:, j, :] = cb[:, j + 1, :]
    convo_ref[:, CONV_W - 2, :] = x
    y = acc * jax.nn.sigmoid(acc)
    q, k, v = y[:, :HD], y[:, HD:2 * HD], y[:, 2 * HD:]
    gmat = _head_group_matrix()
    q = q * lax.rsqrt(_dot_exact_rhs(q * q, gmat) + EPS) * (DK_A ** -0.5)
    k = k * lax.rsqrt(_dot_exact_rhs(k * k, gmat) + EPS)
    g, beta = _gdn_gates(small_ref[...], alog_ref, dtb_ref)
    eg = jnp.exp(g)
    rowid = lax.broadcasted_iota(jnp.int32, (nb, 1), 0)

    def body(bb, o_acc):
        mine = rowid == bb
        outs = []
        for h in range(H_A):
            sl = slice(h * DK_A, (h + 1) * DK_A)
            s = s_ref[0, bb, h]
            egh = eg[:, SM_ALPHA + h:SM_ALPHA + h + 1]
            eg_b = jnp.sum(jnp.where(mine, egh, 0.0), axis=0, keepdims=True)
            delta = (v[:, sl] - _hdot(k[:, sl], s) * egh) * beta[:, SM_BRAW + h:SM_BRAW + h + 1]
            s_new = s * eg_b + lax.dot_general(jnp.where(mine, k[:, sl], 0.0), jnp.where(mine, delta, 0.0), TN_DIMS,
                                               precision=HI, preferred_element_type=F32)
            so_ref[bb, h] = s_new
            outs.append(_hdot(q[:, sl], s_new))
        return jnp.where(mine, jnp.concatenate(outs, axis=1), o_acc)

    o = lax.fori_loop(0, nb, body, jnp.zeros((nb, HD), F32))
    ms_o = _dot_exact_rhs(o * o, gmat) * (1.0 / DV_A)
    gate = gate_ref[...]
    o_ref[...] = (o * lax.rsqrt(ms_o + EPS) * gnorm_ref[...] * (gate * jax.nn.sigmoid(gate))).astype(o_ref.dtype)


def gdn_decode_step(proj, conv_state, ssm_state, layer, conv_w, a_log, dt_bias, gnorm, nb=8):
    b = proj.shape[0]
    blk = lambda w, col: pl.BlockSpec((nb, w), lambda i: (i, col // w))
    const = lambda shp: pl.BlockSpec(shp, lambda i: (0, 0))
    return pl.pallas_call(
        _gdn_decode_kernel,
        grid=(b // nb,),
        in_specs=[blk(QKV_A, COL_QKV), pl.BlockSpec((1, nb, CONV_W - 1, QKV_A), lambda i: (layer, i, 0, 0)),
                  blk(LANE, COL_SMALL), blk(HD, COL_GATE),
                  pl.BlockSpec((1, nb, H_A, DK_A, DV_A), lambda i: (layer, i, 0, 0, 0)),
                  const((CONV_W, QKV_A)), const((1, LANE)), const((1, LANE)), const((1, HD))],
        out_specs=[pl.BlockSpec((nb, HD), lambda i: (i, 0)),
                   pl.BlockSpec((nb, CONV_W - 1, QKV_A), lambda i: (i, 0, 0)),
                   pl.BlockSpec((nb, H_A, DK_A, DV_A), lambda i: (i, 0, 0, 0))],
        out_shape=[jax.ShapeDtypeStruct((b, HD), BF16), jax.ShapeDtypeStruct(conv_state.shape[1:], F32),
                   jax.ShapeDtypeStruct(ssm_state.shape[1:], F32)],
        compiler_params=pltpu.CompilerParams(dimension_semantics=("parallel",), vmem_limit_bytes=VMEM_LIMIT),
        name="gdn_decode",
    )(proj, conv_state, proj, proj, ssm_state, conv_w, *_gdn_params(a_log, dt_bias, gnorm))


def _dsa_select_kernel(topk, n_pages, pt_ref, qi_ref, small_ref, *rest):
    ki_refs = rest[:DEC_PAGES]
    list_ref, sc_s, rank_s = rest[DEC_PAGES:]
    j = pl.program_id(1)
    qi8 = _rows_to_heads(qi_ref[0], D_IDX).astype(BF16)
    sm = small_ref[0]
    pick = (lax.broadcasted_iota(jnp.int32, (H_IDX, LANE), 1)
            == lax.broadcasted_iota(jnp.int32, (H_IDX, LANE), 0) + SM_WI)
    w8 = jnp.sum(jnp.where(pick, sm, 0.0), axis=1, keepdims=True) * (H_IDX ** -0.5)
    ki = jnp.concatenate([r[0, 0] for r in ki_refs], axis=0).astype(BF16)
    d = lax.dot_general(qi8, ki, NT_DIMS, preferred_element_type=F32)
    s = jnp.sum(jnp.maximum(d * (D_IDX ** -0.5), 0.0) * w8, axis=0, keepdims=True)
    for g in range(DEC_PAGES):
        sc_s[pl.ds(j * DEC_PAGES + g, 1), :] = s[:, g * PAGE_SIZE:(g + 1) * PAGE_SIZE]

    @pl.when(j == pl.num_programs(1) - 1)
    def _():
        past = n_pages * PAGE_SIZE
        ki_new = sm[:, SM_KI:SM_KI + D_IDX].astype(BF16).astype(F32)
        d_new = jnp.sum(qi8.astype(F32) * ki_new, axis=1, keepdims=True)
        s_new = jnp.sum(jnp.maximum(d_new * (D_IDX ** -0.5), 0.0) * w8, axis=0, keepdims=True)
        key = _sortable(sc_s[...])
        key_n = _sortable(s_new)
        idx = lax.broadcasted_iota(jnp.int32, key.shape, 0) * PAGE_SIZE + lax.broadcasted_iota(jnp.int32, key.shape, 1)
        kf = jnp.float32(topk)

        def count(m, mn):
            c = jnp.sum(jnp.sum(jnp.where(m, 1.0, 0.0), axis=0, keepdims=True), axis=1, keepdims=True)
            return c + jnp.where(mn, 1.0, 0.0)

        t0 = jnp.where(count(key >= 0, key_n >= 0) >= kf, 0, INT_MIN).astype(jnp.int32)

        def tbody(n, t):
            cand = t | jnp.left_shift(jnp.int32(1), 30 - n)
            return jnp.where(count(key >= cand, key_n >= cand) >= kf, cand, t)

        t = lax.fori_loop(0, 31, tbody, t0)
        m = kf - count(key > t, key_n > t)
        idx_bits = past.bit_length()

        def jbody(n, jj):
            cand = jj | jnp.left_shift(jnp.int32(1), idx_bits - 1 - n)
            return jnp.where(count((key == t) & (idx < cand), (key_n == t) & (past < cand)) <= m, cand, jj)

        jj = lax.fori_loop(0, idx_bits, jbody, jnp.zeros_like(t))
        sel = (key > t) | ((key == t) & (idx < jj))
        sel_n = (key_n > t) | ((key_n == t) & (past < jj))
        self_b = jnp.where(sel, 1.0, 0.0).astype(BF16)
        ri = lax.broadcasted_iota(jnp.int32, (PAGE_SIZE, PAGE_SIZE), 0)
        ci = lax.broadcasted_iota(jnp.int32, (PAGE_SIZE, PAGE_SIZE), 1)
        within = jnp.dot(self_b, jnp.where(ri < ci, 1.0, 0.0).astype(BF16), preferred_element_type=F32)
        cnt = jnp.dot(self_b, jnp.ones((PAGE_SIZE, PAGE_SIZE), BF16), preferred_element_type=F32)
        pi = lax.broadcasted_iota(jnp.int32, (n_pages, n_pages), 0)
        pj = lax.broadcasted_iota(jnp.int32, (n_pages, n_pages), 1)
        before = jnp.dot(jnp.where(pj < pi, 1.0, 0.0).astype(BF16), cnt.astype(BF16), preferred_element_type=F32)
        rank_s[...] = jnp.where(sel, before + within, -1.0)
        slot = lax.broadcasted_iota(jnp.int32, (topk, 1), 0).astype(F32)
        rmat = jnp.where(ci == 0, ri, jnp.where(ci == 1, 1, 0)).astype(BF16)

        def pbody(p, carry):
            acc_off, acc_page = carry
            onehot = jnp.where(rank_s[pl.ds(p, 1), :] == slot, 1.0, 0.0).astype(BF16)
            c = jnp.dot(onehot, rmat, preferred_element_type=F32)
            return acc_off + c, acc_page + c * lax.convert_element_type(p * PAGE_SIZE, F32)

        z = jnp.zeros((topk, LANE), F32)
        acc_off, acc_page = lax.fori_loop(0, n_pages, pbody, (z, z))
        lst = acc_off + pltpu.roll(acc_page, LANE - 1, 1)
        last = (slot == topk - 1) & sel_n
        list_ref[0] = jnp.where(last, jnp.float32(past), lst)


def dsa_decode_select(proj3, cache_ki, layer, page_table, topk):
    b = proj3.shape[0]
    n_pages = page_table.shape[1]
    page_spec = lambda g: pl.BlockSpec((1, 1, PAGE_SIZE, D_IDX),
                                       lambda bi, j, pt: (layer, pt[bi, j * DEC_PAGES + g], 0, 0))
    grid_spec = pltpu.PrefetchScalarGridSpec(
        num_scalar_prefetch=1,
        grid=(b, n_pages // DEC_PAGES),
        in_specs=[pl.BlockSpec((1, 1, HD), lambda bi, j, pt: (bi, 0, COL_QI // HD)),
                  pl.BlockSpec((1, 1, LANE), lambda bi, j, pt: (bi, 0, COL_SMALL // LANE))]
                 + [page_spec(g) for g in range(DEC_PAGES)],
        out_specs=pl.BlockSpec((1, topk, LANE), lambda bi, j, pt: (bi, 0, 0)),
        scratch_shapes=[pltpu.VMEM((n_pages, PAGE_SIZE), F32), pltpu.VMEM((n_pages, PAGE_SIZE), F32)],
    )
    return pl.pallas_call(
        functools.partial(_dsa_select_kernel, topk, n_pages),
        grid_spec=grid_spec,
        out_shape=jax.ShapeDtypeStruct((b, topk, LANE), F32),
        compiler_params=pltpu.CompilerParams(dimension_semantics=("parallel", "arbitrary"),
                                             vmem_limit_bytes=VMEM_LIMIT),
        name="dsa_decode_select",
    )(page_table, proj3, proj3, *([cache_ki] * DEC_PAGES))


def _row_copies(b, slot, layer, past, idx_ref, pt_ref, ck_ref, cv_ref, kbuf, vbuf, sem):
    def copies(r):
        i = jnp.minimum(idx_ref[b, r], past - 1)
        phys = pt_ref[b, i // PAGE_SIZE]
        off = i % PAGE_SIZE
        return (pltpu.make_async_copy(ck_ref.at[layer, phys, off], kbuf.at[slot, r], sem.at[slot, 0]),
                pltpu.make_async_copy(cv_ref.at[layer, phys, off], vbuf.at[slot, r], sem.at[slot, 1]))
    return copies


def _dsa_gather_kernel(topk, past, layer, idx_ref, pt_ref, thr_ref, q_ref, kn_ref, vn_ref, dist_ref, rb_ref,
                       ck_ref, cv_ref, o_ref, kbuf, vbuf, sem):
    b = pl.program_id(0)
    nb = pl.num_programs(0)
    slot = b % 2

    def start_all(bb, sl):
        cp = _row_copies(bb, sl, layer, past, idx_ref, pt_ref, ck_ref, cv_ref, kbuf, vbuf, sem)

        def body(r, _):
            ck, cv = cp(r)
            ck.start()
            cv.start()
            return 0
        lax.fori_loop(0, topk, body, 0)

    @pl.when(b == 0)
    def _():
        start_all(0, 0)

    @pl.when(b + 1 < nb)
    def _():
        start_all(b + 1, 1 - slot)

    cp = _row_copies(b, slot, layer, past, idx_ref, pt_ref, ck_ref, cv_ref, kbuf, vbuf, sem)

    def wbody(r, _):
        ck, cv = cp(r)
        ck.wait()
        cv.wait()
        return 0
    lax.fori_loop(0, topk, wbody, 0)

    dist = dist_ref[0]
    is_new = dist == 0
    q8 = _rows_to_heads(q_ref[0], DH_B).astype(BF16).astype(F32)
    kn8 = _rows_to_heads(kn_ref[0], DH_B)
    vn8 = _rows_to_heads(vn_ref[0], DH_B)
    ksel = jnp.where(is_new, kn8[None], kbuf[slot]).astype(BF16).astype(F32)
    vsel = jnp.where(is_new, vn8[None], vbuf[slot]).astype(BF16).astype(F32)
    lg = jnp.sum(ksel * q8[None], axis=-1, keepdims=True) * (DH_B ** -0.5)
    bias = jnp.broadcast_to(rb_ref[0][None], lg.shape)
    for bk in range(1, REL_BUCKETS):
        bias = jnp.where(dist >= thr_ref[bk], rb_ref[bk][None], bias)
    lg = lg + bias
    mx = jnp.max(lg, axis=0, keepdims=True)
    p = jnp.exp(lg - mx)
    p = (p * (1.0 / jnp.sum(p, axis=0, keepdims=True))).astype(BF16).astype(F32)
    o_ref[0] = jnp.sum(p * vsel, axis=0).astype(o_ref.dtype)


def dsa_decode_gather(proj3, idx, cache_k, cache_v, layer, page_table, rel_bias):
    b, topk = idx.shape
    n_pages = page_table.shape[1]
    past = n_pages * PAGE_SIZE
    dist = jnp.broadcast_to((past - idx)[:, :, None, None], (b, topk, H_B, 1)).astype(jnp.int32)
    buckets = rel_bucket(jnp.arange(past + 1, dtype=jnp.int32))
    thr = jnp.sum(buckets[None, :] < jnp.arange(REL_BUCKETS, dtype=jnp.int32)[:, None], axis=1).astype(jnp.int32)
    rb = rel_bias.astype(F32)[:, :, None]
    row = lambda col: pl.BlockSpec((1, 1, HD), lambda bi, ix, pt, th: (bi, 0, col // HD))
    grid_spec = pltpu.PrefetchScalarGridSpec(
        num_scalar_prefetch=3,
        grid=(b,),
        in_specs=[row(COL_QB), row(COL_KB), row(COL_VB),
                  pl.BlockSpec((1, topk, H_B, 1), lambda bi, ix, pt, th: (bi, 0, 0, 0)),
                  pl.BlockSpec(rb.shape, lambda bi, ix, pt, th: (0, 0, 0)),
                  pl.BlockSpec(memory_space=pl.ANY), pl.BlockSpec(memory_space=pl.ANY)],
        out_specs=pl.BlockSpec((1, H_B, DH_B), lambda bi, ix, pt, th: (bi, 0, 0)),
        scratch_shapes=[pltpu.VMEM((2, topk, H_B, DH_B), F32), pltpu.VMEM((2, topk, H_B, DH_B), F32),
                        pltpu.SemaphoreType.DMA((2, 2))],
    )
    return pl.pallas_call(
        functools.partial(_dsa_gather_kernel, topk, past, layer),
        grid_spec=grid_spec,
        out_shape=jax.ShapeDtypeStruct((b, H_B, DH_B), BF16),
        compiler_params=pltpu.CompilerParams(dimension_semantics=("arbitrary",), vmem_limit_bytes=VMEM_LIMIT),
        name="dsa_decode_gather",
    )(idx, page_table, thr, proj3, proj3, proj3, dist, rb, cache_k, cache_v)


def _online_update(lg, m_s, l_s):
    m_new = jnp.maximum(m_s[...], jnp.max(lg, axis=1, keepdims=True))
    alpha = jnp.exp(m_s[...] - m_new)
    p = jnp.exp(lg - m_new)
    l_s[...] = l_s[...] * alpha + jnp.sum(p, axis=1, keepdims=True)
    m_s[...] = m_new
    return p, alpha


def _mla_decode_kernel(pt_ref, q_ref, kvn_ref, exp_ref, *rest):
    c_refs, kpe_refs = rest[:DEC_PAGES], rest[DEC_PAGES:2 * DEC_PAGES]
    o_ref, m_s, l_s, acc_s = rest[2 * DEC_PAGES:]
    j = pl.program_id(1)

    @pl.when(j == 0)
    def _():
        m_s[...] = jnp.full_like(m_s, -1e30)
        l_s[...] = jnp.zeros_like(l_s)
        acc_s[...] = jnp.zeros_like(acc_s)

    q = q_ref[0]
    cbs, lgs = [], []
    for g in range(DEC_PAGES):
        cb = c_refs[g][0, 0].astype(BF16)
        kpe_e = jnp.dot(kpe_refs[g][0, 0].astype(BF16), exp_ref[...], preferred_element_type=F32).astype(BF16)
        kfull = jnp.concatenate([cb, kpe_e], axis=1)
        lgs.append(lax.dot_general(q, kfull, NT_DIMS, preferred_element_type=F32))
        cbs.append(cb)
    lg = jnp.concatenate(lgs, axis=1) * MLA_SCALE
    p, alpha = _online_update(lg, m_s, l_s)
    pb = p.astype(BF16)
    pv = [jnp.dot(pb[:, g * PAGE_SIZE:(g + 1) * PAGE_SIZE], cbs[g], preferred_element_type=F32)
          for g in range(DEC_PAGES)]
    while len(pv) > 1:
        pv = [pv[i] + pv[i + 1] for i in range(0, len(pv), 2)]
    acc_s[...] = acc_s[...] * alpha + pv[0]

    @pl.when(j == pl.num_programs(1) - 1)
    def _():
        kvn = kvn_ref[0].astype(BF16).astype(F32)
        lgn = jnp.sum(q.astype(F32) * kvn, axis=1, keepdims=True) * MLA_SCALE
        pn, alpha_n = _online_update(lgn, m_s, l_s)
        acc = acc_s[...] * alpha_n + pn.astype(BF16).astype(F32) * kvn[:, :KV_LORA]
        o = acc * (1.0 / l_s[...])
        o_ref[0] = jnp.concatenate([o[h:h + 1] for h in range(H_C)], axis=1).astype(o_ref.dtype)


def mla_decode(q, kvn, cache_c, cache_kpe, layer, page_table):
    b = q.shape[0]
    n_pages = page_table.shape[1]
    src = jnp.concatenate([jnp.tile(jnp.arange(HALF), H_C), HALF + jnp.tile(jnp.arange(HALF), H_C)])
    expand = (jnp.arange(QK_ROPE)[:, None] == src[None, :]).astype(BF16)
    cspec = lambda g: pl.BlockSpec((1, 1, PAGE_SIZE, KV_LORA),
                                   lambda bi, j, pt: (layer, pt[bi, j * DEC_PAGES + g], 0, 0))
    kspec = lambda g: pl.BlockSpec((1, 1, PAGE_SIZE, QK_ROPE),
                                   lambda bi, j, pt: (layer, pt[bi, j * DEC_PAGES + g], 0, 0))
    grid_spec = pltpu.PrefetchScalarGridSpec(
        num_scalar_prefetch=1,
        grid=(b, n_pages // DEC_PAGES),
        in_specs=[pl.BlockSpec((1, H_C, KVW), lambda bi, j, pt: (bi, 0, 0)),
                  pl.BlockSpec((1, 1, KVW), lambda bi, j, pt: (bi, 0, 0)),
                  pl.BlockSpec(expand.shape, lambda bi, j, pt: (0, 0))]
                 + [cspec(g) for g in range(DEC_PAGES)] + [kspec(g) for g in range(DEC_PAGES)],
        out_specs=pl.BlockSpec((1, 1, H_C * KV_LORA), lambda bi, j, pt: (bi, 0, 0)),
        scratch_shapes=[pltpu.VMEM((H_C, 1), F32), pltpu.VMEM((H_C, 1), F32), pltpu.VMEM((H_C, KV_LORA), F32)],
    )
    return pl.pallas_call(
        _mla_decode_kernel,
        grid_spec=grid_spec,
        out_shape=jax.ShapeDtypeStruct((b, 1, H_C * KV_LORA), BF16),
        compiler_params=pltpu.CompilerParams(dimension_semantics=("parallel", "arbitrary"),
                                             vmem_limit_bytes=VMEM_LIMIT),
        name="mla_decode",
    )(page_table, q, kvn, expand, *([cache_c] * DEC_PAGES), *([cache_kpe] * DEC_PAGES))


def _prep_w_ab(w):
    qkv_a, alpha, braw, gate, q_b, k_b, v_b, qi, wi, ki = split_cols(w, AB_SIZES)
    z = jnp.zeros((w.shape[0], SM_KI - SM_WI - H_IDX), w.dtype)
    return jnp.concatenate([qkv_a, gate, q_b, k_b, v_b, qi, alpha, braw, wi, z, ki], axis=1).astype(BF16)


def _prep_w_c(w_in, w_uq, w_uk):
    cq, ckv, kpe = split_cols(w_in, C_SIZES)
    w_in_p = jnp.concatenate([cq, ckv, jnp.tile(kpe[:, :HALF], (1, H_C)), jnp.tile(kpe[:, HALF:], (1, H_C))], axis=1)
    wq = w_uq.reshape(Q_LORA, H_C, QK_NOPE + QK_ROPE)
    nope = jnp.pad(wq[:, :, :QK_NOPE], ((0, 0), (0, 0), (0, LANE - QK_NOPE))).reshape(Q_LORA, H_C * LANE)
    r1 = wq[:, :, QK_NOPE:QK_NOPE + HALF].reshape(Q_LORA, H_C * HALF)
    r2 = wq[:, :, QK_NOPE + HALF:].reshape(Q_LORA, H_C * HALF)
    w_uq_p = jnp.concatenate([nope, r1, r2], axis=1)
    w_uk_p = jnp.pad(jnp.transpose(w_uk, (1, 2, 0)), ((0, 0), (0, LANE - QK_NOPE), (0, 0)))
    return w_in_p.astype(BF16), w_uq_p.astype(BF16), w_uk_p.astype(BF16)


def _rope_tables(pos):
    inv = ROPE_BASE ** (-jnp.arange(HALF, dtype=jnp.float32) / HALF)
    ang = pos.astype(jnp.float32)[:, None] * inv[None, :]
    return jnp.tile(jnp.cos(ang), (1, LANE // HALF)), jnp.tile(jnp.sin(ang), (1, LANE // HALF))


def _kpe_from_rows(kv):
    return jnp.concatenate([kv[..., KV_LORA:KV_LORA + HALF], kv[..., KV_LORA + LANE:KV_LORA + LANE + HALF]], axis=-1)


def kernel(x_prompt, x_sample, state_a_conv, state_a_ssm, cache_b_k, cache_b_v, cache_b_kidx,
           cache_c_latent, cache_c_kpe, page_table, meta_tokens, rel_bias, norm_mix, norm_ffn, norm_final,
           w_in_ab, conv_a, a_log, dt_bias_a, gnorm_a, w_out_ab, w_in_c, qnorm_c, kvnorm_c,
           w_uq_c, w_uk_c, w_uv_c, w_out_c, w_up, w_down):
    w_up_b = w_up.astype(BF16)
    w_down_b = w_down.astype(BF16)
    w_out_ab_b = w_out_ab.astype(BF16)
    w_out_c_b = w_out_c.astype(BF16)
    w_uv_b = jnp.transpose(w_uv_c, (0, 2, 1, 3)).astype(BF16)
    w_in_ab_p = [_prep_w_ab(w_in_ab[j]) for j in range(w_in_ab.shape[0])]
    w_c_p = [_prep_w_c(w_in_c[j], w_uq_c[j], w_uk_c[j]) for j in range(w_in_c.shape[0])]

    b, seq, d = x_prompt.shape
    lp = N_META + seq
    pad = (-lp) % LANE
    l = pad + lp
    n = b * l
    topk = min(TOPK_MAX, SEQ // 4)
    meta = jnp.broadcast_to(meta_tokens[None].astype(x_prompt.dtype), (b, N_META, d))
    h = jnp.concatenate([jnp.zeros((b, pad, d), x_prompt.dtype), meta, x_prompt], axis=1).reshape(n, d)
    cos_p, sin_p = _rope_tables(jnp.arange(l, dtype=jnp.int32) - pad)
    a_conv_p, a_ssm_p, b_k_p, b_v_p, b_kidx_p, c_lat_p, c_kpe_p = [], [], [], [], [], [], []
    for li in range(DEPTH):
        j = li // 2
        last = li == DEPTH - 1
        if li % 2 == 0:
            proj = rms_linear(h, norm_mix[li], w_in_ab_p[j]).reshape(b, l, AB_COLS_P)
            o_a, s_fin = gdn_prompt(proj, conv_a[j], a_log[j], dt_bias_a[j], gnorm_a[j], pad)
            o_b = dsa_prompt(proj, bias_tiles(rel_bias), pad, topk)
            projs = [(o_a.reshape(n, HD), w_out_ab_b[j, :HD]), (o_b.reshape(n, HD), w_out_ab_b[j, HD:])]
            head_w = None
            a_conv_p.append(proj[:, l - (CONV_W - 1):, COL_QKV:COL_QKV + QKV_A])
            a_ssm_p.append(s_fin)
            b_k_p.append(proj[:, pad:, COL_KB:COL_KB + HD].reshape(b, lp, H_B, DH_B))
            b_v_p.append(proj[:, pad:, COL_VB:COL_VB + HD].reshape(b, lp, H_B, DH_B))
            b_kidx_p.append(proj[:, pad:, COL_SMALL + SM_KI:COL_SMALL + SM_KI + D_IDX])
        else:
            q_all, kv = mla_prep(h.reshape(b, l, d), norm_mix[li], *w_c_p[j][:1], qnorm_c[j], kvnorm_c[j],
                                 *w_c_p[j][1:], cos_p, sin_p)
            o_lat = mla_prompt(q_all, kv, pad)
            projs = [(o_lat.reshape(n, H_C * KV_LORA), w_out_c_b[j])]
            head_w = w_uv_b[j]
            c_lat_p.append(kv[:, pad:, :KV_LORA])
            c_kpe_p.append(_kpe_from_rows(kv[:, pad:]))
        h = proj_mlp(h, projs, norm_ffn[li], w_up_b[li], w_down_b[li], norm_final if last else None, head_w)
    y_p = h.reshape(b, l, d)[:, pad + N_META:]

    bs = x_sample.shape[0]
    n_pages = page_table.shape[1]
    past = n_pages * PAGE_SIZE
    topk_s = min(TOPK_MAX, (past + DEC_SEQ) // 4)
    hs = x_sample.reshape(bs, d)
    cos_s, sin_s = _rope_tables(jnp.full((bs,), past, jnp.int32))
    a_conv_s, a_ssm_s, b_k_s, b_v_s, b_kidx_s, c_lat_s, c_kpe_s = [], [], [], [], [], [], []
    for li in range(DEPTH):
        j = li // 2
        last = li == DEPTH - 1
        if li % 2 == 0:
            proj = rms_linear(hs, norm_mix[li], w_in_ab_p[j])
            proj3 = proj.reshape(bs, 1, AB_COLS_P)
            o_a, conv_new, s_new = gdn_decode_step(proj, state_a_conv, state_a_ssm, j, conv_a[j], a_log[j],
                                                   dt_bias_a[j], gnorm_a[j])
            sel_list = dsa_decode_select(proj3, cache_b_kidx, j, page_table, topk_s)
            idx = sel_list[:, :, 0].astype(jnp.int32)
            o_b = dsa_decode_gather(proj3, idx, cache_b_k, cache_b_v, j, page_table, rel_bias)
            projs = [(o_a, w_out_ab_b[j, :HD]), (o_b.reshape(bs, HD), w_out_ab_b[j, HD:])]
            head_w = None
            a_conv_s.append(conv_new)
            a_ssm_s.append(s_new)
            b_k_s.append(proj[:, COL_KB:COL_KB + HD].reshape(bs, DEC_SEQ, H_B, DH_B))
            b_v_s.append(proj[:, COL_VB:COL_VB + HD].reshape(bs, DEC_SEQ, H_B, DH_B))
            b_kidx_s.append(proj[:, COL_SMALL + SM_KI:COL_SMALL + SM_KI + D_IDX].reshape(bs, DEC_SEQ, D_IDX))
        else:
            q_all, kv = mla_prep(hs.reshape(1, bs, d), norm_mix[li], *w_c_p[j][:1], qnorm_c[j], kvnorm_c[j],
                                 *w_c_p[j][1:], cos_s, sin_s)
            o_lat = mla_decode(jnp.transpose(q_all[0], (1, 0, 2)), kv.reshape(bs, 1, KVW), cache_c_latent,
                               cache_c_kpe, j, page_table)
            projs = [(o_lat.reshape(bs, H_C * KV_LORA), w_out_c_b[j])]
            head_w = w_uv_b[j]
            c_lat_s.append(kv[0, :, :KV_LORA].reshape(bs, DEC_SEQ, KV_LORA))
            c_kpe_s.append(_kpe_from_rows(kv[0]).reshape(bs, DEC_SEQ, QK_ROPE))
        hs = proj_mlp(hs, projs, norm_ffn[li], w_up_b[li], w_down_b[li], norm_final if last else None, head_w)
    y_s = hs.reshape(bs, DEC_SEQ, d)

    stack = jnp.stack
    return (y_p, y_s,
            stack(a_conv_p), stack(a_ssm_p), stack(b_k_p), stack(b_v_p), stack(b_kidx_p), stack(c_lat_p), stack(c_kpe_p),
            stack(a_conv_s), stack(a_ssm_s), stack(b_k_s), stack(b_v_s), stack(b_kidx_s), stack(c_lat_s), stack(c_kpe_s))
```

```python
import functools
import math
import jax
import jax.numpy as jnp
from jax import lax
from jax.experimental import pallas as pl
from jax.experimental.pallas import tpu as pltpu

D_MODEL = 1024
BATCH = 8
SEQ = 2048
DEPTH = 2
DEC_BATCH = 32
DEC_SEQ = 1
PAST_LEN = 16384
PAGE_SIZE = 128

N_META = 16
EPS = 1e-6
H_A = 8
DK_A = 64
DV_A = 64
CONV_W = 4
H_B = 8
DH_B = 64
H_IDX = 8
D_IDX = 64
TOPK_MAX = 256
REL_BUCKETS = 32
REL_MAX_DIST = 1024
H_C = 8
Q_LORA = 384
KV_LORA = 256
QK_NOPE = 64
QK_ROPE = 32
V_C = 128
ROPE_BASE = 10000.0
D_FF = 4 * D_MODEL

QKV_A = H_A * (2 * DK_A + DV_A)
AB_SIZES = (QKV_A, H_A, H_A, H_A * DV_A, H_B * DH_B, H_B * DH_B, H_B * DH_B, H_IDX * D_IDX, H_IDX, D_IDX)
C_SIZES = (Q_LORA, KV_LORA, QK_ROPE)
MLA_SCALE = (QK_NOPE + QK_ROPE) ** -0.5

BF16 = jnp.bfloat16
F32 = jnp.float32
LANE = 128
VMEM_LIMIT = 56 * 1024 * 1024
INT_MIN = -(2 ** 31)
NT_DIMS = (((1,), (1,)), ((), ()))
TN_DIMS = (((0,), (0,)), ((), ()))
HI = lax.Precision.HIGHEST
HD = H_A * DK_A
HALF = QK_ROPE // 2

COL_QKV, COL_GATE, COL_QB, COL_KB, COL_VB, COL_QI, COL_SMALL = 0, 1536, 2048, 2560, 3072, 3584, 4096
AB_COLS_P = COL_SMALL + LANE
SM_ALPHA, SM_BRAW, SM_WI, SM_KI = 0, 8, 16, 64
CC_Q, CC_KV, CC_K1, CC_K2, C_COLS_P = 0, 384, 640, 768, 896
KVW = KV_LORA + 2 * LANE
N_BIAS_TILES = 9
DEC_PAGES = 16


def split_cols(x, sizes):
    out, off = [], 0
    for s in sizes:
        out.append(x[..., off:off + s])
        off += s
    return out


def rel_bucket(dist):
    dist = jnp.maximum(dist, 0)
    max_exact = REL_BUCKETS // 2
    log_ratio = jnp.log(jnp.maximum(dist, 1).astype(jnp.float32) / max_exact) / math.log(REL_MAX_DIST / max_exact)
    large = max_exact + (log_ratio * (REL_BUCKETS - max_exact)).astype(jnp.int32)
    return jnp.where(dist < max_exact, dist, jnp.minimum(large, REL_BUCKETS - 1))


def _rms(x, w):
    return x * lax.rsqrt(jnp.mean(x * x, axis=-1, keepdims=True) + EPS) * w


def _hdot(a, b):
    return jnp.dot(a, b, precision=HI, preferred_element_type=F32)


def _dot_exact_rhs(a, b_bf16):
    a_hi = a.astype(BF16)
    a_lo = (a - a_hi.astype(F32)).astype(BF16)
    return jnp.dot(a_hi, b_bf16, preferred_element_type=F32) + jnp.dot(a_lo, b_bf16, preferred_element_type=F32)


def _split2(a):
    hi = a.astype(BF16)
    return hi, (a - hi.astype(F32)).astype(BF16)


def _dot3(ap, bp):
    (ah, al), (bh, bl) = ap, bp
    return (jnp.dot(ah, bh, preferred_element_type=F32) + jnp.dot(al, bh, preferred_element_type=F32)
            + jnp.dot(ah, bl, preferred_element_type=F32))


def _sortable(x):
    bits = pltpu.bitcast(x + 0.0, jnp.int32)
    return bits ^ ((bits >> 31) & 0x7FFFFFFF)


def _rows_to_heads(row, width):
    return jnp.concatenate([row[:, h * width:(h + 1) * width] for h in range(row.shape[1] // width)], axis=0)


def _col_reduce(x, op, groups=8):
    l, w = x.shape
    if l % (8 * groups) == 0:
        x = op(op(x.reshape(groups, l // (8 * groups), 8, w), axis=1), axis=0)
    return op(x, axis=0, keepdims=True)


def _causal_variants(nq):
    n_var = min(4, nq)
    his = [nq - (n_var - 1 - v) * (nq // n_var) for v in range(n_var)]
    return [(hi, hi * LANE) for hi in his]


def _dispatch_variants(variants, body, *args):
    i = pl.program_id(1)
    lo = 0
    for hi, lc in variants:
        pl.when((i >= lo) & (i < hi))(functools.partial(body, lc, *args))
        lo = hi


def _rms_linear_kernel(x_ref, nw_ref, w_ref, o_ref):
    xn = _rms(x_ref[...], nw_ref[...]).astype(BF16)
    o_ref[...] = jnp.dot(xn, w_ref[...], preferred_element_type=F32)


def rms_linear(x, norm_w, w, tm=256):
    n, d = x.shape
    m = w.shape[1]
    tm = min(tm, n)
    return pl.pallas_call(
        _rms_linear_kernel,
        grid=(pl.cdiv(n, tm),),
        in_specs=[pl.BlockSpec((tm, d), lambda i: (i, 0)), pl.BlockSpec((1, d), lambda i: (0, 0)),
                  pl.BlockSpec((d, m), lambda i: (0, 0))],
        out_specs=pl.BlockSpec((tm, m), lambda i: (i, 0)),
        out_shape=jax.ShapeDtypeStruct((n, m), F32),
        compiler_params=pltpu.CompilerParams(dimension_semantics=("parallel",), vmem_limit_bytes=VMEM_LIMIT),
        name="rms_linear",
    )(x, norm_w.reshape(1, d), w)


def _proj_mlp_kernel(n_proj, n_heads, final, *refs):
    refs = list(refs)
    h_ref = refs.pop(0)
    a_refs = [refs.pop(0) for _ in range(n_proj)]
    w_refs = [refs.pop(0) for _ in range(n_proj)]
    hw_ref = refs.pop(0) if n_heads else None
    nw_ref, wup_ref, wdn_ref = refs.pop(0), refs.pop(0), refs.pop(0)
    nf_ref = refs.pop(0) if final else None
    o_ref, h1_s, xn_s, acc_s = refs
    f = pl.program_id(1)

    @pl.when(f == 0)
    def _():
        h1 = h_ref[...]
        for a_ref, w_ref in zip(a_refs, w_refs):
            a = a_ref[...]
            if n_heads:
                kh = a.shape[1] // n_heads
                a = jnp.concatenate([jnp.dot(a[:, hh * kh:(hh + 1) * kh].astype(BF16), hw_ref[hh],
                                             preferred_element_type=F32) for hh in range(n_heads)], axis=1)
            h1 = h1 + jnp.dot(a.astype(BF16), w_ref[...], preferred_element_type=F32)
        h1_s[...] = h1
        xn_s[...] = _rms(h1, nw_ref[...]).astype(BF16)
        acc_s[...] = jnp.zeros_like(acc_s)

    u = jnp.dot(xn_s[...], wup_ref[...], preferred_element_type=F32)
    u = jnp.square(jnp.maximum(u, 0.0)).astype(BF16)
    acc_s[...] += jnp.dot(u, wdn_ref[...], preferred_element_type=F32)

    @pl.when(f == pl.num_programs(1) - 1)
    def _():
        h2 = h1_s[...] + acc_s[...]
        if final:
            h2 = _rms(h2, nf_ref[...])
        o_ref[...] = h2


def proj_mlp(h, projs, norm_w, w_up, w_down, norm_final=None, head_w=None, tm=512, tf=1024):
    n, d = h.shape
    ff = w_up.shape[1]
    tm = min(tm, n)
    final = norm_final is not None
    n_heads = 0 if head_w is None else head_w.shape[0]
    row = lambda i, f: (i, 0)
    const = lambda i, f: (0, 0)
    in_specs = [pl.BlockSpec((tm, d), row)]
    in_specs += [pl.BlockSpec((tm, a.shape[1]), row) for a, _ in projs]
    in_specs += [pl.BlockSpec(w.shape, const) for _, w in projs]
    args = [h] + [a for a, _ in projs] + [w for _, w in projs]
    if n_heads:
        in_specs.append(pl.BlockSpec(head_w.shape, lambda i, f: (0, 0, 0)))
        args.append(head_w)
    in_specs += [pl.BlockSpec((1, d), const), pl.BlockSpec((d, tf), lambda i, f: (0, f)),
                 pl.BlockSpec((tf, d), lambda i, f: (f, 0))]
    args += [norm_w.reshape(1, d), w_up, w_down]
    if final:
        in_specs.append(pl.BlockSpec((1, d), const))
        args.append(norm_final.reshape(1, d))
    return pl.pallas_call(
        functools.partial(_proj_mlp_kernel, len(projs), n_heads, final),
        grid=(pl.cdiv(n, tm), ff // tf),
        in_specs=in_specs,
        out_specs=pl.BlockSpec((tm, d), row),
        out_shape=jax.ShapeDtypeStruct((n, d), F32),
        scratch_shapes=[pltpu.VMEM((tm, d), F32), pltpu.VMEM((tm, d), BF16), pltpu.VMEM((tm, d), F32)],
        compiler_params=pltpu.CompilerParams(dimension_semantics=("parallel", "arbitrary"),
                                             vmem_limit_bytes=VMEM_LIMIT),
        name="proj_mlp",
    )(*args)


def _gdn_gates(sm, alog_ref, dtb_ref):
    zs = sm + dtb_ref[...]
    g = -jnp.exp(alog_ref[...]) * (jnp.maximum(zs, 0.0) + jnp.log(1.0 + jnp.exp(-jnp.abs(zs))))
    return g, jax.nn.sigmoid(sm)


def _head_group_matrix():
    gi = lax.broadcasted_iota(jnp.int32, (HD, HD), 0) // DK_A
    gj = lax.broadcasted_iota(jnp.int32, (HD, HD), 1) // DK_A
    return jnp.where(gi == gj, 1.0, 0.0).astype(BF16)


def _gdn_kernel(c_sz, pad, qkv_ref, small_ref, gate_ref, convw_ref, alog_ref, dtb_ref, gnorm_ref,
                o_ref, sfin_ref, s_s, prev_s):
    c = pl.program_id(1)

    @pl.when(c == 0)
    def _():
        s_s[...] = jnp.zeros_like(s_s)
        prev_s[...] = jnp.zeros_like(prev_s)

    row = lax.broadcasted_iota(jnp.int32, (c_sz, 1), 0)
    valid = (row + c * c_sz) >= pad
    x = jnp.where(valid, qkv_ref[0], 0.0)
    prevx = prev_s[...]
    acc = x * convw_ref[CONV_W - 1:CONV_W, :]
    for sft in range(1, CONV_W):
        shifted = jnp.where(row < sft, pltpu.roll(prevx, sft, 0), pltpu.roll(x, sft, 0))
        acc = acc + shifted * convw_ref[CONV_W - 1 - sft:CONV_W - sft, :]
    prev_s[...] = x
    y = acc * jax.nn.sigmoid(acc)
    q, k, v = y[:, :HD], y[:, HD:2 * HD], y[:, 2 * HD:]
    gmat = _head_group_matrix()
    q = q * lax.rsqrt(_dot_exact_rhs(q * q, gmat) + EPS) * (DK_A ** -0.5)
    k = k * lax.rsqrt(_dot_exact_rhs(k * k, gmat) + EPS)

    g, beta = _gdn_gates(small_ref[0], alog_ref, dtb_ref)
    g = jnp.where(valid, g, 0.0)
    beta = jnp.where(valid, beta, 0.0)
    ti = lax.broadcasted_iota(jnp.int32, (c_sz, c_sz), 0)
    tj = lax.broadcasted_iota(jnp.int32, (c_sz, c_sz), 1)
    incl = ti >= tj
    strict = ti > tj
    gc = _hdot(jnp.where(incl, 1.0, 0.0), g)
    gct = gc.T
    ei = lax.broadcasted_iota(jnp.int32, (LANE, HD), 0)
    ej = lax.broadcasted_iota(jnp.int32, (LANE, HD), 1) // DK_A
    gc_w = _hdot(gc, jnp.where(ei == ej + SM_ALPHA, 1.0, 0.0))
    beta_w = _dot_exact_rhs(beta, jnp.where(ei == ej + SM_BRAW, 1.0, 0.0).astype(BF16))
    gl_w = gc_w[c_sz - 1:c_sz, :]
    eg_w = jnp.exp(gc_w)
    q_s = q * eg_w
    k_b = k * (beta_w * eg_w)
    v_b = v * beta_w
    k_l = k * jnp.exp(gl_w - gc_w)
    s_old = s_s[...]
    s_old_b = s_old.astype(BF16)
    n_fac = max(1, (c_sz - 1).bit_length())
    sls = [slice(h * DK_A, (h + 1) * DK_A) for h in range(H_A)]
    k16, q16 = k.astype(BF16), q.astype(BF16)
    decays, ms, xss = [], [], []
    for h, sl in enumerate(sls):
        diff = gc[:, SM_ALPHA + h:SM_ALPHA + h + 1] - gct[SM_ALPHA + h:SM_ALPHA + h + 1, :]
        decay = jnp.exp(jnp.where(incl, diff, -1e30))
        kk = lax.dot_general(k16[:, sl], k16[:, sl], NT_DIMS, preferred_element_type=F32)
        ms.append(-jnp.where(strict, beta[:, SM_BRAW + h:SM_BRAW + h + 1] * kk * decay, 0.0))
        xss.append(jnp.concatenate([v_b[:, sl], k_b[:, sl]], axis=1))
        decays.append(decay)
    for s in range(n_fac):
        msp = [_split2(m) for m in ms]
        xss = [x + _dot3(mp, _split2(x)) for x, mp in zip(xss, msp)]
        if s < n_fac - 1:
            ms = [_dot3(mp, mp) for mp in msp]
    outs, s_upd = [], []
    for h, sl in enumerate(sls):
        xs = xss[h]
        sh = s_old_b[:, sl]
        v_new = xs[:, :DV_A] - jnp.dot(xs[:, DV_A:].astype(BF16), sh, preferred_element_type=F32)
        v_new_b = v_new.astype(BF16)
        qk = lax.dot_general(q16[:, sl], k16[:, sl], NT_DIMS, preferred_element_type=F32) * decays[h]
        outs.append(jnp.dot(q_s[:, sl].astype(BF16), sh, preferred_element_type=F32)
                    + jnp.dot(qk.astype(BF16), v_new_b, preferred_element_type=F32))
        s_upd.append(lax.dot_general(k_l[:, sl].astype(BF16), v_new_b, TN_DIMS, preferred_element_type=F32))
    s_new = s_old * jnp.exp(gl_w) + jnp.concatenate(s_upd, axis=1)
    s_s[...] = s_new
    o = jnp.concatenate(outs, axis=1)
    ms_o = _dot_exact_rhs(o * o, gmat) * (1.0 / DV_A)
    gate = gate_ref[0]
    o_ref[0] = (o * lax.rsqrt(ms_o + EPS) * gnorm_ref[...] * (gate * jax.nn.sigmoid(gate))).astype(o_ref.dtype)

    @pl.when(c == pl.num_programs(1) - 1)
    def _():
        for h in range(H_A):
            sfin_ref[0, h] = s_new[:, h * DV_A:(h + 1) * DV_A]


def _gdn_params(a_log, dt_bias, gnorm):
    alog_p = jnp.zeros((1, LANE), F32).at[0, SM_ALPHA:SM_ALPHA + H_A].set(a_log)
    dtb_p = jnp.zeros((1, LANE), F32).at[0, SM_ALPHA:SM_ALPHA + H_A].set(dt_bias)
    return alog_p, dtb_p, jnp.tile(gnorm, H_A).reshape(1, HD)


def gdn_prompt(proj, conv_w, a_log, dt_bias, gnorm, pad, c_sz=LANE):
    b, l, _ = proj.shape
    blk = lambda w, col: pl.BlockSpec((1, c_sz, w), lambda bi, ci: (bi, ci, col // w))
    const = lambda shp: pl.BlockSpec(shp, lambda bi, ci: (0, 0))
    return pl.pallas_call(
        functools.partial(_gdn_kernel, c_sz, pad),
        grid=(b, l // c_sz),
        in_specs=[blk(QKV_A, COL_QKV), blk(LANE, COL_SMALL), blk(HD, COL_GATE), const((CONV_W, QKV_A)),
                  const((1, LANE)), const((1, LANE)), const((1, HD))],
        out_specs=[pl.BlockSpec((1, c_sz, HD), lambda bi, ci: (bi, ci, 0)),
                   pl.BlockSpec((1, H_A, DK_A, DV_A), lambda bi, ci: (bi, 0, 0, 0))],
        out_shape=[jax.ShapeDtypeStruct((b, l, HD), BF16), jax.ShapeDtypeStruct((b, H_A, DK_A, DV_A), F32)],
        scratch_shapes=[pltpu.VMEM((DK_A, HD), F32), pltpu.VMEM((c_sz, QKV_A), F32)],
        compiler_params=pltpu.CompilerParams(dimension_semantics=("parallel", "arbitrary"),
                                             vmem_limit_bytes=VMEM_LIMIT),
        name="gdn_prompt",
    )(proj, proj, proj, conv_w, *_gdn_params(a_log, dt_bias, gnorm))


def bias_tiles(rel_bias):
    d = jnp.arange(-(LANE - 1), N_BIAS_TILES * LANE, dtype=jnp.int32)
    tab = rel_bias[rel_bucket(d)].astype(F32).T
    tiles = []
    for dl in range(N_BIAS_TILES):
        w = jnp.pad(tab[:, dl * LANE:dl * LANE + 2 * LANE - 1], ((0, 0), (0, 1)))
        sh = jnp.tile(w, (1, LANE))[:, :LANE * (2 * LANE - 1)].reshape(-1, LANE, 2 * LANE - 1)
        tiles.append(sh[:, :, LANE - 1:])
    return jnp.stack(tiles, axis=1)


def _select_topk(sc, valid, krow, topk, idx_bits):
    key = jnp.where(valid, _sortable(sc), INT_MIN)
    kf = jnp.float32(topk)

    def count(m):
        return _col_reduce(jnp.where(m, 1.0, 0.0), jnp.sum)

    t0 = jnp.where(count(key >= 0) >= kf, 0, INT_MIN).astype(jnp.int32)

    def tbody(n, t):
        cand = t | jnp.left_shift(jnp.int32(1), 30 - n)
        return jnp.where(count(key >= cand) >= kf, cand, t)

    t = lax.fori_loop(0, 31, tbody, t0)
    gt = key > t
    eq = key == t
    m = kf - count(gt)

    def jbody(n, j):
        cand = j | jnp.left_shift(jnp.int32(1), idx_bits - 1 - n)
        return jnp.where(count(eq & (krow < cand)) <= m, cand, j)

    j = lax.fori_loop(0, idx_bits, jbody, jnp.zeros_like(t))
    return valid & (gt | (eq & (krow < j)))


def _dsa_prompt_body(lc, pad, topk, q_ref, qi_ref, smq_ref, k_ref, v_ref, smk_ref, bias_ref, o_ref):
    i = pl.program_id(1)
    nkb = lc // LANE
    ki = smk_ref[0, :lc, SM_KI:SM_KI + D_IDX].astype(BF16)
    smt = smq_ref[0].T
    sc = jnp.zeros((lc, LANE), F32)
    for h in range(H_IDX):
        qih = qi_ref[0, :, h * D_IDX:(h + 1) * D_IDX].astype(BF16)
        d = lax.dot_general(ki, qih, NT_DIMS, preferred_element_type=F32)
        w = smt[SM_WI + h:SM_WI + h + 1, :] * (H_IDX ** -0.5)
        sc = sc + jnp.maximum(d * (D_IDX ** -0.5), 0.0) * w
    krow = lax.broadcasted_iota(jnp.int32, (lc, LANE), 0)
    qrow = i * LANE + lax.broadcasted_iota(jnp.int32, (lc, LANE), 1)
    valid = (krow <= qrow) & (krow >= pad)
    sel = _select_topk(sc, valid, krow, topk, max(1, (lc - 1).bit_length()))
    outs = []
    for h in range(H_B):
        sl = slice(h * DH_B, (h + 1) * DH_B)
        kh = k_ref[0, :lc, sl].astype(BF16)
        qh = q_ref[0, :, sl].astype(BF16)
        lg = lax.dot_general(kh, qh, NT_DIMS, preferred_element_type=F32) * (DH_B ** -0.5)
        bias = jnp.concatenate([bias_ref[h, jnp.clip(i - jb, 0, N_BIAS_TILES - 1)] for jb in range(nkb)], axis=0)
        lg = jnp.where(sel, lg + bias, -1e30)
        p = jnp.exp(lg - _col_reduce(lg, jnp.max))
        p = (p * (1.0 / _col_reduce(p, jnp.sum))).astype(BF16)
        vh = v_ref[0, :lc, sl].astype(BF16)
        outs.append(lax.dot_general(p, vh, TN_DIMS, preferred_element_type=F32))
    o_ref[0] = jnp.concatenate(outs, axis=1).astype(o_ref.dtype)


def _dsa_prompt_kernel(variants, pad, topk, *refs):
    _dispatch_variants(variants, _dsa_prompt_body, pad, topk, *refs)


def dsa_prompt(proj, btiles, pad, topk):
    b, l, _ = proj.shape
    nq = l // LANE
    qblk = lambda w, col: pl.BlockSpec((1, LANE, w), lambda bi, i: (bi, i, col // w))
    full = lambda w, col: pl.BlockSpec((1, l, w), lambda bi, i: (bi, 0, col // w))
    return pl.pallas_call(
        functools.partial(_dsa_prompt_kernel, _causal_variants(nq), pad, topk),
        grid=(b, nq),
        in_specs=[qblk(HD, COL_QB), qblk(HD, COL_QI), qblk(LANE, COL_SMALL),
                  full(HD, COL_KB), full(HD, COL_VB), full(LANE, COL_SMALL),
                  pl.BlockSpec(btiles.shape, lambda bi, i: (0, 0, 0, 0))],
        out_specs=pl.BlockSpec((1, LANE, HD), lambda bi, i: (bi, i, 0)),
        out_shape=jax.ShapeDtypeStruct((b, l, HD), BF16),
        compiler_params=pltpu.CompilerParams(dimension_semantics=("parallel", "arbitrary"),
                                             vmem_limit_bytes=VMEM_LIMIT),
        name="dsa_prompt",
    )(proj, proj, proj, proj, proj, proj, btiles)


def _mla_prep_kernel(h_ref, nm_ref, win_ref, qn_ref, kvn_ref, wuq_ref, wuk_ref, cos_ref, sin_ref, q_ref, kv_ref):
    x = _rms(h_ref[0], nm_ref[...]).astype(BF16)
    p = jnp.dot(x, win_ref[...], preferred_element_type=F32)
    cqn = _rms(p[:, CC_Q:CC_KV], qn_ref[...]).astype(BF16)
    qf = jnp.dot(cqn, wuq_ref[...], preferred_element_type=F32)
    cos, sin = cos_ref[...], sin_ref[...]
    q1, q2 = qf[:, H_C * LANE:(H_C + 1) * LANE], qf[:, (H_C + 1) * LANE:]
    q1r = q1 * cos - q2 * sin
    q2r = q2 * cos + q1 * sin
    head_of_lane = lax.broadcasted_iota(jnp.int32, (1, LANE), 1) // HALF
    for h in range(H_C):
        qlat = jnp.dot(qf[:, h * LANE:(h + 1) * LANE].astype(BF16), wuk_ref[h], preferred_element_type=F32)
        mine = head_of_lane == h
        q_ref[0, h] = jnp.concatenate([qlat, jnp.where(mine, q1r, 0.0), jnp.where(mine, q2r, 0.0)],
                                      axis=1).astype(q_ref.dtype)
    c = _rms(p[:, CC_KV:CC_K1], kvn_ref[...])
    k1, k2 = p[:, CC_K1:CC_K2], p[:, CC_K2:]
    kv_ref[0] = jnp.concatenate([c, k1 * cos - k2 * sin, k2 * cos + k1 * sin], axis=1)


def mla_prep(h, norm_w, w_in_p, qnorm, kvnorm, w_uq_p, w_uk_p, cos, sin):
    b, l, d = h.shape
    tm = min(LANE, l)
    const2 = lambda shp: pl.BlockSpec(shp, lambda bi, i: (0, 0))
    return pl.pallas_call(
        _mla_prep_kernel,
        grid=(b, l // tm),
        in_specs=[pl.BlockSpec((1, tm, d), lambda bi, i: (bi, i, 0)), const2((1, d)), const2(w_in_p.shape),
                  const2((1, Q_LORA)), const2((1, KV_LORA)), const2(w_uq_p.shape),
                  pl.BlockSpec(w_uk_p.shape, lambda bi, i: (0, 0, 0)),
                  pl.BlockSpec((tm, LANE), lambda bi, i: (i, 0)), pl.BlockSpec((tm, LANE), lambda bi, i: (i, 0))],
        out_specs=[pl.BlockSpec((1, H_C, tm, KVW), lambda bi, i: (bi, 0, i, 0)),
                   pl.BlockSpec((1, tm, KVW), lambda bi, i: (bi, i, 0))],
        out_shape=[jax.ShapeDtypeStruct((b, H_C, l, KVW), BF16), jax.ShapeDtypeStruct((b, l, KVW), F32)],
        compiler_params=pltpu.CompilerParams(dimension_semantics=("parallel", "arbitrary"),
                                             vmem_limit_bytes=VMEM_LIMIT),
        name="mla_prep",
    )(h, norm_w.reshape(1, d), w_in_p, qnorm.reshape(1, -1), kvnorm.reshape(1, -1), w_uq_p, w_uk_p, cos, sin)


def _mla_body(lc, pad, q_ref, kv_ref, o_ref):
    i = pl.program_id(1)
    q = q_ref[0].reshape(H_C * LANE, KVW)
    kb = kv_ref[0, :lc, :].astype(BF16)
    s = lax.dot_general(q, kb, NT_DIMS, preferred_element_type=F32) * MLA_SCALE
    s = s.reshape(H_C, LANE, lc)
    krow = lax.broadcasted_iota(jnp.int32, (LANE, lc), 1)
    qrow = i * LANE + lax.broadcasted_iota(jnp.int32, (LANE, lc), 0)
    ok = (krow <= qrow) & (krow >= pad)
    s = jnp.where(ok[None], s, -1e30)
    mx = jnp.max(s, axis=-1, keepdims=True)
    p = jnp.exp(s - mx)
    p = (p * (1.0 / jnp.sum(p, axis=-1, keepdims=True))).astype(BF16).reshape(H_C * LANE, lc)
    o = jnp.dot(p, kb[:, :KV_LORA], preferred_element_type=F32).astype(o_ref.dtype)
    for h in range(H_C):
        o_ref[0, :, h * KV_LORA:(h + 1) * KV_LORA] = o[h * LANE:(h + 1) * LANE]


def _mla_kernel(variants, pad, *refs):
    _dispatch_variants(variants, _mla_body, pad, *refs)


def mla_prompt(q, kv, pad):
    b, _, l, _ = q.shape
    nq = l // LANE
    return pl.pallas_call(
        functools.partial(_mla_kernel, _causal_variants(nq), pad),
        grid=(b, nq),
        in_specs=[pl.BlockSpec((1, H_C, LANE, KVW), lambda bi, i: (bi, 0, i, 0)),
                  pl.BlockSpec((1, l, KVW), lambda bi, i: (bi, 0, 0))],
        out_specs=pl.BlockSpec((1, LANE, H_C * KV_LORA), lambda bi, i: (bi, i, 0)),
        out_shape=jax.ShapeDtypeStruct((b, l, H_C * KV_LORA), BF16),
        compiler_params=pltpu.CompilerParams(dimension_semantics=("parallel", "arbitrary"),
                                             vmem_limit_bytes=VMEM_LIMIT),
        name="mla_prompt",
    )(q, kv)


def _gdn_decode_kernel(x_ref, conv_ref, small_ref, gate_ref, s_ref, convw_ref, alog_ref, dtb_ref, gnorm_ref,
                       o_ref, convo_ref, so_ref):
    nb = x_ref.shape[0]
    x = x_ref[...]
    cb = conv_ref[0]
    acc = x * convw_ref[CONV_W - 1:CONV_W, :]
    for j in range(CONV_W - 1):
        acc = acc + cb[:, j, :] * convw_ref[j:j + 1, :]
    for j in range(CONV_W - 2):
        convo_ref[:, j, :] = cb[:, j + 1, :]
    convo_ref[:, CONV_W - 2, :] = x
    y = acc * jax.nn.sigmoid(acc)
    q, k, v = y[:, :HD], y[:, HD:2 * HD], y[:, 2 * HD:]
    gmat = _head_group_matrix()
    q = q * lax.rsqrt(_dot_exact_rhs(q * q, gmat) + EPS) * (DK_A ** -0.5)
    k = k * lax.rsqrt(_dot_exact_rhs(k * k, gmat) + EPS)
    g, beta = _gdn_gates(small_ref[...], alog_ref, dtb_ref)
    eg = jnp.exp(g)
    rowid = lax.broadcasted_iota(jnp.int32, (nb, 1), 0)

    def body(bb, o_acc):
        mine = rowid == bb
        outs = []
        for h in range(H_A):
            sl = slice(h * DK_A, (h + 1) * DK_A)
            s = s_ref[0, bb, h]
            egh = eg[:, SM_ALPHA + h:SM_ALPHA + h + 1]
            eg_b = jnp.sum(jnp.where(mine, egh, 0.0), axis=0, keepdims=True)
            delta = (v[:, sl] - _hdot(k[:, sl], s) * egh) * beta[:, SM_BRAW + h:SM_BRAW + h + 1]
            s_new = s * eg_b + lax.dot_general(jnp.where(mine, k[:, sl], 0.0), jnp.where(mine, delta, 0.0), TN_DIMS,
                                               precision=HI, preferred_element_type=F32)
            so_ref[bb, h] = s_new
            outs.append(_hdot(q[:, sl], s_new))
        return jnp.where(mine, jnp.concatenate(outs, axis=1), o_acc)

    o = lax.fori_loop(0, nb, body, jnp.zeros((nb, HD), F32))
    ms_o = _dot_exact_rhs(o * o, gmat) * (1.0 / DV_A)
    gate = gate_ref[...]
    o_ref[...] = (o * lax.rsqrt(ms_o + EPS) * gnorm_ref[...] * (gate * jax.nn.sigmoid(gate))).astype(o_ref.dtype)


def gdn_decode_step(proj, conv_state, ssm_state, layer, conv_w, a_log, dt_bias, gnorm, nb=8):
    b = proj.shape[0]
    blk = lambda w, col: pl.BlockSpec((nb, w), lambda i: (i, col // w))
    const = lambda shp: pl.BlockSpec(shp, lambda i: (0, 0))
    return pl.pallas_call(
        _gdn_decode_kernel,
        grid=(b // nb,),
        in_specs=[blk(QKV_A, COL_QKV), pl.BlockSpec((1, nb, CONV_W - 1, QKV_A), lambda i: (layer, i, 0, 0)),
                  blk(LANE, COL_SMALL), blk(HD, COL_GATE),
                  pl.BlockSpec((1, nb, H_A, DK_A, DV_A), lambda i: (layer, i, 0, 0, 0)),
                  const((CONV_W, QKV_A)), const((1, LANE)), const((1, LANE)), const((1, HD))],
        out_specs=[pl.BlockSpec((nb, HD), lambda i: (i, 0)),
                   pl.BlockSpec((nb, CONV_W - 1, QKV_A), lambda i: (i, 0, 0)),
                   pl.BlockSpec((nb, H_A, DK_A, DV_A), lambda i: (i, 0, 0, 0))],
        out_shape=[jax.ShapeDtypeStruct((b, HD), BF16), jax.ShapeDtypeStruct(conv_state.shape[1:], F32),
                   jax.ShapeDtypeStruct(ssm_state.shape[1:], F32)],
        compiler_params=pltpu.CompilerParams(dimension_semantics=("parallel",), vmem_limit_bytes=VMEM_LIMIT),
        name="gdn_decode",
    )(proj, conv_state, proj, proj, ssm_state, conv_w, *_gdn_params(a_log, dt_bias, gnorm))


def _dsa_select_kernel(topk, n_pages, pt_ref, qi_ref, small_ref, *rest):
    ki_refs = rest[:DEC_PAGES]
    mask_ref, mnew_ref, sc_s = rest[DEC_PAGES:]
    j = pl.program_id(1)
    qi8 = _rows_to_heads(qi_ref[0], D_IDX).astype(BF16)
    sm = small_ref[0]
    pick = (lax.broadcasted_iota(jnp.int32, (H_IDX, LANE), 1)
            == lax.broadcasted_iota(jnp.int32, (H_IDX, LANE), 0) + SM_WI)
    w8 = jnp.sum(jnp.where(pick, sm, 0.0), axis=1, keepdims=True) * (H_IDX ** -0.5)
    kit = jnp.concatenate([r[0, 0] for r in ki_refs], axis=1).astype(BF16)
    d = jnp.dot(qi8, kit, preferred_element_type=F32)
    s = jnp.sum(jnp.maximum(d * (D_IDX ** -0.5), 0.0) * w8, axis=0, keepdims=True)
    for g in range(DEC_PAGES):
        sc_s[pl.ds(j * DEC_PAGES + g, 1), :] = s[:, g * PAGE_SIZE:(g + 1) * PAGE_SIZE]

    @pl.when(j == pl.num_programs(1) - 1)
    def _():
        past = n_pages * PAGE_SIZE
        ki_new = sm[:, SM_KI:SM_KI + D_IDX].astype(BF16).astype(F32)
        d_new = jnp.sum(qi8.astype(F32) * ki_new, axis=1, keepdims=True)
        s_new = jnp.sum(jnp.maximum(d_new * (D_IDX ** -0.5), 0.0) * w8, axis=0, keepdims=True)
        key = _sortable(sc_s[...])
        key_n = _sortable(s_new)
        idx = lax.broadcasted_iota(jnp.int32, key.shape, 0) * PAGE_SIZE + lax.broadcasted_iota(jnp.int32, key.shape, 1)
        kf = jnp.float32(topk)

        def count(m, mn):
            c = jnp.sum(jnp.sum(jnp.where(m, 1.0, 0.0), axis=0, keepdims=True), axis=1, keepdims=True)
            return c + jnp.where(mn, 1.0, 0.0)

        t0 = jnp.where(count(key >= 0, key_n >= 0) >= kf, 0, INT_MIN).astype(jnp.int32)

        def tbody(n, t):
            cand = t | jnp.left_shift(jnp.int32(1), 30 - n)
            return jnp.where(count(key >= cand, key_n >= cand) >= kf, cand, t)

        t = lax.fori_loop(0, 31, tbody, t0)
        m = kf - count(key > t, key_n > t)
        idx_bits = past.bit_length()

        def jbody(n, jj):
            cand = jj | jnp.left_shift(jnp.int32(1), idx_bits - 1 - n)
            return jnp.where(count((key == t) & (idx < cand), (key_n == t) & (past < cand)) <= m, cand, jj)

        jj = lax.fori_loop(0, idx_bits, jbody, jnp.zeros_like(t))
        mask_ref[0] = jnp.where((key > t) | ((key == t) & (idx < jj)), 1.0, 0.0)
        sel_n = (key_n > t) | ((key_n == t) & (past < jj))
        lane0 = lax.broadcasted_iota(jnp.int32, (1, LANE), 1) == 0
        mnew_ref[0] = jnp.where(lane0 & sel_n, 1.0, 0.0)


def dsa_decode_select(proj3, cache_kit, layer, page_table, topk):
    b = proj3.shape[0]
    n_pages = page_table.shape[1]
    page_spec = lambda g: pl.BlockSpec((1, 1, D_IDX, PAGE_SIZE),
                                       lambda bi, j, pt: (layer, pt[bi, j * DEC_PAGES + g], 0, 0))
    grid_spec = pltpu.PrefetchScalarGridSpec(
        num_scalar_prefetch=1,
        grid=(b, n_pages // DEC_PAGES),
        in_specs=[pl.BlockSpec((1, 1, HD), lambda bi, j, pt: (bi, 0, COL_QI // HD)),
                  pl.BlockSpec((1, 1, LANE), lambda bi, j, pt: (bi, 0, COL_SMALL // LANE))]
                 + [page_spec(g) for g in range(DEC_PAGES)],
        out_specs=[pl.BlockSpec((1, n_pages, PAGE_SIZE), lambda bi, j, pt: (bi, 0, 0)),
                   pl.BlockSpec((1, 1, LANE), lambda bi, j, pt: (bi, 0, 0))],
        scratch_shapes=[pltpu.VMEM((n_pages, PAGE_SIZE), F32)],
    )
    return pl.pallas_call(
        functools.partial(_dsa_select_kernel, topk, n_pages),
        grid_spec=grid_spec,
        out_shape=[jax.ShapeDtypeStruct((b, n_pages, PAGE_SIZE), F32), jax.ShapeDtypeStruct((b, 1, LANE), F32)],
        compiler_params=pltpu.CompilerParams(dimension_semantics=("parallel", "arbitrary"),
                                             vmem_limit_bytes=VMEM_LIMIT),
        name="dsa_decode_select",
    )(page_table, proj3, proj3, *([cache_kit] * DEC_PAGES))


def _dsa_attend_kernel(n_pages, pt_ref, thr_ref, q_ref, knt_ref, vnt_ref, mask_ref, mnew_ref, rb_ref, *rest):
    kt_refs, vt_refs = rest[:DEC_PAGES], rest[DEC_PAGES:2 * DEC_PAGES]
    o_ref, qbd_s, m_s, l_s, acc_s = rest[2 * DEC_PAGES:]
    j = pl.program_id(1)
    past = n_pages * PAGE_SIZE
    eye = lax.broadcasted_iota(jnp.int32, (H_B, LANE), 0) == lax.broadcasted_iota(jnp.int32, (H_B, LANE), 1)

    @pl.when(j == 0)
    def _():
        bd = lax.broadcasted_iota(jnp.int32, (H_B, HD), 1) // DH_B == lax.broadcasted_iota(jnp.int32, (H_B, HD), 0)
        qbd_s[...] = jnp.where(bd, jnp.broadcast_to(q_ref[0], (H_B, HD)), 0.0).astype(BF16)
        m_s[...] = jnp.full_like(m_s, -1e30)
        l_s[...] = jnp.zeros_like(l_s)
        acc_s[...] = jnp.zeros_like(acc_s)

    def attend(kts, vts, ok, first_key):
        n = len(kts)
        lg = jnp.concatenate([jnp.dot(qbd_s[...], kt.astype(BF16), preferred_element_type=F32) for kt in kts], axis=1)
        dist = past - first_key - lax.broadcasted_iota(jnp.int32, (1, n * PAGE_SIZE), 1)
        bias = jnp.broadcast_to(rb_ref[0], lg.shape)
        for bk in range(1, REL_BUCKETS):
            bias = jnp.where(dist >= thr_ref[bk], rb_ref[bk], bias)
        lg = jnp.where(ok, lg * (DH_B ** -0.5) + bias, -1e30)
        m_new = jnp.maximum(m_s[...], jnp.max(lg, axis=1, keepdims=True))
        alpha = jnp.exp(m_s[...] - m_new)
        p = jnp.where(ok, jnp.exp(lg - m_new), 0.0)
        l_s[...] = l_s[...] * alpha + jnp.sum(p, axis=1, keepdims=True)
        m_s[...] = m_new
        pb = p.astype(BF16)
        pv = [lax.dot_general(vts[g].astype(BF16), pb[:, g * PAGE_SIZE:(g + 1) * PAGE_SIZE], NT_DIMS,
                              preferred_element_type=F32) for g in range(n)]
        while len(pv) > 1:
            pv = [pv[i] + pv[i + 1] if i + 1 < len(pv) else pv[i] for i in range(0, len(pv), 2)]
        alpha_row = jnp.sum(jnp.where(eye, alpha, 0.0), axis=0, keepdims=True)
        acc_s[...] = acc_s[...] * alpha_row[:, :H_B] + pv[0]

    rows = mask_ref[0, pl.ds(j * DEC_PAGES, DEC_PAGES), :]
    ok = jnp.concatenate([rows[g:g + 1] for g in range(DEC_PAGES)], axis=1) > 0.0
    attend([r[0, 0].reshape(HD, PAGE_SIZE) for r in kt_refs], [r[0, 0].reshape(HD, PAGE_SIZE) for r in vt_refs],
           ok, j * (DEC_PAGES * PAGE_SIZE))

    @pl.when(j == pl.num_programs(1) - 1)
    def _():
        attend([knt_ref[0]], [vnt_ref[0]], mnew_ref[0] > 0.0, past)
        l_row = jnp.sum(jnp.where(eye, l_s[...], 0.0), axis=0, keepdims=True)
        o = acc_s[...] * (1.0 / l_row[:, :H_B])
        own = (lax.broadcasted_iota(jnp.int32, (HD, H_B), 0) // DH_B
               == lax.broadcasted_iota(jnp.int32, (HD, H_B), 1))
        o_ref[0] = jnp.sum(jnp.where(own, o, 0.0), axis=1, keepdims=True).astype(o_ref.dtype)


def _decode_thresholds(past):
    buckets = rel_bucket(jnp.arange(past + 1, dtype=jnp.int32))
    return jnp.sum(buckets[None, :] < jnp.arange(REL_BUCKETS, dtype=jnp.int32)[:, None], axis=1).astype(jnp.int32)


def dsa_decode_attend(proj, mask, mnew, cache_kt, cache_vt, layer, page_table, rel_bias):
    b = proj.shape[0]
    n_pages = page_table.shape[1]
    past = n_pages * PAGE_SIZE
    thr = _decode_thresholds(past)
    rb = rel_bias.astype(F32)[:, :, None]
    proj3 = proj.reshape(b, 1, AB_COLS_P)
    one_key_page = lambda col: jnp.pad(proj[:, col:col + HD, None], ((0, 0), (0, 0), (0, PAGE_SIZE - 1)))
    page_spec = lambda g: pl.BlockSpec((1, 1, H_B, DH_B, PAGE_SIZE),
                                       lambda bi, j, pt, th: (layer, pt[bi, j * DEC_PAGES + g], 0, 0, 0))
    per_b = lambda shp: pl.BlockSpec((1,) + shp, lambda bi, j, pt, th: (bi, 0, 0))
    grid_spec = pltpu.PrefetchScalarGridSpec(
        num_scalar_prefetch=2,
        grid=(b, n_pages // DEC_PAGES),
        in_specs=[pl.BlockSpec((1, 1, HD), lambda bi, j, pt, th: (bi, 0, COL_QB // HD)),
                  per_b((HD, PAGE_SIZE)), per_b((HD, PAGE_SIZE)), per_b((n_pages, PAGE_SIZE)), per_b((1, LANE)),
                  pl.BlockSpec(rb.shape, lambda bi, j, pt, th: (0, 0, 0))]
                 + [page_spec(g) for g in range(DEC_PAGES)] * 2,
        out_specs=per_b((HD, 1)),
        scratch_shapes=[pltpu.VMEM((H_B, HD), BF16), pltpu.VMEM((H_B, 1), F32), pltpu.VMEM((H_B, 1), F32),
                        pltpu.VMEM((HD, H_B), F32)],
    )
    return pl.pallas_call(
        functools.partial(_dsa_attend_kernel, n_pages),
        grid_spec=grid_spec,
        out_shape=jax.ShapeDtypeStruct((b, HD, 1), F32),
        compiler_params=pltpu.CompilerParams(dimension_semantics=("parallel", "arbitrary"),
                                             vmem_limit_bytes=VMEM_LIMIT),
        name="dsa_decode_attend",
    )(page_table, thr, proj3, one_key_page(COL_KB), one_key_page(COL_VB), mask, mnew, rb,
      *([cache_kt] * DEC_PAGES), *([cache_vt] * DEC_PAGES))


def _online_update(lg, m_s, l_s):
    m_new = jnp.maximum(m_s[...], jnp.max(lg, axis=1, keepdims=True))
    alpha = jnp.exp(m_s[...] - m_new)
    p = jnp.exp(lg - m_new)
    l_s[...] = l_s[...] * alpha + jnp.sum(p, axis=1, keepdims=True)
    m_s[...] = m_new
    return p, alpha


def _mla_decode_kernel(pt_ref, q_ref, kvn_ref, exp_ref, *rest):
    c_refs, kpe_refs = rest[:DEC_PAGES], rest[DEC_PAGES:2 * DEC_PAGES]
    o_ref, m_s, l_s, acc_s = rest[2 * DEC_PAGES:]
    j = pl.program_id(1)

    @pl.when(j == 0)
    def _():
        m_s[...] = jnp.full_like(m_s, -1e30)
        l_s[...] = jnp.zeros_like(l_s)
        acc_s[...] = jnp.zeros_like(acc_s)

    q = q_ref[0]
    q_lat, q_pe = q[:, :KV_LORA], q[:, KV_LORA:]
    cbs, lgs = [], []
    for g in range(DEC_PAGES):
        cb = c_refs[g][0, 0].astype(BF16)
        kpe_t = jnp.dot(exp_ref[...], kpe_refs[g][0, 0].astype(BF16),
                        preferred_element_type=F32).astype(BF16)
        lgs.append(lax.dot_general(q_lat, cb, NT_DIMS, preferred_element_type=F32)
                   + jnp.dot(q_pe, kpe_t, preferred_element_type=F32))
        cbs.append(cb)
    lg = jnp.concatenate(lgs, axis=1) * MLA_SCALE
    p, alpha = _online_update(lg, m_s, l_s)
    pb = p.astype(BF16)
    pv = [jnp.dot(pb[:, g * PAGE_SIZE:(g + 1) * PAGE_SIZE], cbs[g], preferred_element_type=F32)
          for g in range(DEC_PAGES)]
    while len(pv) > 1:
        pv = [pv[i] + pv[i + 1] for i in range(0, len(pv), 2)]
    acc_s[...] = acc_s[...] * alpha + pv[0]

    @pl.when(j == pl.num_programs(1) - 1)
    def _():
        kvn = kvn_ref[0].astype(BF16).astype(F32)
        lgn = jnp.sum(q.astype(F32) * kvn, axis=1, keepdims=True) * MLA_SCALE
        pn, alpha_n = _online_update(lgn, m_s, l_s)
        acc = acc_s[...] * alpha_n + pn.astype(BF16).astype(F32) * kvn[:, :KV_LORA]
        o = acc * (1.0 / l_s[...])
        o_ref[0] = jnp.concatenate([o[h:h + 1] for h in range(H_C)], axis=1).astype(o_ref.dtype)


def mla_decode(q, kvn, cache_c, cache_kpet, layer, page_table):
    b = q.shape[0]
    n_pages = page_table.shape[1]
    src = jnp.concatenate([jnp.tile(jnp.arange(HALF), H_C), HALF + jnp.tile(jnp.arange(HALF), H_C)])
    expand = (src[:, None] == jnp.arange(QK_ROPE)[None, :]).astype(BF16)
    cspec = lambda g: pl.BlockSpec((1, 1, PAGE_SIZE, KV_LORA),
                                   lambda bi, j, pt: (layer, pt[bi, j * DEC_PAGES + g], 0, 0))
    kspec = lambda g: pl.BlockSpec((1, 1, QK_ROPE, PAGE_SIZE),
                                   lambda bi, j, pt: (layer, pt[bi, j * DEC_PAGES + g], 0, 0))
    grid_spec = pltpu.PrefetchScalarGridSpec(
        num_scalar_prefetch=1,
        grid=(b, n_pages // DEC_PAGES),
        in_specs=[pl.BlockSpec((1, H_C, KVW), lambda bi, j, pt: (bi, 0, 0)),
                  pl.BlockSpec((1, 1, KVW), lambda bi, j, pt: (bi, 0, 0)),
                  pl.BlockSpec(expand.shape, lambda bi, j, pt: (0, 0))]
                 + [cspec(g) for g in range(DEC_PAGES)] + [kspec(g) for g in range(DEC_PAGES)],
        out_specs=pl.BlockSpec((1, 1, H_C * KV_LORA), lambda bi, j, pt: (bi, 0, 0)),
        scratch_shapes=[pltpu.VMEM((H_C, 1), F32), pltpu.VMEM((H_C, 1), F32), pltpu.VMEM((H_C, KV_LORA), F32)],
    )
    return pl.pallas_call(
        _mla_decode_kernel,
        grid_spec=grid_spec,
        out_shape=jax.ShapeDtypeStruct((b, 1, H_C * KV_LORA), BF16),
        compiler_params=pltpu.CompilerParams(dimension_semantics=("parallel", "arbitrary"),
                                             vmem_limit_bytes=VMEM_LIMIT),
        name="mla_decode",
    )(page_table, q, kvn, expand, *([cache_c] * DEC_PAGES), *([cache_kpet] * DEC_PAGES))


def _prep_w_ab(w):
    qkv_a, alpha, braw, gate, q_b, k_b, v_b, qi, wi, ki = split_cols(w, AB_SIZES)
    z = jnp.zeros((w.shape[0], SM_KI - SM_WI - H_IDX), w.dtype)
    return jnp.concatenate([qkv_a, gate, q_b, k_b, v_b, qi, alpha, braw, wi, z, ki], axis=1).astype(BF16)


def _prep_w_c(w_in, w_uq, w_uk):
    cq, ckv, kpe = split_cols(w_in, C_SIZES)
    w_in_p = jnp.concatenate([cq, ckv, jnp.tile(kpe[:, :HALF], (1, H_C)), jnp.tile(kpe[:, HALF:], (1, H_C))], axis=1)
    wq = w_uq.reshape(Q_LORA, H_C, QK_NOPE + QK_ROPE)
    nope = jnp.pad(wq[:, :, :QK_NOPE], ((0, 0), (0, 0), (0, LANE - QK_NOPE))).reshape(Q_LORA, H_C * LANE)
    r1 = wq[:, :, QK_NOPE:QK_NOPE + HALF].reshape(Q_LORA, H_C * HALF)
    r2 = wq[:, :, QK_NOPE + HALF:].reshape(Q_LORA, H_C * HALF)
    w_uq_p = jnp.concatenate([nope, r1, r2], axis=1)
    w_uk_p = jnp.pad(jnp.transpose(w_uk, (1, 2, 0)), ((0, 0), (0, LANE - QK_NOPE), (0, 0)))
    return w_in_p.astype(BF16), w_uq_p.astype(BF16), w_uk_p.astype(BF16)


def _rope_tables(pos):
    inv = ROPE_BASE ** (-jnp.arange(HALF, dtype=jnp.float32) / HALF)
    ang = pos.astype(jnp.float32)[:, None] * inv[None, :]
    return jnp.tile(jnp.cos(ang), (1, LANE // HALF)), jnp.tile(jnp.sin(ang), (1, LANE // HALF))


def _kpe_from_rows(kv):
    return jnp.concatenate([kv[..., KV_LORA:KV_LORA + HALF], kv[..., KV_LORA + LANE:KV_LORA + LANE + HALF]], axis=-1)


def kernel(x_prompt, x_sample, state_a_conv, state_a_ssm, cache_b_k, cache_b_v, cache_b_kidx,
           cache_c_latent, cache_c_kpe, page_table, meta_tokens, rel_bias, norm_mix, norm_ffn, norm_final,
           w_in_ab, conv_a, a_log, dt_bias_a, gnorm_a, w_out_ab, w_in_c, qnorm_c, kvnorm_c,
           w_uq_c, w_uk_c, w_uv_c, w_out_c, w_up, w_down):
    w_up_b = w_up.astype(BF16)
    w_down_b = w_down.astype(BF16)
    w_out_ab_b = w_out_ab.astype(BF16)
    w_out_c_b = w_out_c.astype(BF16)
    w_uv_b = jnp.transpose(w_uv_c, (0, 2, 1, 3)).astype(BF16)
    w_in_ab_p = [_prep_w_ab(w_in_ab[j]) for j in range(w_in_ab.shape[0])]
    w_c_p = [_prep_w_c(w_in_c[j], w_uq_c[j], w_uk_c[j]) for j in range(w_in_c.shape[0])]

    b, seq, d = x_prompt.shape
    lp = N_META + seq
    pad = (-lp) % LANE
    l = pad + lp
    n = b * l
    topk = min(TOPK_MAX, SEQ // 4)
    meta = jnp.broadcast_to(meta_tokens[None].astype(x_prompt.dtype), (b, N_META, d))
    h = jnp.concatenate([jnp.zeros((b, pad, d), x_prompt.dtype), meta, x_prompt], axis=1).reshape(n, d)
    cos_p, sin_p = _rope_tables(jnp.arange(l, dtype=jnp.int32) - pad)
    a_conv_p, a_ssm_p, b_k_p, b_v_p, b_kidx_p, c_lat_p, c_kpe_p = [], [], [], [], [], [], []
    for li in range(DEPTH):
        j = li // 2
        last = li == DEPTH - 1
        if li % 2 == 0:
            proj = rms_linear(h, norm_mix[li], w_in_ab_p[j]).reshape(b, l, AB_COLS_P)
            o_a, s_fin = gdn_prompt(proj, conv_a[j], a_log[j], dt_bias_a[j], gnorm_a[j], pad)
            o_b = dsa_prompt(proj, bias_tiles(rel_bias), pad, topk)
            projs = [(o_a.reshape(n, HD), w_out_ab_b[j, :HD]), (o_b.reshape(n, HD), w_out_ab_b[j, HD:])]
            head_w = None
            a_conv_p.append(proj[:, l - (CONV_W - 1):, COL_QKV:COL_QKV + QKV_A])
            a_ssm_p.append(s_fin)
            b_k_p.append(proj[:, pad:, COL_KB:COL_KB + HD].reshape(b, lp, H_B, DH_B))
            b_v_p.append(proj[:, pad:, COL_VB:COL_VB + HD].reshape(b, lp, H_B, DH_B))
            b_kidx_p.append(proj[:, pad:, COL_SMALL + SM_KI:COL_SMALL + SM_KI + D_IDX])
        else:
            q_all, kv = mla_prep(h.reshape(b, l, d), norm_mix[li], *w_c_p[j][:1], qnorm_c[j], kvnorm_c[j],
                                 *w_c_p[j][1:], cos_p, sin_p)
            o_lat = mla_prompt(q_all, kv, pad)
            projs = [(o_lat.reshape(n, H_C * KV_LORA), w_out_c_b[j])]
            head_w = w_uv_b[j]
            c_lat_p.append(kv[:, pad:, :KV_LORA])
            c_kpe_p.append(_kpe_from_rows(kv[:, pad:]))
        h = proj_mlp(h, projs, norm_ffn[li], w_up_b[li], w_down_b[li], norm_final if last else None, head_w)
    y_p = h.reshape(b, l, d)[:, pad + N_META:]

    bs = x_sample.shape[0]
    n_pages = page_table.shape[1]
    past = n_pages * PAGE_SIZE
    topk_s = min(TOPK_MAX, (past + DEC_SEQ) // 4)
    hs = x_sample.reshape(bs, d)
    cos_s, sin_s = _rope_tables(jnp.full((bs,), past, jnp.int32))
    cache_kit = jnp.transpose(cache_b_kidx, (0, 1, 3, 2))
    cache_kt = jnp.transpose(cache_b_k, (0, 1, 3, 4, 2))
    cache_vt = jnp.transpose(cache_b_v, (0, 1, 3, 4, 2))
    cache_kpet = jnp.transpose(cache_c_kpe, (0, 1, 3, 2))
    a_conv_s, a_ssm_s, b_k_s, b_v_s, b_kidx_s, c_lat_s, c_kpe_s = [], [], [], [], [], [], []
    for li in range(DEPTH):
        j = li // 2
        last = li == DEPTH - 1
        if li % 2 == 0:
            proj = rms_linear(hs, norm_mix[li], w_in_ab_p[j])
            proj3 = proj.reshape(bs, 1, AB_COLS_P)
            o_a, conv_new, s_new = gdn_decode_step(proj, state_a_conv, state_a_ssm, j, conv_a[j], a_log[j],
                                                   dt_bias_a[j], gnorm_a[j])
            mask, mnew = dsa_decode_select(proj3, cache_kit, j, page_table, topk_s)
            o_b = dsa_decode_attend(proj, mask, mnew, cache_kt, cache_vt, j, page_table, rel_bias)
            projs = [(o_a, w_out_ab_b[j, :HD]), (o_b.reshape(bs, HD), w_out_ab_b[j, HD:])]
            head_w = None
            a_conv_s.append(conv_new)
            a_ssm_s.append(s_new)
            b_k_s.append(proj[:, COL_KB:COL_KB + HD].reshape(bs, DEC_SEQ, H_B, DH_B))
            b_v_s.append(proj[:, COL_VB:COL_VB + HD].reshape(bs, DEC_SEQ, H_B, DH_B))
            b_kidx_s.append(proj[:, COL_SMALL + SM_KI:COL_SMALL + SM_KI + D_IDX].reshape(bs, DEC_SEQ, D_IDX))
        else:
            q_all, kv = mla_prep(hs.reshape(1, bs, d), norm_mix[li], *w_c_p[j][:1], qnorm_c[j], kvnorm_c[j],
                                 *w_c_p[j][1:], cos_s, sin_s)
            o_lat = mla_decode(jnp.transpose(q_all[0], (1, 0, 2)), kv.reshape(bs, 1, KVW), cache_c_latent,
                               cache_kpet, j, page_table)
            projs = [(o_lat.reshape(bs, H_C * KV_LORA), w_out_c_b[j])]
            head_w = w_uv_b[j]
            c_lat_s.append(kv[0, :, :KV_LORA].reshape(bs, DEC_SEQ, KV_LORA))
            c_kpe_s.append(_kpe_from_rows(kv[0]).reshape(bs, DEC_SEQ, QK_ROPE))
        hs = proj_mlp(hs, projs, norm_ffn[li], w_up_b[li], w_down_b[li], norm_final if last else None, head_w)
    y_s = hs.reshape(bs, DEC_SEQ, d)

    stack = jnp.stack
    return (y_p, y_s,
            stack(a_conv_p), stack(a_ssm_p), stack(b_k_p), stack(b_v_p), stack(b_kidx_p), stack(c_lat_p), stack(c_kpe_p),
            stack(a_conv_s), stack(a_ssm_s), stack(b_k_s), stack(b_v_s), stack(b_kidx_s), stack(c_lat_s), stack(c_kpe_s))
```

```python
import functools
import math
import jax
import jax.numpy as jnp
from jax import lax
from jax.experimental import pallas as pl
from jax.experimental.pallas import tpu as pltpu

D_MODEL = 1024
BATCH = 8
SEQ = 2048
DEPTH = 2
DEC_BATCH = 32
DEC_SEQ = 1
PAST_LEN = 16384
PAGE_SIZE = 128

N_META = 16
EPS = 1e-6
H_A = 8
DK_A = 64
DV_A = 64
CONV_W = 4
H_B = 8
DH_B = 64
H_IDX = 8
D_IDX = 64
TOPK_MAX = 256
REL_BUCKETS = 32
REL_MAX_DIST = 1024
H_C = 8
Q_LORA = 384
KV_LORA = 256
QK_NOPE = 64
QK_ROPE = 32
V_C = 128
ROPE_BASE = 10000.0
D_FF = 4 * D_MODEL

QKV_A = H_A * (2 * DK_A + DV_A)
AB_SIZES = (QKV_A, H_A, H_A, H_A * DV_A, H_B * DH_B, H_B * DH_B, H_B * DH_B, H_IDX * D_IDX, H_IDX, D_IDX)
C_SIZES = (Q_LORA, KV_LORA, QK_ROPE)
MLA_SCALE = (QK_NOPE + QK_ROPE) ** -0.5

BF16 = jnp.bfloat16
F32 = jnp.float32
LANE = 128
VMEM_LIMIT = 56 * 1024 * 1024
INT_MIN = -(2 ** 31)
NT_DIMS = (((1,), (1,)), ((), ()))
TN_DIMS = (((0,), (0,)), ((), ()))
HI = lax.Precision.HIGHEST
HD = H_A * DK_A
HALF = QK_ROPE // 2

COL_QKV, COL_GATE, COL_QB, COL_KB, COL_VB, COL_QI, COL_SMALL = 0, 1536, 2048, 2560, 3072, 3584, 4096
AB_COLS_P = COL_SMALL + LANE
SM_ALPHA, SM_BRAW, SM_WI, SM_KI = 0, 8, 16, 64
CC_Q, CC_KV, CC_K1, CC_K2, C_COLS_P = 0, 384, 640, 768, 896
KVW = KV_LORA + 2 * LANE
N_BIAS_TILES = 9
DEC_PAGES = 16
MLA_DEC_PAGES = 32


def split_cols(x, sizes):
    out, off = [], 0
    for s in sizes:
        out.append(x[..., off:off + s])
        off += s
    return out


def rel_bucket(dist):
    dist = jnp.maximum(dist, 0)
    max_exact = REL_BUCKETS // 2
    log_ratio = jnp.log(jnp.maximum(dist, 1).astype(jnp.float32) / max_exact) / math.log(REL_MAX_DIST / max_exact)
    large = max_exact + (log_ratio * (REL_BUCKETS - max_exact)).astype(jnp.int32)
    return jnp.where(dist < max_exact, dist, jnp.minimum(large, REL_BUCKETS - 1))


def _rms(x, w):
    return x * lax.rsqrt(jnp.mean(x * x, axis=-1, keepdims=True) + EPS) * w


def _hdot(a, b):
    return jnp.dot(a, b, precision=HI, preferred_element_type=F32)


def _dot_exact_rhs(a, b_bf16):
    a_hi = a.astype(BF16)
    a_lo = (a - a_hi.astype(F32)).astype(BF16)
    return jnp.dot(a_hi, b_bf16, preferred_element_type=F32) + jnp.dot(a_lo, b_bf16, preferred_element_type=F32)


def _split2(a):
    hi = a.astype(BF16)
    return hi, (a - hi.astype(F32)).astype(BF16)


def _dot3(ap, bp):
    (ah, al), (bh, bl) = ap, bp
    return (jnp.dot(ah, bh, preferred_element_type=F32) + jnp.dot(al, bh, preferred_element_type=F32)
            + jnp.dot(ah, bl, preferred_element_type=F32))


def _sortable(x):
    bits = pltpu.bitcast(x + 0.0, jnp.int32)
    return bits ^ ((bits >> 31) & 0x7FFFFFFF)


def _rows_to_heads(row, width):
    return jnp.concatenate([row[:, h * width:(h + 1) * width] for h in range(row.shape[1] // width)], axis=0)


def _col_reduce(x, op, groups=8):
    l, w = x.shape
    if l % (8 * groups) == 0:
        x = op(op(x.reshape(groups, l // (8 * groups), 8, w), axis=1), axis=0)
    return op(x, axis=0, keepdims=True)


def _causal_variants(nq):
    n_var = min(4, nq)
    his = [nq - (n_var - 1 - v) * (nq // n_var) for v in range(n_var)]
    return [(hi, hi * LANE) for hi in his]


def _dispatch_variants(variants, body, *args):
    i = pl.program_id(1)
    lo = 0
    for hi, lc in variants:
        pl.when((i >= lo) & (i < hi))(functools.partial(body, lc, *args))
        lo = hi


def _rms_linear_kernel(x_ref, nw_ref, w_ref, o_ref):
    xn = _rms(x_ref[...], nw_ref[...]).astype(BF16)
    o_ref[...] = jnp.dot(xn, w_ref[...], preferred_element_type=F32)


def rms_linear(x, norm_w, w, tm=256):
    n, d = x.shape
    m = w.shape[1]
    tm = min(tm, n)
    return pl.pallas_call(
        _rms_linear_kernel,
        grid=(pl.cdiv(n, tm),),
        in_specs=[pl.BlockSpec((tm, d), lambda i: (i, 0)), pl.BlockSpec((1, d), lambda i: (0, 0)),
                  pl.BlockSpec((d, m), lambda i: (0, 0))],
        out_specs=pl.BlockSpec((tm, m), lambda i: (i, 0)),
        out_shape=jax.ShapeDtypeStruct((n, m), F32),
        compiler_params=pltpu.CompilerParams(dimension_semantics=("parallel",), vmem_limit_bytes=VMEM_LIMIT),
        name="rms_linear",
    )(x, norm_w.reshape(1, d), w)


def _proj_mlp_kernel(n_proj, n_heads, final, *refs):
    refs = list(refs)
    h_ref = refs.pop(0)
    a_refs = [refs.pop(0) for _ in range(n_proj)]
    w_refs = [refs.pop(0) for _ in range(n_proj)]
    hw_ref = refs.pop(0) if n_heads else None
    nw_ref, wup_ref, wdn_ref = refs.pop(0), refs.pop(0), refs.pop(0)
    nf_ref = refs.pop(0) if final else None
    o_ref, h1_s, xn_s, acc_s = refs
    f = pl.program_id(1)

    @pl.when(f == 0)
    def _():
        h1 = h_ref[...]
        for a_ref, w_ref in zip(a_refs, w_refs):
            a = a_ref[...]
            if n_heads:
                kh = a.shape[1] // n_heads
                a = jnp.concatenate([jnp.dot(a[:, hh * kh:(hh + 1) * kh].astype(BF16), hw_ref[hh],
                                             preferred_element_type=F32) for hh in range(n_heads)], axis=1)
            h1 = h1 + jnp.dot(a.astype(BF16), w_ref[...], preferred_element_type=F32)
        h1_s[...] = h1
        xn_s[...] = _rms(h1, nw_ref[...]).astype(BF16)
        acc_s[...] = jnp.zeros_like(acc_s)

    u = jnp.dot(xn_s[...], wup_ref[...], preferred_element_type=F32)
    u = jnp.square(jnp.maximum(u, 0.0)).astype(BF16)
    acc_s[...] += jnp.dot(u, wdn_ref[...], preferred_element_type=F32)

    @pl.when(f == pl.num_programs(1) - 1)
    def _():
        h2 = h1_s[...] + acc_s[...]
        if final:
            h2 = _rms(h2, nf_ref[...])
        o_ref[...] = h2


def proj_mlp(h, projs, norm_w, w_up, w_down, norm_final=None, head_w=None, tm=1024, tf=1024):
    n, d = h.shape
    ff = w_up.shape[1]
    tm = min(tm, n)
    final = norm_final is not None
    n_heads = 0 if head_w is None else head_w.shape[0]
    row = lambda i, f: (i, 0)
    const = lambda i, f: (0, 0)
    in_specs = [pl.BlockSpec((tm, d), row)]
    in_specs += [pl.BlockSpec((tm, a.shape[1]), row) for a, _ in projs]
    in_specs += [pl.BlockSpec(w.shape, const) for _, w in projs]
    args = [h] + [a for a, _ in projs] + [w for _, w in projs]
    if n_heads:
        in_specs.append(pl.BlockSpec(head_w.shape, lambda i, f: (0, 0, 0)))
        args.append(head_w)
    in_specs += [pl.BlockSpec((1, d), const), pl.BlockSpec((d, tf), lambda i, f: (0, f)),
                 pl.BlockSpec((tf, d), lambda i, f: (f, 0))]
    args += [norm_w.reshape(1, d), w_up, w_down]
    if final:
        in_specs.append(pl.BlockSpec((1, d), const))
        args.append(norm_final.reshape(1, d))
    return pl.pallas_call(
        functools.partial(_proj_mlp_kernel, len(projs), n_heads, final),
        grid=(pl.cdiv(n, tm), ff // tf),
        in_specs=in_specs,
        out_specs=pl.BlockSpec((tm, d), row),
        out_shape=jax.ShapeDtypeStruct((n, d), F32),
        scratch_shapes=[pltpu.VMEM((tm, d), F32), pltpu.VMEM((tm, d), BF16), pltpu.VMEM((tm, d), F32)],
        compiler_params=pltpu.CompilerParams(dimension_semantics=("parallel", "arbitrary"),
                                             vmem_limit_bytes=VMEM_LIMIT),
        name="proj_mlp",
    )(*args)


def _gdn_gates(sm, alog_ref, dtb_ref):
    zs = sm + dtb_ref[...]
    g = -jnp.exp(alog_ref[...]) * (jnp.maximum(zs, 0.0) + jnp.log(1.0 + jnp.exp(-jnp.abs(zs))))
    return g, jax.nn.sigmoid(sm)


def _head_group_matrix():
    gi = lax.broadcasted_iota(jnp.int32, (HD, HD), 0) // DK_A
    gj = lax.broadcasted_iota(jnp.int32, (HD, HD), 1) // DK_A
    return jnp.where(gi == gj, 1.0, 0.0).astype(BF16)


def _gdn_kernel(c_sz, pad, qkv_ref, small_ref, gate_ref, convw_ref, alog_ref, dtb_ref, gnorm_ref,
                o_ref, sfin_ref, s_s, prev_s):
    c = pl.program_id(1)

    @pl.when(c == 0)
    def _():
        s_s[...] = jnp.zeros_like(s_s)
        prev_s[...] = jnp.zeros_like(prev_s)

    row = lax.broadcasted_iota(jnp.int32, (c_sz, 1), 0)
    valid = (row + c * c_sz) >= pad
    x = jnp.where(valid, qkv_ref[0], 0.0)
    prevx = prev_s[...]
    acc = x * convw_ref[CONV_W - 1:CONV_W, :]
    for sft in range(1, CONV_W):
        shifted = jnp.where(row < sft, pltpu.roll(prevx, sft, 0), pltpu.roll(x, sft, 0))
        acc = acc + shifted * convw_ref[CONV_W - 1 - sft:CONV_W - sft, :]
    prev_s[...] = x
    y = acc * jax.nn.sigmoid(acc)
    q, k, v = y[:, :HD], y[:, HD:2 * HD], y[:, 2 * HD:]
    gmat = _head_group_matrix()
    q = q * lax.rsqrt(_dot_exact_rhs(q * q, gmat) + EPS) * (DK_A ** -0.5)
    k = k * lax.rsqrt(_dot_exact_rhs(k * k, gmat) + EPS)

    g, beta = _gdn_gates(small_ref[0], alog_ref, dtb_ref)
    g = jnp.where(valid, g, 0.0)
    beta = jnp.where(valid, beta, 0.0)
    ti = lax.broadcasted_iota(jnp.int32, (c_sz, c_sz), 0)
    tj = lax.broadcasted_iota(jnp.int32, (c_sz, c_sz), 1)
    incl = ti >= tj
    strict = ti > tj
    gc = _hdot(jnp.where(incl, 1.0, 0.0), g)
    gct = gc.T
    ei = lax.broadcasted_iota(jnp.int32, (LANE, HD), 0)
    ej = lax.broadcasted_iota(jnp.int32, (LANE, HD), 1) // DK_A
    gc_w = _hdot(gc, jnp.where(ei == ej + SM_ALPHA, 1.0, 0.0))
    beta_w = _dot_exact_rhs(beta, jnp.where(ei == ej + SM_BRAW, 1.0, 0.0).astype(BF16))
    gl_w = gc_w[c_sz - 1:c_sz, :]
    eg_w = jnp.exp(gc_w)
    q_s = q * eg_w
    k_b = k * (beta_w * eg_w)
    v_b = v * beta_w
    k_l = k * jnp.exp(gl_w - gc_w)
    s_old = s_s[...]
    s_old_b = s_old.astype(BF16)
    n_fac = max(1, (c_sz - 1).bit_length())
    sls = [slice(h * DK_A, (h + 1) * DK_A) for h in range(H_A)]
    k16, q16 = k.astype(BF16), q.astype(BF16)
    decays, ms, xss = [], [], []
    for h, sl in enumerate(sls):
        diff = gc[:, SM_ALPHA + h:SM_ALPHA + h + 1] - gct[SM_ALPHA + h:SM_ALPHA + h + 1, :]
        decay = jnp.exp(jnp.where(incl, diff, -1e30))
        kk = lax.dot_general(k16[:, sl], k16[:, sl], NT_DIMS, preferred_element_type=F32)
        ms.append(-jnp.where(strict, beta[:, SM_BRAW + h:SM_BRAW + h + 1] * kk * decay, 0.0))
        xss.append(jnp.concatenate([v_b[:, sl], k_b[:, sl]], axis=1))
        decays.append(decay)
    for s in range(n_fac):
        msp = [_split2(m) for m in ms]
        xss = [x + _dot3(mp, _split2(x)) for x, mp in zip(xss, msp)]
        if s < n_fac - 1:
            ms = [_dot3(mp, mp) for mp in msp]
    outs, s_upd = [], []
    for h, sl in enumerate(sls):
        xs = xss[h]
        sh = s_old_b[:, sl]
        v_new = xs[:, :DV_A] - jnp.dot(xs[:, DV_A:].astype(BF16), sh, preferred_element_type=F32)
        v_new_b = v_new.astype(BF16)
        qk = lax.dot_general(q16[:, sl], k16[:, sl], NT_DIMS, preferred_element_type=F32) * decays[h]
        outs.append(jnp.dot(q_s[:, sl].astype(BF16), sh, preferred_element_type=F32)
                    + jnp.dot(qk.astype(BF16), v_new_b, preferred_element_type=F32))
        s_upd.append(lax.dot_general(k_l[:, sl].astype(BF16), v_new_b, TN_DIMS, preferred_element_type=F32))
    s_new = s_old * jnp.exp(gl_w) + jnp.concatenate(s_upd, axis=1)
    s_s[...] = s_new
    o = jnp.concatenate(outs, axis=1)
    ms_o = _dot_exact_rhs(o * o, gmat) * (1.0 / DV_A)
    gate = gate_ref[0]
    o_ref[0] = (o * lax.rsqrt(ms_o + EPS) * gnorm_ref[...] * (gate * jax.nn.sigmoid(gate))).astype(o_ref.dtype)

    @pl.when(c == pl.num_programs(1) - 1)
    def _():
        for h in range(H_A):
            sfin_ref[0, h] = s_new[:, h * DV_A:(h + 1) * DV_A]


def _gdn_params(a_log, dt_bias, gnorm):
    alog_p = jnp.zeros((1, LANE), F32).at[0, SM_ALPHA:SM_ALPHA + H_A].set(a_log)
    dtb_p = jnp.zeros((1, LANE), F32).at[0, SM_ALPHA:SM_ALPHA + H_A].set(dt_bias)
    return alog_p, dtb_p, jnp.tile(gnorm, H_A).reshape(1, HD)


def gdn_prompt(proj, conv_w, a_log, dt_bias, gnorm, pad, c_sz=LANE):
    b, l, _ = proj.shape
    blk = lambda w, col: pl.BlockSpec((1, c_sz, w), lambda bi, ci: (bi, ci, col // w))
    const = lambda shp: pl.BlockSpec(shp, lambda bi, ci: (0, 0))
    return pl.pallas_call(
        functools.partial(_gdn_kernel, c_sz, pad),
        grid=(b, l // c_sz),
        in_specs=[blk(QKV_A, COL_QKV), blk(LANE, COL_SMALL), blk(HD, COL_GATE), const((CONV_W, QKV_A)),
                  const((1, LANE)), const((1, LANE)), const((1, HD))],
        out_specs=[pl.BlockSpec((1, c_sz, HD), lambda bi, ci: (bi, ci, 0)),
                   pl.BlockSpec((1, H_A, DK_A, DV_A), lambda bi, ci: (bi, 0, 0, 0))],
        out_shape=[jax.ShapeDtypeStruct((b, l, HD), BF16), jax.ShapeDtypeStruct((b, H_A, DK_A, DV_A), F32)],
        scratch_shapes=[pltpu.VMEM((DK_A, HD), F32), pltpu.VMEM((c_sz, QKV_A), F32)],
        compiler_params=pltpu.CompilerParams(dimension_semantics=("parallel", "arbitrary"),
                                             vmem_limit_bytes=VMEM_LIMIT),
        name="gdn_prompt",
    )(proj, proj, proj, conv_w, *_gdn_params(a_log, dt_bias, gnorm))


def bias_tiles(rel_bias):
    d = jnp.arange(-(LANE - 1), N_BIAS_TILES * LANE, dtype=jnp.int32)
    tab = rel_bias[rel_bucket(d)].astype(F32).T
    tiles = []
    for dl in range(N_BIAS_TILES):
        w = jnp.pad(tab[:, dl * LANE:dl * LANE + 2 * LANE - 1], ((0, 0), (0, 1)))
        sh = jnp.tile(w, (1, LANE))[:, :LANE * (2 * LANE - 1)].reshape(-1, LANE, 2 * LANE - 1)
        tiles.append(sh[:, :, LANE - 1:])
    return jnp.stack(tiles, axis=1)


def _select_topk(sc, valid, krow, topk, idx_bits):
    key = jnp.where(valid, _sortable(sc), INT_MIN)
    kf = jnp.float32(topk)

    def count(m):
        return _col_reduce(jnp.where(m, 1.0, 0.0), jnp.sum)

    t0 = jnp.where(count(key >= 0) >= kf, 0, INT_MIN).astype(jnp.int32)

    def tbody(n, t):
        cand = t | jnp.left_shift(jnp.int32(1), 30 - n)
        return jnp.where(count(key >= cand) >= kf, cand, t)

    t = lax.fori_loop(0, 31, tbody, t0)
    gt = key > t
    eq = key == t
    m = kf - count(gt)

    def jbody(n, j):
        cand = j | jnp.left_shift(jnp.int32(1), idx_bits - 1 - n)
        return jnp.where(count(eq & (krow < cand)) <= m, cand, j)

    split = jnp.max(count(eq) - m) > 0.0
    j = lax.cond(split, lambda: lax.fori_loop(0, idx_bits, jbody, jnp.zeros_like(t)),
                 lambda: jnp.full_like(t, 1 << idx_bits))
    return valid & (gt | (eq & (krow < j)))


def _dsa_prompt_body(lc, pad, topk, q_ref, qi_ref, smq_ref, k_ref, v_ref, smk_ref, bias_ref, o_ref):
    i = pl.program_id(1)
    nkb = lc // LANE
    ki = smk_ref[0, :lc, SM_KI:SM_KI + D_IDX].astype(BF16)
    smt = smq_ref[0].T
    sc = jnp.zeros((lc, LANE), F32)
    for h in range(H_IDX):
        qih = qi_ref[0, :, h * D_IDX:(h + 1) * D_IDX].astype(BF16)
        d = lax.dot_general(ki, qih, NT_DIMS, preferred_element_type=F32)
        w = smt[SM_WI + h:SM_WI + h + 1, :] * (H_IDX ** -0.5 * D_IDX ** -0.5)
        sc = sc + jnp.maximum(d, 0.0) * w
    krow = lax.broadcasted_iota(jnp.int32, (lc, LANE), 0)
    qrow = i * LANE + lax.broadcasted_iota(jnp.int32, (lc, LANE), 1)
    valid = (krow <= qrow) & (krow >= pad)
    sel = _select_topk(sc, valid, krow, topk, max(1, (lc - 1).bit_length()))
    outs = []
    for h in range(H_B):
        sl = slice(h * DH_B, (h + 1) * DH_B)
        kh = k_ref[0, :lc, sl].astype(BF16)
        qh = (q_ref[0, :, sl] * (DH_B ** -0.5)).astype(BF16)
        lg = lax.dot_general(kh, qh, NT_DIMS, preferred_element_type=F32)
        bias = jnp.concatenate([bias_ref[h, jnp.clip(i - jb, 0, N_BIAS_TILES - 1)] for jb in range(nkb)], axis=0)
        lg = jnp.where(sel, lg + bias, -1e30)
        p = jnp.exp(lg - _col_reduce(lg, jnp.max))
        p = (p * (1.0 / _col_reduce(p, jnp.sum))).astype(BF16)
        vh = v_ref[0, :lc, sl].astype(BF16)
        outs.append(lax.dot_general(p, vh, TN_DIMS, preferred_element_type=F32))
    o_ref[0] = jnp.concatenate(outs, axis=1).astype(o_ref.dtype)


def _dsa_prompt_kernel(variants, pad, topk, *refs):
    _dispatch_variants(variants, _dsa_prompt_body, pad, topk, *refs)


def dsa_prompt(proj, btiles, pad, topk):
    b, l, _ = proj.shape
    nq = l // LANE
    qblk = lambda w, col: pl.BlockSpec((1, LANE, w), lambda bi, i: (bi, i, col // w))
    full = lambda w, col: pl.BlockSpec((1, l, w), lambda bi, i: (bi, 0, col // w))
    return pl.pallas_call(
        functools.partial(_dsa_prompt_kernel, _causal_variants(nq), pad, topk),
        grid=(b, nq),
        in_specs=[qblk(HD, COL_QB), qblk(HD, COL_QI), qblk(LANE, COL_SMALL),
                  full(HD, COL_KB), full(HD, COL_VB), full(LANE, COL_SMALL),
                  pl.BlockSpec(btiles.shape, lambda bi, i: (0, 0, 0, 0))],
        out_specs=pl.BlockSpec((1, LANE, HD), lambda bi, i: (bi, i, 0)),
        out_shape=jax.ShapeDtypeStruct((b, l, HD), BF16),
        compiler_params=pltpu.CompilerParams(dimension_semantics=("parallel", "arbitrary"),
                                             vmem_limit_bytes=VMEM_LIMIT),
        name="dsa_prompt",
    )(proj, proj, proj, proj, proj, proj, btiles)


def _mla_prep_kernel(h_ref, nm_ref, win_ref, qn_ref, kvn_ref, wuq_ref, wuk_ref, cos_ref, sin_ref, q_ref, kv_ref):
    x = _rms(h_ref[0], nm_ref[...]).astype(BF16)
    p = jnp.dot(x, win_ref[...], preferred_element_type=F32)
    cqn = _rms(p[:, CC_Q:CC_KV], qn_ref[...]).astype(BF16)
    qf = jnp.dot(cqn, wuq_ref[...], preferred_element_type=F32)
    cos, sin = cos_ref[...], sin_ref[...]
    q1, q2 = qf[:, H_C * LANE:(H_C + 1) * LANE], qf[:, (H_C + 1) * LANE:]
    q1r = q1 * cos - q2 * sin
    q2r = q2 * cos + q1 * sin
    head_of_lane = lax.broadcasted_iota(jnp.int32, (1, LANE), 1) // HALF
    for h in range(H_C):
        qlat = jnp.dot(qf[:, h * LANE:(h + 1) * LANE].astype(BF16), wuk_ref[h], preferred_element_type=F32)
        mine = head_of_lane == h
        q_ref[0, h] = jnp.concatenate([qlat, jnp.where(mine, q1r, 0.0), jnp.where(mine, q2r, 0.0)],
                                      axis=1).astype(q_ref.dtype)
    c = _rms(p[:, CC_KV:CC_K1], kvn_ref[...])
    k1, k2 = p[:, CC_K1:CC_K2], p[:, CC_K2:]
    kv_ref[0] = jnp.concatenate([c, k1 * cos - k2 * sin, k2 * cos + k1 * sin], axis=1)


def mla_prep(h, norm_w, w_in_p, qnorm, kvnorm, w_uq_p, w_uk_p, cos, sin):
    b, l, d = h.shape
    tm = min(LANE, l)
    const2 = lambda shp: pl.BlockSpec(shp, lambda bi, i: (0, 0))
    return pl.pallas_call(
        _mla_prep_kernel,
        grid=(b, l // tm),
        in_specs=[pl.BlockSpec((1, tm, d), lambda bi, i: (bi, i, 0)), const2((1, d)), const2(w_in_p.shape),
                  const2((1, Q_LORA)), const2((1, KV_LORA)), const2(w_uq_p.shape),
                  pl.BlockSpec(w_uk_p.shape, lambda bi, i: (0, 0, 0)),
                  pl.BlockSpec((tm, LANE), lambda bi, i: (i, 0)), pl.BlockSpec((tm, LANE), lambda bi, i: (i, 0))],
        out_specs=[pl.BlockSpec((1, H_C, tm, KVW), lambda bi, i: (bi, 0, i, 0)),
                   pl.BlockSpec((1, tm, KVW), lambda bi, i: (bi, i, 0))],
        out_shape=[jax.ShapeDtypeStruct((b, H_C, l, KVW), BF16), jax.ShapeDtypeStruct((b, l, KVW), F32)],
        compiler_params=pltpu.CompilerParams(dimension_semantics=("parallel", "arbitrary"),
                                             vmem_limit_bytes=VMEM_LIMIT),
        name="mla_prep",
    )(h, norm_w.reshape(1, d), w_in_p, qnorm.reshape(1, -1), kvnorm.reshape(1, -1), w_uq_p, w_uk_p, cos, sin)


def _mla_body(lc, pad, q_ref, kv_ref, o_ref):
    i = pl.program_id(1)
    q = q_ref[0].reshape(H_C * LANE, KVW)
    kb = kv_ref[0, :lc, :].astype(BF16)
    s = lax.dot_general(q, kb, NT_DIMS, preferred_element_type=F32) * MLA_SCALE
    s = s.reshape(H_C, LANE, lc)
    krow = lax.broadcasted_iota(jnp.int32, (LANE, lc), 1)
    qrow = i * LANE + lax.broadcasted_iota(jnp.int32, (LANE, lc), 0)
    ok = (krow <= qrow) & (krow >= pad)
    s = jnp.where(ok[None], s, -1e30)
    mx = jnp.max(s, axis=-1, keepdims=True)
    p = jnp.exp(s - mx)
    p = (p * (1.0 / jnp.sum(p, axis=-1, keepdims=True))).astype(BF16).reshape(H_C * LANE, lc)
    o = jnp.dot(p, kb[:, :KV_LORA], preferred_element_type=F32).astype(o_ref.dtype)
    for h in range(H_C):
        o_ref[0, :, h * KV_LORA:(h + 1) * KV_LORA] = o[h * LANE:(h + 1) * LANE]


def _mla_kernel(variants, pad, *refs):
    _dispatch_variants(variants, _mla_body, pad, *refs)


def mla_prompt(q, kv, pad):
    b, _, l, _ = q.shape
    nq = l // LANE
    return pl.pallas_call(
        functools.partial(_mla_kernel, _causal_variants(nq), pad),
        grid=(b, nq),
        in_specs=[pl.BlockSpec((1, H_C, LANE, KVW), lambda bi, i: (bi, 0, i, 0)),
                  pl.BlockSpec((1, l, KVW), lambda bi, i: (bi, 0, 0))],
        out_specs=pl.BlockSpec((1, LANE, H_C * KV_LORA), lambda bi, i: (bi, i, 0)),
        out_shape=jax.ShapeDtypeStruct((b, l, H_C * KV_LORA), BF16),
        compiler_params=pltpu.CompilerParams(dimension_semantics=("parallel", "arbitrary"),
                                             vmem_limit_bytes=VMEM_LIMIT),
        name="mla_prompt",
    )(q, kv)


def _gdn_decode_kernel(x_ref, conv_ref, small_ref, gate_ref, s_ref, convw_ref, alog_ref, dtb_ref, gnorm_ref,
                       o_ref, convo_ref, so_ref):
    nb = x_ref.shape[0]
    x = x_ref[...]
    cb = conv_ref[0]
    acc = x * convw_ref[CONV_W - 1:CONV_W, :]
    for j in range(CONV_W - 1):
        acc = acc + cb[:, j, :] * convw_ref[j:j + 1, :]
    for j in range(CONV_W - 2):
        convo_ref[:, j, :] = cb[:, j + 1, :]
    convo_ref[:, CONV_W - 2, :] = x
    y = acc * jax.nn.sigmoid(acc)
    q, k, v = y[:, :HD], y[:, HD:2 * HD], y[:, 2 * HD:]
    gmat = _head_group_matrix()
    q = q * lax.rsqrt(_dot_exact_rhs(q * q, gmat) + EPS) * (DK_A ** -0.5)
    k = k * lax.rsqrt(_dot_exact_rhs(k * k, gmat) + EPS)
    g, beta = _gdn_gates(small_ref[...], alog_ref, dtb_ref)
    eg = jnp.exp(g)
    rowid = lax.broadcasted_iota(jnp.int32, (nb, 1), 0)

    def body(bb, o_acc):
        mine = rowid == bb
        outs = []
        for h in range(H_A):
            sl = slice(h * DK_A, (h + 1) * DK_A)
            s = s_ref[0, bb, h]
            egh = eg[:, SM_ALPHA + h:SM_ALPHA + h + 1]
            eg_b = jnp.sum(jnp.where(mine, egh, 0.0), axis=0, keepdims=True)
            delta = (v[:, sl] - _hdot(k[:, sl], s) * egh) * beta[:, SM_BRAW + h:SM_BRAW + h + 1]
            s_new = s * eg_b + lax.dot_general(jnp.where(mine, k[:, sl], 0.0), jnp.where(mine, delta, 0.0), TN_DIMS,
                                               precision=HI, preferred_element_type=F32)
            so_ref[bb, h] = s_new
            outs.append(_hdot(q[:, sl], s_new))
        return jnp.where(mine, jnp.concatenate(outs, axis=1), o_acc)

    o = lax.fori_loop(0, nb, body, jnp.zeros((nb, HD), F32))
    ms_o = _dot_exact_rhs(o * o, gmat) * (1.0 / DV_A)
    gate = gate_ref[...]
    o_ref[...] = (o * lax.rsqrt(ms_o + EPS) * gnorm_ref[...] * (gate * jax.nn.sigmoid(gate))).astype(o_ref.dtype)


def gdn_decode_step(proj, conv_state, ssm_state, layer, conv_w, a_log, dt_bias, gnorm, nb=8):
    b = proj.shape[0]
    blk = lambda w, col: pl.BlockSpec((nb, w), lambda i: (i, col // w))
    const = lambda shp: pl.BlockSpec(shp, lambda i: (0, 0))
    return pl.pallas_call(
        _gdn_decode_kernel,
        grid=(b // nb,),
        in_specs=[blk(QKV_A, COL_QKV), pl.BlockSpec((1, nb, CONV_W - 1, QKV_A), lambda i: (layer, i, 0, 0)),
                  blk(LANE, COL_SMALL), blk(HD, COL_GATE),
                  pl.BlockSpec((1, nb, H_A, DK_A, DV_A), lambda i: (layer, i, 0, 0, 0)),
                  const((CONV_W, QKV_A)), const((1, LANE)), const((1, LANE)), const((1, HD))],
        out_specs=[pl.BlockSpec((nb, HD), lambda i: (i, 0)),
                   pl.BlockSpec((nb, CONV_W - 1, QKV_A), lambda i: (i, 0, 0)),
                   pl.BlockSpec((nb, H_A, DK_A, DV_A), lambda i: (i, 0, 0, 0))],
        out_shape=[jax.ShapeDtypeStruct((b, HD), BF16), jax.ShapeDtypeStruct(conv_state.shape[1:], F32),
                   jax.ShapeDtypeStruct(ssm_state.shape[1:], F32)],
        compiler_params=pltpu.CompilerParams(dimension_semantics=("parallel",), vmem_limit_bytes=VMEM_LIMIT),
        name="gdn_decode",
    )(proj, conv_state, proj, proj, ssm_state, conv_w, *_gdn_params(a_log, dt_bias, gnorm))


def _dsa_select_kernel(topk, n_pages, pt_ref, qi_ref, small_ref, *rest):
    ki_refs = rest[:DEC_PAGES]
    mask_ref, mnew_ref, sc_s = rest[DEC_PAGES:]
    j = pl.program_id(1)
    qi8 = _rows_to_heads(qi_ref[0], D_IDX).astype(BF16)
    sm = small_ref[0]
    pick = (lax.broadcasted_iota(jnp.int32, (H_IDX, LANE), 1)
            == lax.broadcasted_iota(jnp.int32, (H_IDX, LANE), 0) + SM_WI)
    w8 = jnp.sum(jnp.where(pick, sm, 0.0), axis=1, keepdims=True) * (H_IDX ** -0.5)
    kit = jnp.concatenate([r[0, 0] for r in ki_refs], axis=1).astype(BF16)
    d = jnp.dot(qi8, kit, preferred_element_type=F32)
    s = jnp.sum(jnp.maximum(d * (D_IDX ** -0.5), 0.0) * w8, axis=0, keepdims=True)
    for g in range(DEC_PAGES):
        sc_s[pl.ds(j * DEC_PAGES + g, 1), :] = s[:, g * PAGE_SIZE:(g + 1) * PAGE_SIZE]

    @pl.when(j == pl.num_programs(1) - 1)
    def _():
        past = n_pages * PAGE_SIZE
        ki_new = sm[:, SM_KI:SM_KI + D_IDX].astype(BF16).astype(F32)
        d_new = jnp.sum(qi8.astype(F32) * ki_new, axis=1, keepdims=True)
        s_new = jnp.sum(jnp.maximum(d_new * (D_IDX ** -0.5), 0.0) * w8, axis=0, keepdims=True)
        key = _sortable(sc_s[...])
        key_n = _sortable(s_new)
        idx = lax.broadcasted_iota(jnp.int32, key.shape, 0) * PAGE_SIZE + lax.broadcasted_iota(jnp.int32, key.shape, 1)
        kf = jnp.float32(topk)

        def count(m, mn):
            c = jnp.sum(jnp.sum(jnp.where(m, 1.0, 0.0), axis=0, keepdims=True), axis=1, keepdims=True)
            return c + jnp.where(mn, 1.0, 0.0)

        t0 = jnp.where(count(key >= 0, key_n >= 0) >= kf, 0, INT_MIN).astype(jnp.int32)

        def tbody(n, t):
            cand = t | jnp.left_shift(jnp.int32(1), 30 - n)
            return jnp.where(count(key >= cand, key_n >= cand) >= kf, cand, t)

        t = lax.fori_loop(0, 31, tbody, t0)
        m = kf - count(key > t, key_n > t)
        idx_bits = past.bit_length()

        def jbody(n, jj):
            cand = jj | jnp.left_shift(jnp.int32(1), idx_bits - 1 - n)
            return jnp.where(count((key == t) & (idx < cand), (key_n == t) & (past < cand)) <= m, cand, jj)

        jj = lax.fori_loop(0, idx_bits, jbody, jnp.zeros_like(t))
        mask_ref[0] = jnp.where((key > t) | ((key == t) & (idx < jj)), 1.0, 0.0)
        sel_n = (key_n > t) | ((key_n == t) & (past < jj))
        lane0 = lax.broadcasted_iota(jnp.int32, (1, LANE), 1) == 0
        mnew_ref[0] = jnp.where(lane0 & sel_n, 1.0, 0.0)


def dsa_decode_select(proj3, cache_kit, layer, page_table, topk):
    b = proj3.shape[0]
    n_pages = page_table.shape[1]
    page_spec = lambda g: pl.BlockSpec((1, 1, D_IDX, PAGE_SIZE),
                                       lambda bi, j, pt: (layer, pt[bi, j * DEC_PAGES + g], 0, 0))
    grid_spec = pltpu.PrefetchScalarGridSpec(
        num_scalar_prefetch=1,
        grid=(b, n_pages // DEC_PAGES),
        in_specs=[pl.BlockSpec((1, 1, HD), lambda bi, j, pt: (bi, 0, COL_QI // HD)),
                  pl.BlockSpec((1, 1, LANE), lambda bi, j, pt: (bi, 0, COL_SMALL // LANE))]
                 + [page_spec(g) for g in range(DEC_PAGES)],
        out_specs=[pl.BlockSpec((1, n_pages, PAGE_SIZE), lambda bi, j, pt: (bi, 0, 0)),
                   pl.BlockSpec((1, 1, LANE), lambda bi, j, pt: (bi, 0, 0))],
        scratch_shapes=[pltpu.VMEM((n_pages, PAGE_SIZE), F32)],
    )
    return pl.pallas_call(
        functools.partial(_dsa_select_kernel, topk, n_pages),
        grid_spec=grid_spec,
        out_shape=[jax.ShapeDtypeStruct((b, n_pages, PAGE_SIZE), F32), jax.ShapeDtypeStruct((b, 1, LANE), F32)],
        compiler_params=pltpu.CompilerParams(dimension_semantics=("parallel", "arbitrary"),
                                             vmem_limit_bytes=VMEM_LIMIT),
        name="dsa_decode_select",
    )(page_table, proj3, proj3, *([cache_kit] * DEC_PAGES))


def _dsa_attend_kernel(n_pages, pt_ref, thr_ref, q_ref, knt_ref, vnt_ref, mask_ref, mnew_ref, rb_ref, *rest):
    kt_refs, vt_refs = rest[:DEC_PAGES], rest[DEC_PAGES:2 * DEC_PAGES]
    o_ref, qbd_s, m_s, l_s, acc_s = rest[2 * DEC_PAGES:]
    j = pl.program_id(1)
    past = n_pages * PAGE_SIZE
    eye = lax.broadcasted_iota(jnp.int32, (H_B, LANE), 0) == lax.broadcasted_iota(jnp.int32, (H_B, LANE), 1)

    @pl.when(j == 0)
    def _():
        bd = lax.broadcasted_iota(jnp.int32, (H_B, HD), 1) // DH_B == lax.broadcasted_iota(jnp.int32, (H_B, HD), 0)
        qbd_s[...] = jnp.where(bd, jnp.broadcast_to(q_ref[0], (H_B, HD)), 0.0).astype(BF16)
        m_s[...] = jnp.full_like(m_s, -1e30)
        l_s[...] = jnp.zeros_like(l_s)
        acc_s[...] = jnp.zeros_like(acc_s)

    def attend(kts, vts, ok, first_key):
        n = len(kts)
        lg = jnp.concatenate([jnp.dot(qbd_s[...], kt.astype(BF16), preferred_element_type=F32) for kt in kts], axis=1)
        dist = past - first_key - lax.broadcasted_iota(jnp.int32, (1, n * PAGE_SIZE), 1)
        bias = jnp.broadcast_to(rb_ref[0], lg.shape)
        for bk in range(1, REL_BUCKETS):
            bias = jnp.where(dist >= thr_ref[bk], rb_ref[bk], bias)
        lg = jnp.where(ok, lg * (DH_B ** -0.5) + bias, -1e30)
        m_new = jnp.maximum(m_s[...], jnp.max(lg, axis=1, keepdims=True))
        alpha = jnp.exp(m_s[...] - m_new)
        p = jnp.where(ok, jnp.exp(lg - m_new), 0.0)
        l_s[...] = l_s[...] * alpha + jnp.sum(p, axis=1, keepdims=True)
        m_s[...] = m_new
        pb = p.astype(BF16)
        pv = [lax.dot_general(vts[g].astype(BF16), pb[:, g * PAGE_SIZE:(g + 1) * PAGE_SIZE], NT_DIMS,
                              preferred_element_type=F32) for g in range(n)]
        while len(pv) > 1:
            pv = [pv[i] + pv[i + 1] if i + 1 < len(pv) else pv[i] for i in range(0, len(pv), 2)]
        alpha_row = jnp.sum(jnp.where(eye, alpha, 0.0), axis=0, keepdims=True)
        acc_s[...] = acc_s[...] * alpha_row[:, :H_B] + pv[0]

    rows = mask_ref[0, pl.ds(j * DEC_PAGES, DEC_PAGES), :]
    ok = jnp.concatenate([rows[g:g + 1] for g in range(DEC_PAGES)], axis=1) > 0.0
    attend([r[0, 0].reshape(HD, PAGE_SIZE) for r in kt_refs], [r[0, 0].reshape(HD, PAGE_SIZE) for r in vt_refs],
           ok, j * (DEC_PAGES * PAGE_SIZE))

    @pl.when(j == pl.num_programs(1) - 1)
    def _():
        attend([knt_ref[0]], [vnt_ref[0]], mnew_ref[0] > 0.0, past)
        l_row = jnp.sum(jnp.where(eye, l_s[...], 0.0), axis=0, keepdims=True)
        o = acc_s[...] * (1.0 / l_row[:, :H_B])
        own = (lax.broadcasted_iota(jnp.int32, (HD, H_B), 0) // DH_B
               == lax.broadcasted_iota(jnp.int32, (HD, H_B), 1))
        o_ref[0] = jnp.sum(jnp.where(own, o, 0.0), axis=1, keepdims=True).astype(o_ref.dtype)


def _decode_thresholds(past):
    buckets = rel_bucket(jnp.arange(past + 1, dtype=jnp.int32))
    return jnp.sum(buckets[None, :] < jnp.arange(REL_BUCKETS, dtype=jnp.int32)[:, None], axis=1).astype(jnp.int32)


def dsa_decode_attend(proj, mask, mnew, cache_kt, cache_vt, layer, page_table, rel_bias):
    b = proj.shape[0]
    n_pages = page_table.shape[1]
    past = n_pages * PAGE_SIZE
    thr = _decode_thresholds(past)
    rb = rel_bias.astype(F32)[:, :, None]
    proj3 = proj.reshape(b, 1, AB_COLS_P)
    one_key_page = lambda col: jnp.pad(proj[:, col:col + HD, None], ((0, 0), (0, 0), (0, PAGE_SIZE - 1)))
    page_spec = lambda g: pl.BlockSpec((1, 1, H_B, DH_B, PAGE_SIZE),
                                       lambda bi, j, pt, th: (layer, pt[bi, j * DEC_PAGES + g], 0, 0, 0))
    per_b = lambda shp: pl.BlockSpec((1,) + shp, lambda bi, j, pt, th: (bi, 0, 0))
    grid_spec = pltpu.PrefetchScalarGridSpec(
        num_scalar_prefetch=2,
        grid=(b, n_pages // DEC_PAGES),
        in_specs=[pl.BlockSpec((1, 1, HD), lambda bi, j, pt, th: (bi, 0, COL_QB // HD)),
                  per_b((HD, PAGE_SIZE)), per_b((HD, PAGE_SIZE)), per_b((n_pages, PAGE_SIZE)), per_b((1, LANE)),
                  pl.BlockSpec(rb.shape, lambda bi, j, pt, th: (0, 0, 0))]
                 + [page_spec(g) for g in range(DEC_PAGES)] * 2,
        out_specs=per_b((HD, 1)),
        scratch_shapes=[pltpu.VMEM((H_B, HD), BF16), pltpu.VMEM((H_B, 1), F32), pltpu.VMEM((H_B, 1), F32),
                        pltpu.VMEM((HD, H_B), F32)],
    )
    return pl.pallas_call(
        functools.partial(_dsa_attend_kernel, n_pages),
        grid_spec=grid_spec,
        out_shape=jax.ShapeDtypeStruct((b, HD, 1), F32),
        compiler_params=pltpu.CompilerParams(dimension_semantics=("parallel", "arbitrary"),
                                             vmem_limit_bytes=VMEM_LIMIT),
        name="dsa_decode_attend",
    )(page_table, thr, proj3, one_key_page(COL_KB), one_key_page(COL_VB), mask, mnew, rb,
      *([cache_kt] * DEC_PAGES), *([cache_vt] * DEC_PAGES))


def _online_update(lg, m_s, l_s):
    m_new = jnp.maximum(m_s[...], jnp.max(lg, axis=1, keepdims=True))
    alpha = jnp.exp(m_s[...] - m_new)
    p = jnp.exp(lg - m_new)
    l_s[...] = l_s[...] * alpha + jnp.sum(p, axis=1, keepdims=True)
    m_s[...] = m_new
    return p, alpha


def _mla_decode_kernel(pt_ref, q_ref, kvn_ref, exp_ref, *rest):
    c_refs, kpe_refs = rest[:MLA_DEC_PAGES], rest[MLA_DEC_PAGES:2 * MLA_DEC_PAGES]
    o_ref, m_s, l_s, acc_s = rest[2 * MLA_DEC_PAGES:]
    j = pl.program_id(1)

    @pl.when(j == 0)
    def _():
        m_s[...] = jnp.full_like(m_s, -1e30)
        l_s[...] = jnp.zeros_like(l_s)
        acc_s[...] = jnp.zeros_like(acc_s)

    q = q_ref[0]
    q_lat = q[:, :KV_LORA]
    q_pe = jnp.dot(q[:, KV_LORA:], exp_ref[...], preferred_element_type=F32).astype(BF16)
    cb = jnp.concatenate([r[0, 0] for r in c_refs], axis=0).astype(BF16)
    kpet = jnp.concatenate([r[0, 0] for r in kpe_refs], axis=1).astype(BF16)
    lg = (lax.dot_general(q_lat, cb, NT_DIMS, preferred_element_type=F32)
          + jnp.dot(q_pe, kpet, preferred_element_type=F32)) * MLA_SCALE
    p, alpha = _online_update(lg, m_s, l_s)
    acc_s[...] = acc_s[...] * alpha + jnp.dot(p.astype(BF16), cb, preferred_element_type=F32)

    @pl.when(j == pl.num_programs(1) - 1)
    def _():
        kvn = kvn_ref[0].astype(BF16).astype(F32)
        lgn = jnp.sum(q.astype(F32) * kvn, axis=1, keepdims=True) * MLA_SCALE
        pn, alpha_n = _online_update(lgn, m_s, l_s)
        acc = acc_s[...] * alpha_n + pn.astype(BF16).astype(F32) * kvn[:, :KV_LORA]
        o = acc * (1.0 / l_s[...])
        o_ref[0] = jnp.concatenate([o[h:h + 1] for h in range(H_C)], axis=1).astype(o_ref.dtype)


def mla_decode(q, kvn, cache_c, cache_kpet, layer, page_table):
    b = q.shape[0]
    n_pages = page_table.shape[1]
    src = jnp.concatenate([jnp.tile(jnp.arange(HALF), H_C), HALF + jnp.tile(jnp.arange(HALF), H_C)])
    expand = (src[:, None] == jnp.arange(QK_ROPE)[None, :]).astype(BF16)
    cspec = lambda g: pl.BlockSpec((1, 1, PAGE_SIZE, KV_LORA),
                                   lambda bi, j, pt: (layer, pt[bi, j * MLA_DEC_PAGES + g], 0, 0))
    kspec = lambda g: pl.BlockSpec((1, 1, QK_ROPE, PAGE_SIZE),
                                   lambda bi, j, pt: (layer, pt[bi, j * MLA_DEC_PAGES + g], 0, 0))
    grid_spec = pltpu.PrefetchScalarGridSpec(
        num_scalar_prefetch=1,
        grid=(b, n_pages // MLA_DEC_PAGES),
        in_specs=[pl.BlockSpec((1, H_C, KVW), lambda bi, j, pt: (bi, 0, 0)),
                  pl.BlockSpec((1, 1, KVW), lambda bi, j, pt: (bi, 0, 0)),
                  pl.BlockSpec(expand.shape, lambda bi, j, pt: (0, 0))]
                 + [cspec(g) for g in range(MLA_DEC_PAGES)] + [kspec(g) for g in range(MLA_DEC_PAGES)],
        out_specs=pl.BlockSpec((1, 1, H_C * KV_LORA), lambda bi, j, pt: (bi, 0, 0)),
        scratch_shapes=[pltpu.VMEM((H_C, 1), F32), pltpu.VMEM((H_C, 1), F32), pltpu.VMEM((H_C, KV_LORA), F32)],
    )
    return pl.pallas_call(
        _mla_decode_kernel,
        grid_spec=grid_spec,
        out_shape=jax.ShapeDtypeStruct((b, 1, H_C * KV_LORA), BF16),
        compiler_params=pltpu.CompilerParams(dimension_semantics=("parallel", "arbitrary"),
                                             vmem_limit_bytes=VMEM_LIMIT),
        name="mla_decode",
    )(page_table, q, kvn, expand, *([cache_c] * MLA_DEC_PAGES), *([cache_kpet] * MLA_DEC_PAGES))


def _prep_w_ab(w):
    qkv_a, alpha, braw, gate, q_b, k_b, v_b, qi, wi, ki = split_cols(w, AB_SIZES)
    z = jnp.zeros((w.shape[0], SM_KI - SM_WI - H_IDX), w.dtype)
    return jnp.concatenate([qkv_a, gate, q_b, k_b, v_b, qi, alpha, braw, wi, z, ki], axis=1).astype(BF16)


def _prep_w_c(w_in, w_uq, w_uk):
    cq, ckv, kpe = split_cols(w_in, C_SIZES)
    w_in_p = jnp.concatenate([cq, ckv, jnp.tile(kpe[:, :HALF], (1, H_C)), jnp.tile(kpe[:, HALF:], (1, H_C))], axis=1)
    wq = w_uq.reshape(Q_LORA, H_C, QK_NOPE + QK_ROPE)
    nope = jnp.pad(wq[:, :, :QK_NOPE], ((0, 0), (0, 0), (0, LANE - QK_NOPE))).reshape(Q_LORA, H_C * LANE)
    r1 = wq[:, :, QK_NOPE:QK_NOPE + HALF].reshape(Q_LORA, H_C * HALF)
    r2 = wq[:, :, QK_NOPE + HALF:].reshape(Q_LORA, H_C * HALF)
    w_uq_p = jnp.concatenate([nope, r1, r2], axis=1)
    w_uk_p = jnp.pad(jnp.transpose(w_uk, (1, 2, 0)), ((0, 0), (0, LANE - QK_NOPE), (0, 0)))
    return w_in_p.astype(BF16), w_uq_p.astype(BF16), w_uk_p.astype(BF16)


def _rope_tables(pos):
    inv = ROPE_BASE ** (-jnp.arange(HALF, dtype=jnp.float32) / HALF)
    ang = pos.astype(jnp.float32)[:, None] * inv[None, :]
    return jnp.tile(jnp.cos(ang), (1, LANE // HALF)), jnp.tile(jnp.sin(ang), (1, LANE // HALF))


def _kpe_from_rows(kv):
    return jnp.concatenate([kv[..., KV_LORA:KV_LORA + HALF], kv[..., KV_LORA + LANE:KV_LORA + LANE + HALF]], axis=-1)


def kernel(x_prompt, x_sample, state_a_conv, state_a_ssm, cache_b_k, cache_b_v, cache_b_kidx,
           cache_c_latent, cache_c_kpe, page_table, meta_tokens, rel_bias, norm_mix, norm_ffn, norm_final,
           w_in_ab, conv_a, a_log, dt_bias_a, gnorm_a, w_out_ab, w_in_c, qnorm_c, kvnorm_c,
           w_uq_c, w_uk_c, w_uv_c, w_out_c, w_up, w_down):
    w_up_b = w_up.astype(BF16)
    w_down_b = w_down.astype(BF16)
    w_out_ab_b = w_out_ab.astype(BF16)
    w_out_c_b = w_out_c.astype(BF16)
    w_uv_b = jnp.transpose(w_uv_c, (0, 2, 1, 3)).astype(BF16)
    w_in_ab_p = [_prep_w_ab(w_in_ab[j]) for j in range(w_in_ab.shape[0])]
    w_c_p = [_prep_w_c(w_in_c[j], w_uq_c[j], w_uk_c[j]) for j in range(w_in_c.shape[0])]

    b, seq, d = x_prompt.shape
    lp = N_META + seq
    pad = (-lp) % LANE
    l = pad + lp
    n = b * l
    topk = min(TOPK_MAX, SEQ // 4)
    meta = jnp.broadcast_to(meta_tokens[None].astype(x_prompt.dtype), (b, N_META, d))
    h = jnp.concatenate([jnp.zeros((b, pad, d), x_prompt.dtype), meta, x_prompt], axis=1).reshape(n, d)
    cos_p, sin_p = _rope_tables(jnp.arange(l, dtype=jnp.int32) - pad)
    a_conv_p, a_ssm_p, b_k_p, b_v_p, b_kidx_p, c_lat_p, c_kpe_p = [], [], [], [], [], [], []
    for li in range(DEPTH):
        j = li // 2
        last = li == DEPTH - 1
        if li % 2 == 0:
            proj = rms_linear(h, norm_mix[li], w_in_ab_p[j]).reshape(b, l, AB_COLS_P)
            o_a, s_fin = gdn_prompt(proj, conv_a[j], a_log[j], dt_bias_a[j], gnorm_a[j], pad)
            o_b = dsa_prompt(proj, bias_tiles(rel_bias), pad, topk)
            projs = [(o_a.reshape(n, HD), w_out_ab_b[j, :HD]), (o_b.reshape(n, HD), w_out_ab_b[j, HD:])]
            head_w = None
            a_conv_p.append(proj[:, l - (CONV_W - 1):, COL_QKV:COL_QKV + QKV_A])
            a_ssm_p.append(s_fin)
            b_k_p.append(proj[:, pad:, COL_KB:COL_KB + HD].reshape(b, lp, H_B, DH_B))
            b_v_p.append(proj[:, pad:, COL_VB:COL_VB + HD].reshape(b, lp, H_B, DH_B))
            b_kidx_p.append(proj[:, pad:, COL_SMALL + SM_KI:COL_SMALL + SM_KI + D_IDX])
        else:
            q_all, kv = mla_prep(h.reshape(b, l, d), norm_mix[li], *w_c_p[j][:1], qnorm_c[j], kvnorm_c[j],
                                 *w_c_p[j][1:], cos_p, sin_p)
            o_lat = mla_prompt(q_all, kv, pad)
            projs = [(o_lat.reshape(n, H_C * KV_LORA), w_out_c_b[j])]
            head_w = w_uv_b[j]
            c_lat_p.append(kv[:, pad:, :KV_LORA])
            c_kpe_p.append(_kpe_from_rows(kv[:, pad:]))
        h = proj_mlp(h, projs, norm_ffn[li], w_up_b[li], w_down_b[li], norm_final if last else None, head_w)
    y_p = h.reshape(b, l, d)[:, pad + N_META:]

    bs = x_sample.shape[0]
    n_pages = page_table.shape[1]
    past = n_pages * PAGE_SIZE
    topk_s = min(TOPK_MAX, (past + DEC_SEQ) // 4)
    hs = x_sample.reshape(bs, d)
    cos_s, sin_s = _rope_tables(jnp.full((bs,), past, jnp.int32))
    cache_kit = jnp.transpose(cache_b_kidx, (0, 1, 3, 2))
    cache_kt = jnp.transpose(cache_b_k, (0, 1, 3, 4, 2))
    cache_vt = jnp.transpose(cache_b_v, (0, 1, 3, 4, 2))
    cache_kpet = jnp.transpose(cache_c_kpe, (0, 1, 3, 2))
    a_conv_s, a_ssm_s, b_k_s, b_v_s, b_kidx_s, c_lat_s, c_kpe_s = [], [], [], [], [], [], []
    for li in range(DEPTH):
        j = li // 2
        last = li == DEPTH - 1
        if li % 2 == 0:
            proj = rms_linear(hs, norm_mix[li], w_in_ab_p[j])
            proj3 = proj.reshape(bs, 1, AB_COLS_P)
            o_a, conv_new, s_new = gdn_decode_step(proj, state_a_conv, state_a_ssm, j, conv_a[j], a_log[j],
                                                   dt_bias_a[j], gnorm_a[j])
            mask, mnew = dsa_decode_select(proj3, cache_kit, j, page_table, topk_s)
            o_b = dsa_decode_attend(proj, mask, mnew, cache_kt, cache_vt, j, page_table, rel_bias)
            projs = [(o_a, w_out_ab_b[j, :HD]), (o_b.reshape(bs, HD), w_out_ab_b[j, HD:])]
            head_w = None
            a_conv_s.append(conv_new)
            a_ssm_s.append(s_new)
            b_k_s.append(proj[:, COL_KB:COL_KB + HD].reshape(bs, DEC_SEQ, H_B, DH_B))
            b_v_s.append(proj[:, COL_VB:COL_VB + HD].reshape(bs, DEC_SEQ, H_B, DH_B))
            b_kidx_s.append(proj[:, COL_SMALL + SM_KI:COL_SMALL + SM_KI + D_IDX].reshape(bs, DEC_SEQ, D_IDX))
        else:
            q_all, kv = mla_prep(hs.reshape(1, bs, d), norm_mix[li], *w_c_p[j][:1], qnorm_c[j], kvnorm_c[j],
                                 *w_c_p[j][1:], cos_s, sin_s)
            o_lat = mla_decode(jnp.transpose(q_all[0], (1, 0, 2)), kv.reshape(bs, 1, KVW), cache_c_latent,
                               cache_kpet, j, page_table)
            projs = [(o_lat.reshape(bs, H_C * KV_LORA), w_out_c_b[j])]
            head_w = w_uv_b[j]
            c_lat_s.append(kv[0, :, :KV_LORA].reshape(bs, DEC_SEQ, KV_LORA))
            c_kpe_s.append(_kpe_from_rows(kv[0]).reshape(bs, DEC_SEQ, QK_ROPE))
        hs = proj_mlp(hs, projs, norm_ffn[li], w_up_b[li], w_down_b[li], norm_final if last else None, head_w)
    y_s = hs.reshape(bs, DEC_SEQ, d)

    stack = jnp.stack
    return (y_p, y_s,
            stack(a_conv_p), stack(a_ssm_p), stack(b_k_p), stack(b_v_p), stack(b_kidx_p), stack(c_lat_p), stack(c_kpe_p),
            stack(a_conv_s), stack(a_ssm_s), stack(b_k_s), stack(b_v_s), stack(b_kidx_s), stack(c_lat_s), stack(c_kpe_s))
```

```python
import functools
import math
import jax
import jax.numpy as jnp
from jax import lax
from jax.experimental import pallas as pl
from jax.experimental.pallas import tpu as pltpu

D_MODEL = 1024
BATCH = 8
SEQ = 2048
DEPTH = 2
DEC_BATCH = 32
DEC_SEQ = 1
PAST_LEN = 16384
PAGE_SIZE = 128

N_META = 16
EPS = 1e-6
H_A = 8
DK_A = 64
DV_A = 64
CONV_W = 4
H_B = 8
DH_B = 64
H_IDX = 8
D_IDX = 64
TOPK_MAX = 256
REL_BUCKETS = 32
REL_MAX_DIST = 1024
H_C = 8
Q_LORA = 384
KV_LORA = 256
QK_NOPE = 64
QK_ROPE = 32
V_C = 128
ROPE_BASE = 10000.0
D_FF = 4 * D_MODEL

QKV_A = H_A * (2 * DK_A + DV_A)
AB_SIZES = (QKV_A, H_A, H_A, H_A * DV_A, H_B * DH_B, H_B * DH_B, H_B * DH_B, H_IDX * D_IDX, H_IDX, D_IDX)
C_SIZES = (Q_LORA, KV_LORA, QK_ROPE)
MLA_SCALE = (QK_NOPE + QK_ROPE) ** -0.5

BF16 = jnp.bfloat16
F32 = jnp.float32
LANE = 128
VMEM_LIMIT = 56 * 1024 * 1024
INT_MIN = -(2 ** 31)
NT_DIMS = (((1,), (1,)), ((), ()))
TN_DIMS = (((0,), (0,)), ((), ()))
HI = lax.Precision.HIGHEST
HD = H_A * DK_A
HALF = QK_ROPE // 2

COL_QKV, COL_GATE, COL_QB, COL_KB, COL_VB, COL_QI, COL_SMALL = 0, 1536, 2048, 2560, 3072, 3584, 4096
AB_COLS_P = COL_SMALL + LANE
SM_ALPHA, SM_BRAW, SM_WI, SM_KI = 0, 8, 16, 64
CC_Q, CC_KV, CC_K1, CC_K2, C_COLS_P = 0, 384, 640, 768, 896
KVW = KV_LORA + 2 * LANE
N_BIAS_TILES = 9
DEC_PAGES = 16
MLA_DEC_PAGES = 32


def split_cols(x, sizes):
    out, off = [], 0
    for s in sizes:
        out.append(x[..., off:off + s])
        off += s
    return out


def rel_bucket(dist):
    dist = jnp.maximum(dist, 0)
    max_exact = REL_BUCKETS // 2
    log_ratio = jnp.log(jnp.maximum(dist, 1).astype(jnp.float32) / max_exact) / math.log(REL_MAX_DIST / max_exact)
    large = max_exact + (log_ratio * (REL_BUCKETS - max_exact)).astype(jnp.int32)
    return jnp.where(dist < max_exact, dist, jnp.minimum(large, REL_BUCKETS - 1))


def _rms(x, w):
    return x * lax.rsqrt(jnp.mean(x * x, axis=-1, keepdims=True) + EPS) * w


def _hdot(a, b):
    return jnp.dot(a, b, precision=HI, preferred_element_type=F32)


def _dot_exact_rhs(a, b_bf16):
    a_hi = a.astype(BF16)
    a_lo = (a - a_hi.astype(F32)).astype(BF16)
    return jnp.dot(a_hi, b_bf16, preferred_element_type=F32) + jnp.dot(a_lo, b_bf16, preferred_element_type=F32)


def _split2(a):
    hi = a.astype(BF16)
    return hi, (a - hi.astype(F32)).astype(BF16)


def _dot3(ap, bp):
    (ah, al), (bh, bl) = ap, bp
    return (jnp.dot(ah, bh, preferred_element_type=F32) + jnp.dot(al, bh, preferred_element_type=F32)
            + jnp.dot(ah, bl, preferred_element_type=F32))


def _sortable(x):
    bits = pltpu.bitcast(x + 0.0, jnp.int32)
    return bits ^ ((bits >> 31) & 0x7FFFFFFF)


def _rows_to_heads(row, width):
    return jnp.concatenate([row[:, h * width:(h + 1) * width] for h in range(row.shape[1] // width)], axis=0)


def _col_reduce(x, op, groups=8):
    l, w = x.shape
    if l % (8 * groups) == 0:
        x = op(op(x.reshape(groups, l // (8 * groups), 8, w), axis=1), axis=0)
    return op(x, axis=0, keepdims=True)


def _causal_variants(nq):
    n_var = min(8, nq)
    his = [nq - (n_var - 1 - v) * (nq // n_var) for v in range(n_var)]
    return [(hi, hi * LANE) for hi in his]


def _dispatch_variants(variants, body, *args):
    i = pl.program_id(1)
    lo = 0
    for hi, lc in variants:
        pl.when((i >= lo) & (i < hi))(functools.partial(body, lc, *args))
        lo = hi


def _rms_linear_kernel(x_ref, nw_ref, w_ref, o_ref):
    xn = _rms(x_ref[...], nw_ref[...]).astype(BF16)
    o_ref[...] = jnp.dot(xn, w_ref[...], preferred_element_type=F32)


def rms_linear(x, norm_w, w, tm=256):
    n, d = x.shape
    m = w.shape[1]
    tm = min(tm, n)
    return pl.pallas_call(
        _rms_linear_kernel,
        grid=(pl.cdiv(n, tm),),
        in_specs=[pl.BlockSpec((tm, d), lambda i: (i, 0)), pl.BlockSpec((1, d), lambda i: (0, 0)),
                  pl.BlockSpec((d, m), lambda i: (0, 0))],
        out_specs=pl.BlockSpec((tm, m), lambda i: (i, 0)),
        out_shape=jax.ShapeDtypeStruct((n, m), F32),
        compiler_params=pltpu.CompilerParams(dimension_semantics=("parallel",), vmem_limit_bytes=VMEM_LIMIT),
        name="rms_linear",
    )(x, norm_w.reshape(1, d), w)


def _proj_mlp_kernel(n_proj, n_heads, final, *refs):
    refs = list(refs)
    h_ref = refs.pop(0)
    a_refs = [refs.pop(0) for _ in range(n_proj)]
    w_refs = [refs.pop(0) for _ in range(n_proj)]
    hw_ref = refs.pop(0) if n_heads else None
    nw_ref, wup_ref, wdn_ref = refs.pop(0), refs.pop(0), refs.pop(0)
    nf_ref = refs.pop(0) if final else None
    o_ref, h1_s, xn_s, acc_s = refs
    f = pl.program_id(1)

    @pl.when(f == 0)
    def _():
        h1 = h_ref[...]
        for a_ref, w_ref in zip(a_refs, w_refs):
            a = a_ref[...]
            if n_heads:
                kh = a.shape[1] // n_heads
                a = jnp.concatenate([jnp.dot(a[:, hh * kh:(hh + 1) * kh].astype(BF16), hw_ref[hh],
                                             preferred_element_type=F32) for hh in range(n_heads)], axis=1)
            h1 = h1 + jnp.dot(a.astype(BF16), w_ref[...], preferred_element_type=F32)
        h1_s[...] = h1
        xn_s[...] = _rms(h1, nw_ref[...]).astype(BF16)
        acc_s[...] = jnp.zeros_like(acc_s)

    u = jnp.dot(xn_s[...], wup_ref[...], preferred_element_type=F32)
    u = jnp.square(jnp.maximum(u, 0.0)).astype(BF16)
    acc_s[...] += jnp.dot(u, wdn_ref[...], preferred_element_type=F32)

    @pl.when(f == pl.num_programs(1) - 1)
    def _():
        h2 = h1_s[...] + acc_s[...]
        if final:
            h2 = _rms(h2, nf_ref[...])
        o_ref[...] = h2


def proj_mlp(h, projs, norm_w, w_up, w_down, norm_final=None, head_w=None, tm=1024, tf=1024):
    n, d = h.shape
    ff = w_up.shape[1]
    tm = min(tm, n)
    final = norm_final is not None
    n_heads = 0 if head_w is None else head_w.shape[0]
    row = lambda i, f: (i, 0)
    const = lambda i, f: (0, 0)
    in_specs = [pl.BlockSpec((tm, d), row)]
    in_specs += [pl.BlockSpec((tm, a.shape[1]), row) for a, _ in projs]
    in_specs += [pl.BlockSpec(w.shape, const) for _, w in projs]
    args = [h] + [a for a, _ in projs] + [w for _, w in projs]
    if n_heads:
        in_specs.append(pl.BlockSpec(head_w.shape, lambda i, f: (0, 0, 0)))
        args.append(head_w)
    in_specs += [pl.BlockSpec((1, d), const), pl.BlockSpec((d, tf), lambda i, f: (0, f)),
                 pl.BlockSpec((tf, d), lambda i, f: (f, 0))]
    args += [norm_w.reshape(1, d), w_up, w_down]
    if final:
        in_specs.append(pl.BlockSpec((1, d), const))
        args.append(norm_final.reshape(1, d))
    return pl.pallas_call(
        functools.partial(_proj_mlp_kernel, len(projs), n_heads, final),
        grid=(pl.cdiv(n, tm), ff // tf),
        in_specs=in_specs,
        out_specs=pl.BlockSpec((tm, d), row),
        out_shape=jax.ShapeDtypeStruct((n, d), F32),
        scratch_shapes=[pltpu.VMEM((tm, d), F32), pltpu.VMEM((tm, d), BF16), pltpu.VMEM((tm, d), F32)],
        compiler_params=pltpu.CompilerParams(dimension_semantics=("parallel", "arbitrary"),
                                             vmem_limit_bytes=VMEM_LIMIT),
        name="proj_mlp",
    )(*args)


def _gdn_gates(sm, alog_ref, dtb_ref):
    zs = sm + dtb_ref[...]
    g = -jnp.exp(alog_ref[...]) * (jnp.maximum(zs, 0.0) + jnp.log(1.0 + jnp.exp(-jnp.abs(zs))))
    return g, jax.nn.sigmoid(sm)


def _head_group_matrix():
    gi = lax.broadcasted_iota(jnp.int32, (HD, HD), 0) // DK_A
    gj = lax.broadcasted_iota(jnp.int32, (HD, HD), 1) // DK_A
    return jnp.where(gi == gj, 1.0, 0.0).astype(BF16)


def _gdn_kernel(c_sz, pad, qkv_ref, small_ref, gate_ref, convw_ref, alog_ref, dtb_ref, gnorm_ref,
                o_ref, sfin_ref, s_s, prev_s):
    c = pl.program_id(1)

    @pl.when(c == 0)
    def _():
        s_s[...] = jnp.zeros_like(s_s)
        prev_s[...] = jnp.zeros_like(prev_s)

    row = lax.broadcasted_iota(jnp.int32, (c_sz, 1), 0)
    valid = (row + c * c_sz) >= pad
    x = jnp.where(valid, qkv_ref[0], 0.0)
    prevx = prev_s[...]
    acc = x * convw_ref[CONV_W - 1:CONV_W, :]
    for sft in range(1, CONV_W):
        shifted = jnp.where(row < sft, pltpu.roll(prevx, sft, 0), pltpu.roll(x, sft, 0))
        acc = acc + shifted * convw_ref[CONV_W - 1 - sft:CONV_W - sft, :]
    prev_s[...] = x
    y = acc * jax.nn.sigmoid(acc)
    q, k, v = y[:, :HD], y[:, HD:2 * HD], y[:, 2 * HD:]
    gmat = _head_group_matrix()
    ss = jnp.dot(jnp.concatenate([q * q, k * k], axis=0).astype(BF16), gmat, preferred_element_type=F32)
    q = q * lax.rsqrt(ss[:c_sz] + EPS) * (DK_A ** -0.5)
    k = k * lax.rsqrt(ss[c_sz:] + EPS)

    g, beta = _gdn_gates(small_ref[0], alog_ref, dtb_ref)
    g = jnp.where(valid, g, 0.0)
    beta = jnp.where(valid, beta, 0.0)
    ti = lax.broadcasted_iota(jnp.int32, (c_sz, c_sz), 0)
    tj = lax.broadcasted_iota(jnp.int32, (c_sz, c_sz), 1)
    incl = ti >= tj
    strict = ti > tj
    gc = _hdot(jnp.where(incl, 1.0, 0.0), g)
    gct = gc.T
    ei = lax.broadcasted_iota(jnp.int32, (LANE, HD), 0)
    ej = lax.broadcasted_iota(jnp.int32, (LANE, HD), 1) // DK_A
    gc_w = _hdot(gc, jnp.where(ei == ej + SM_ALPHA, 1.0, 0.0))
    beta_w = _dot_exact_rhs(beta, jnp.where(ei == ej + SM_BRAW, 1.0, 0.0).astype(BF16))
    gl_w = gc_w[c_sz - 1:c_sz, :]
    eg_w = jnp.exp(gc_w)
    q_s = q * eg_w
    k_b = k * (beta_w * eg_w)
    v_b = v * beta_w
    k_l = k * jnp.exp(gl_w - gc_w)
    s_old = s_s[...]
    s_old_b = s_old.astype(BF16)
    n_fac = max(1, (c_sz - 1).bit_length())
    sls = [slice(h * DK_A, (h + 1) * DK_A) for h in range(H_A)]
    k16, q16 = k.astype(BF16), q.astype(BF16)
    decays, ms, xss = [], [], []
    for h, sl in enumerate(sls):
        diff = gc[:, SM_ALPHA + h:SM_ALPHA + h + 1] - gct[SM_ALPHA + h:SM_ALPHA + h + 1, :]
        decay = jnp.exp(jnp.where(incl, diff, -1e30))
        kk = lax.dot_general(k16[:, sl], k16[:, sl], NT_DIMS, preferred_element_type=F32)
        ms.append(-jnp.where(strict, beta[:, SM_BRAW + h:SM_BRAW + h + 1] * kk * decay, 0.0))
        xss.append(jnp.concatenate([v_b[:, sl], k_b[:, sl]], axis=1))
        decays.append(decay)
    for s in range(n_fac):
        msp = [_split2(m) for m in ms]
        xss = [x + _dot3(mp, _split2(x)) for x, mp in zip(xss, msp)]
        if s < n_fac - 1:
            ms = [_dot3(mp, mp) for mp in msp]
    outs, s_upd = [], []
    for h, sl in enumerate(sls):
        xs = xss[h]
        sh = s_old_b[:, sl]
        v_new = xs[:, :DV_A] - jnp.dot(xs[:, DV_A:].astype(BF16), sh, preferred_element_type=F32)
        v_new_b = v_new.astype(BF16)
        qk = lax.dot_general(q16[:, sl], k16[:, sl], NT_DIMS, preferred_element_type=F32) * decays[h]
        outs.append(jnp.dot(q_s[:, sl].astype(BF16), sh, preferred_element_type=F32)
                    + jnp.dot(qk.astype(BF16), v_new_b, preferred_element_type=F32))
        s_upd.append(lax.dot_general(k_l[:, sl].astype(BF16), v_new_b, TN_DIMS, preferred_element_type=F32))
    s_new = s_old * jnp.exp(gl_w) + jnp.concatenate(s_upd, axis=1)
    s_s[...] = s_new
    o = jnp.concatenate(outs, axis=1)
    ms_o = jnp.dot((o * o).astype(BF16), gmat, preferred_element_type=F32) * (1.0 / DV_A)
    gate = gate_ref[0]
    o_ref[0] = (o * lax.rsqrt(ms_o + EPS) * gnorm_ref[...] * (gate * jax.nn.sigmoid(gate))).astype(o_ref.dtype)

    @pl.when(c == pl.num_programs(1) - 1)
    def _():
        for h in range(H_A):
            sfin_ref[0, h] = s_new[:, h * DV_A:(h + 1) * DV_A]


def _gdn_params(a_log, dt_bias, gnorm):
    alog_p = jnp.zeros((1, LANE), F32).at[0, SM_ALPHA:SM_ALPHA + H_A].set(a_log)
    dtb_p = jnp.zeros((1, LANE), F32).at[0, SM_ALPHA:SM_ALPHA + H_A].set(dt_bias)
    return alog_p, dtb_p, jnp.tile(gnorm, H_A).reshape(1, HD)


def gdn_prompt(proj, conv_w, a_log, dt_bias, gnorm, pad, c_sz=LANE):
    b, l, _ = proj.shape
    blk = lambda w, col: pl.BlockSpec((1, c_sz, w), lambda bi, ci: (bi, ci, col // w))
    const = lambda shp: pl.BlockSpec(shp, lambda bi, ci: (0, 0))
    return pl.pallas_call(
        functools.partial(_gdn_kernel, c_sz, pad),
        grid=(b, l // c_sz),
        in_specs=[blk(QKV_A, COL_QKV), blk(LANE, COL_SMALL), blk(HD, COL_GATE), const((CONV_W, QKV_A)),
                  const((1, LANE)), const((1, LANE)), const((1, HD))],
        out_specs=[pl.BlockSpec((1, c_sz, HD), lambda bi, ci: (bi, ci, 0)),
                   pl.BlockSpec((1, H_A, DK_A, DV_A), lambda bi, ci: (bi, 0, 0, 0))],
        out_shape=[jax.ShapeDtypeStruct((b, l, HD), BF16), jax.ShapeDtypeStruct((b, H_A, DK_A, DV_A), F32)],
        scratch_shapes=[pltpu.VMEM((DK_A, HD), F32), pltpu.VMEM((c_sz, QKV_A), F32)],
        compiler_params=pltpu.CompilerParams(dimension_semantics=("parallel", "arbitrary"),
                                             vmem_limit_bytes=VMEM_LIMIT),
        name="gdn_prompt",
    )(proj, proj, proj, conv_w, *_gdn_params(a_log, dt_bias, gnorm))


def bias_tiles(rel_bias):
    d = jnp.arange(-(LANE - 1), N_BIAS_TILES * LANE, dtype=jnp.int32)
    tab = rel_bias[rel_bucket(d)].astype(F32).T
    tiles = []
    for dl in range(N_BIAS_TILES):
        w = jnp.pad(tab[:, dl * LANE:dl * LANE + 2 * LANE - 1], ((0, 0), (0, 1)))
        sh = jnp.tile(w, (1, LANE))[:, :LANE * (2 * LANE - 1)].reshape(-1, LANE, 2 * LANE - 1)
        tiles.append(sh[:, :, LANE - 1:])
    return jnp.stack(tiles, axis=1)


def _select_topk(sc, valid, krow, topk, idx_bits):
    key = jnp.where(valid, _sortable(sc), INT_MIN)
    kf = jnp.float32(topk)

    def count(m):
        return _col_reduce(jnp.where(m, 1.0, 0.0), jnp.sum)

    t0 = jnp.where(count(key >= 0) >= kf, 0, INT_MIN).astype(jnp.int32)

    def tbody(n, t):
        cand = t | jnp.left_shift(jnp.int32(1), 30 - n)
        return jnp.where(count(key >= cand) >= kf, cand, t)

    t = lax.fori_loop(0, 31, tbody, t0)
    gt = key > t
    eq = key == t
    m = kf - count(gt)

    def jbody(n, j):
        cand = j | jnp.left_shift(jnp.int32(1), idx_bits - 1 - n)
        return jnp.where(count(eq & (krow < cand)) <= m, cand, j)

    split = jnp.max(count(eq) - m) > 0.0
    j = lax.cond(split, lambda: lax.fori_loop(0, idx_bits, jbody, jnp.zeros_like(t)),
                 lambda: jnp.full_like(t, 1 << idx_bits))
    return valid & (gt | (eq & (krow < j)))


def _dsa_prompt_body(lc, pad, topk, q_ref, qi_ref, smq_ref, k_ref, v_ref, smk_ref, bias_ref, o_ref):
    i = pl.program_id(1)
    nkb = lc // LANE
    ki = smk_ref[0, :lc, SM_KI:SM_KI + D_IDX].astype(BF16)
    smt = smq_ref[0].T
    sc = jnp.zeros((lc, LANE), F32)
    for h in range(H_IDX):
        qih = qi_ref[0, :, h * D_IDX:(h + 1) * D_IDX].astype(BF16)
        d = lax.dot_general(ki, qih, NT_DIMS, preferred_element_type=F32)
        w = smt[SM_WI + h:SM_WI + h + 1, :] * (H_IDX ** -0.5 * D_IDX ** -0.5)
        sc = sc + jnp.maximum(d, 0.0) * w
    krow = lax.broadcasted_iota(jnp.int32, (lc, LANE), 0)
    qrow = i * LANE + lax.broadcasted_iota(jnp.int32, (lc, LANE), 1)
    valid = (krow <= qrow) & (krow >= pad)
    sel = _select_topk(sc, valid, krow, topk, max(1, (lc - 1).bit_length()))
    outs = []
    for h in range(H_B):
        sl = slice(h * DH_B, (h + 1) * DH_B)
        kh = k_ref[0, :lc, sl].astype(BF16)
        qh = (q_ref[0, :, sl] * (DH_B ** -0.5)).astype(BF16)
        lg = lax.dot_general(kh, qh, NT_DIMS, preferred_element_type=F32)
        bias = jnp.concatenate([bias_ref[h, jnp.clip(i - jb, 0, N_BIAS_TILES - 1)] for jb in range(nkb)], axis=0)
        lg = jnp.where(sel, lg + bias, -1e30)
        p = jnp.exp(lg - _col_reduce(lg, jnp.max))
        p = (p * (1.0 / _col_reduce(p, jnp.sum))).astype(BF16)
        vh = v_ref[0, :lc, sl].astype(BF16)
        outs.append(lax.dot_general(p, vh, TN_DIMS, preferred_element_type=F32))
    o_ref[0] = jnp.concatenate(outs, axis=1).astype(o_ref.dtype)


def _dsa_prompt_kernel(variants, pad, topk, *refs):
    _dispatch_variants(variants, _dsa_prompt_body, pad, topk, *refs)


def dsa_prompt(proj, btiles, pad, topk):
    b, l, _ = proj.shape
    nq = l // LANE
    qblk = lambda w, col: pl.BlockSpec((1, LANE, w), lambda bi, i: (bi, i, col // w))
    full = lambda w, col: pl.BlockSpec((1, l, w), lambda bi, i: (bi, 0, col // w))
    return pl.pallas_call(
        functools.partial(_dsa_prompt_kernel, _causal_variants(nq), pad, topk),
        grid=(b, nq),
        in_specs=[qblk(HD, COL_QB), qblk(HD, COL_QI), qblk(LANE, COL_SMALL),
                  full(HD, COL_KB), full(HD, COL_VB), full(LANE, COL_SMALL),
                  pl.BlockSpec(btiles.shape, lambda bi, i: (0, 0, 0, 0))],
        out_specs=pl.BlockSpec((1, LANE, HD), lambda bi, i: (bi, i, 0)),
        out_shape=jax.ShapeDtypeStruct((b, l, HD), BF16),
        compiler_params=pltpu.CompilerParams(dimension_semantics=("parallel", "arbitrary"),
                                             vmem_limit_bytes=VMEM_LIMIT),
        name="dsa_prompt",
    )(proj, proj, proj, proj, proj, proj, btiles)


def _mla_prep_kernel(h_ref, nm_ref, win_ref, qn_ref, kvn_ref, wuq_ref, wuk_ref, cos_ref, sin_ref, q_ref, kv_ref):
    x = _rms(h_ref[0], nm_ref[...]).astype(BF16)
    p = jnp.dot(x, win_ref[...], preferred_element_type=F32)
    cqn = _rms(p[:, CC_Q:CC_KV], qn_ref[...]).astype(BF16)
    qf = jnp.dot(cqn, wuq_ref[...], preferred_element_type=F32)
    cos, sin = cos_ref[...], sin_ref[...]
    q1, q2 = qf[:, H_C * LANE:(H_C + 1) * LANE], qf[:, (H_C + 1) * LANE:]
    q1r = q1 * cos - q2 * sin
    q2r = q2 * cos + q1 * sin
    head_of_lane = lax.broadcasted_iota(jnp.int32, (1, LANE), 1) // HALF
    for h in range(H_C):
        qlat = jnp.dot(qf[:, h * LANE:(h + 1) * LANE].astype(BF16), wuk_ref[h], preferred_element_type=F32)
        mine = head_of_lane == h
        q_ref[0, h] = jnp.concatenate([qlat, jnp.where(mine, q1r, 0.0), jnp.where(mine, q2r, 0.0)],
                                      axis=1).astype(q_ref.dtype)
    c = _rms(p[:, CC_KV:CC_K1], kvn_ref[...])
    k1, k2 = p[:, CC_K1:CC_K2], p[:, CC_K2:]
    kv_ref[0] = jnp.concatenate([c, k1 * cos - k2 * sin, k2 * cos + k1 * sin], axis=1)


def mla_prep(h, norm_w, w_in_p, qnorm, kvnorm, w_uq_p, w_uk_p, cos, sin):
    b, l, d = h.shape
    tm = min(LANE, l)
    const2 = lambda shp: pl.BlockSpec(shp, lambda bi, i: (0, 0))
    return pl.pallas_call(
        _mla_prep_kernel,
        grid=(b, l // tm),
        in_specs=[pl.BlockSpec((1, tm, d), lambda bi, i: (bi, i, 0)), const2((1, d)), const2(w_in_p.shape),
                  const2((1, Q_LORA)), const2((1, KV_LORA)), const2(w_uq_p.shape),
                  pl.BlockSpec(w_uk_p.shape, lambda bi, i: (0, 0, 0)),
                  pl.BlockSpec((tm, LANE), lambda bi, i: (i, 0)), pl.BlockSpec((tm, LANE), lambda bi, i: (i, 0))],
        out_specs=[pl.BlockSpec((1, H_C, tm, KVW), lambda bi, i: (bi, 0, i, 0)),
                   pl.BlockSpec((1, tm, KVW), lambda bi, i: (bi, i, 0))],
        out_shape=[jax.ShapeDtypeStruct((b, H_C, l, KVW), BF16), jax.ShapeDtypeStruct((b, l, KVW), F32)],
        compiler_params=pltpu.CompilerParams(dimension_semantics=("parallel", "arbitrary"),
                                             vmem_limit_bytes=VMEM_LIMIT),
        name="mla_prep",
    )(h, norm_w.reshape(1, d), w_in_p, qnorm.reshape(1, -1), kvnorm.reshape(1, -1), w_uq_p, w_uk_p, cos, sin)


def _mla_body(lc, pad, q_ref, kv_ref, o_ref):
    i = pl.program_id(1)
    q = q_ref[0].reshape(H_C * LANE, KVW)
    kb = kv_ref[0, :lc, :].astype(BF16)
    s = lax.dot_general(q, kb, NT_DIMS, preferred_element_type=F32) * MLA_SCALE
    s = s.reshape(H_C, LANE, lc)
    krow = lax.broadcasted_iota(jnp.int32, (LANE, lc), 1)
    qrow = i * LANE + lax.broadcasted_iota(jnp.int32, (LANE, lc), 0)
    ok = (krow <= qrow) & (krow >= pad)
    s = jnp.where(ok[None], s, -1e30)
    mx = jnp.max(s, axis=-1, keepdims=True)
    p = jnp.exp(s - mx)
    inv_l = 1.0 / jnp.sum(p, axis=-1, keepdims=True)
    o = jnp.dot(p.astype(BF16).reshape(H_C * LANE, lc), kb[:, :KV_LORA], preferred_element_type=F32)
    o = (o.reshape(H_C, LANE, KV_LORA) * inv_l).astype(o_ref.dtype)
    for h in range(H_C):
        o_ref[0, :, h * KV_LORA:(h + 1) * KV_LORA] = o[h]


def _mla_kernel(variants, pad, *refs):
    _dispatch_variants(variants, _mla_body, pad, *refs)


def mla_prompt(q, kv, pad):
    b, _, l, _ = q.shape
    nq = l // LANE
    return pl.pallas_call(
        functools.partial(_mla_kernel, _causal_variants(nq), pad),
        grid=(b, nq),
        in_specs=[pl.BlockSpec((1, H_C, LANE, KVW), lambda bi, i: (bi, 0, i, 0)),
                  pl.BlockSpec((1, l, KVW), lambda bi, i: (bi, 0, 0))],
        out_specs=pl.BlockSpec((1, LANE, H_C * KV_LORA), lambda bi, i: (bi, i, 0)),
        out_shape=jax.ShapeDtypeStruct((b, l, H_C * KV_LORA), BF16),
        compiler_params=pltpu.CompilerParams(dimension_semantics=("parallel", "arbitrary"),
                                             vmem_limit_bytes=VMEM_LIMIT),
        name="mla_prompt",
    )(q, kv)


def _gdn_decode_kernel(x_ref, conv_ref, small_ref, gate_ref, s_ref, convw_ref, alog_ref, dtb_ref, gnorm_ref,
                       o_ref, convo_ref, so_ref):
    nb = x_ref.shape[0]
    x = x_ref[...]
    cb = conv_ref[0]
    acc = x * convw_ref[CONV_W - 1:CONV_W, :]
    for j in range(CONV_W - 1):
        acc = acc + cb[:, j, :] * convw_ref[j:j + 1, :]
    for j in range(CONV_W - 2):
        convo_ref[:, j, :] = cb[:, j + 1, :]
    convo_ref[:, CONV_W - 2, :] = x
    y = acc * jax.nn.sigmoid(acc)
    q, k, v = y[:, :HD], y[:, HD:2 * HD], y[:, 2 * HD:]
    gmat = _head_group_matrix()
    q = q * lax.rsqrt(_dot_exact_rhs(q * q, gmat) + EPS) * (DK_A ** -0.5)
    k = k * lax.rsqrt(_dot_exact_rhs(k * k, gmat) + EPS)
    g, beta = _gdn_gates(small_ref[...], alog_ref, dtb_ref)
    eg = jnp.exp(g)
    rowid = lax.broadcasted_iota(jnp.int32, (nb, 1), 0)

    def body(bb, o_acc):
        mine = rowid == bb
        outs = []
        for h in range(H_A):
            sl = slice(h * DK_A, (h + 1) * DK_A)
            s = s_ref[0, bb, h]
            egh = eg[:, SM_ALPHA + h:SM_ALPHA + h + 1]
            eg_b = jnp.sum(jnp.where(mine, egh, 0.0), axis=0, keepdims=True)
            delta = (v[:, sl] - _hdot(k[:, sl], s) * egh) * beta[:, SM_BRAW + h:SM_BRAW + h + 1]
            s_new = s * eg_b + lax.dot_general(jnp.where(mine, k[:, sl], 0.0), jnp.where(mine, delta, 0.0), TN_DIMS,
                                               precision=HI, preferred_element_type=F32)
            so_ref[bb, h] = s_new
            outs.append(_hdot(q[:, sl], s_new))
        return jnp.where(mine, jnp.concatenate(outs, axis=1), o_acc)

    o = lax.fori_loop(0, nb, body, jnp.zeros((nb, HD), F32))
    ms_o = _dot_exact_rhs(o * o, gmat) * (1.0 / DV_A)
    gate = gate_ref[...]
    o_ref[...] = (o * lax.rsqrt(ms_o + EPS) * gnorm_ref[...] * (gate * jax.nn.sigmoid(gate))).astype(o_ref.dtype)


def gdn_decode_step(proj, conv_state, ssm_state, layer, conv_w, a_log, dt_bias, gnorm, nb=8):
    b = proj.shape[0]
    blk = lambda w, col: pl.BlockSpec((nb, w), lambda i: (i, col // w))
    const = lambda shp: pl.BlockSpec(shp, lambda i: (0, 0))
    return pl.pallas_call(
        _gdn_decode_kernel,
        grid=(b // nb,),
        in_specs=[blk(QKV_A, COL_QKV), pl.BlockSpec((1, nb, CONV_W - 1, QKV_A), lambda i: (layer, i, 0, 0)),
                  blk(LANE, COL_SMALL), blk(HD, COL_GATE),
                  pl.BlockSpec((1, nb, H_A, DK_A, DV_A), lambda i: (layer, i, 0, 0, 0)),
                  const((CONV_W, QKV_A)), const((1, LANE)), const((1, LANE)), const((1, HD))],
        out_specs=[pl.BlockSpec((nb, HD), lambda i: (i, 0)),
                   pl.BlockSpec((nb, CONV_W - 1, QKV_A), lambda i: (i, 0, 0)),
                   pl.BlockSpec((nb, H_A, DK_A, DV_A), lambda i: (i, 0, 0, 0))],
        out_shape=[jax.ShapeDtypeStruct((b, HD), BF16), jax.ShapeDtypeStruct(conv_state.shape[1:], F32),
                   jax.ShapeDtypeStruct(ssm_state.shape[1:], F32)],
        compiler_params=pltpu.CompilerParams(dimension_semantics=("parallel",), vmem_limit_bytes=VMEM_LIMIT),
        name="gdn_decode",
    )(proj, conv_state, proj, proj, ssm_state, conv_w, *_gdn_params(a_log, dt_bias, gnorm))


def _dsa_select_kernel(topk, n_pages, pt_ref, qi_ref, small_ref, *rest):
    ki_refs = rest[:DEC_PAGES]
    mask_ref, mnew_ref, sc_s = rest[DEC_PAGES:]
    j = pl.program_id(1)
    qi8 = _rows_to_heads(qi_ref[0], D_IDX).astype(BF16)
    sm = small_ref[0]
    pick = (lax.broadcasted_iota(jnp.int32, (H_IDX, LANE), 1)
            == lax.broadcasted_iota(jnp.int32, (H_IDX, LANE), 0) + SM_WI)
    w8 = jnp.sum(jnp.where(pick, sm, 0.0), axis=1, keepdims=True) * (H_IDX ** -0.5)
    kit = jnp.concatenate([r[0, 0] for r in ki_refs], axis=1).astype(BF16)
    d = jnp.dot(qi8, kit, preferred_element_type=F32)
    s = jnp.sum(jnp.maximum(d * (D_IDX ** -0.5), 0.0) * w8, axis=0, keepdims=True)
    for g in range(DEC_PAGES):
        sc_s[pl.ds(j * DEC_PAGES + g, 1), :] = s[:, g * PAGE_SIZE:(g + 1) * PAGE_SIZE]

    @pl.when(j == pl.num_programs(1) - 1)
    def _():
        past = n_pages * PAGE_SIZE
        ki_new = sm[:, SM_KI:SM_KI + D_IDX].astype(BF16).astype(F32)
        d_new = jnp.sum(qi8.astype(F32) * ki_new, axis=1, keepdims=True)
        s_new = jnp.sum(jnp.maximum(d_new * (D_IDX ** -0.5), 0.0) * w8, axis=0, keepdims=True)
        key = _sortable(sc_s[...])
        key_n = _sortable(s_new)
        idx = lax.broadcasted_iota(jnp.int32, key.shape, 0) * PAGE_SIZE + lax.broadcasted_iota(jnp.int32, key.shape, 1)
        kf = jnp.float32(topk)

        def count(m, mn):
            c = jnp.sum(jnp.sum(jnp.where(m, 1.0, 0.0), axis=0, keepdims=True), axis=1, keepdims=True)
            return c + jnp.where(mn, 1.0, 0.0)

        t0 = jnp.where(count(key >= 0, key_n >= 0) >= kf, 0, INT_MIN).astype(jnp.int32)

        def tbody(n, t):
            cand = t | jnp.left_shift(jnp.int32(1), 30 - n)
            return jnp.where(count(key >= cand, key_n >= cand) >= kf, cand, t)

        t = lax.fori_loop(0, 31, tbody, t0)
        m = kf - count(key > t, key_n > t)
        idx_bits = past.bit_length()

        def jbody(n, jj):
            cand = jj | jnp.left_shift(jnp.int32(1), idx_bits - 1 - n)
            return jnp.where(count((key == t) & (idx < cand), (key_n == t) & (past < cand)) <= m, cand, jj)

        jj = lax.fori_loop(0, idx_bits, jbody, jnp.zeros_like(t))
        mask_ref[0] = jnp.where((key > t) | ((key == t) & (idx < jj)), 1.0, 0.0)
        sel_n = (key_n > t) | ((key_n == t) & (past < jj))
        lane0 = lax.broadcasted_iota(jnp.int32, (1, LANE), 1) == 0
        mnew_ref[0] = jnp.where(lane0 & sel_n, 1.0, 0.0)


def dsa_decode_select(proj3, cache_kit, layer, page_table, topk):
    b = proj3.shape[0]
    n_pages = page_table.shape[1]
    page_spec = lambda g: pl.BlockSpec((1, 1, D_IDX, PAGE_SIZE),
                                       lambda bi, j, pt: (layer, pt[bi, j * DEC_PAGES + g], 0, 0))
    grid_spec = pltpu.PrefetchScalarGridSpec(
        num_scalar_prefetch=1,
        grid=(b, n_pages // DEC_PAGES),
        in_specs=[pl.BlockSpec((1, 1, HD), lambda bi, j, pt: (bi, 0, COL_QI // HD)),
                  pl.BlockSpec((1, 1, LANE), lambda bi, j, pt: (bi, 0, COL_SMALL // LANE))]
                 + [page_spec(g) for g in range(DEC_PAGES)],
        out_specs=[pl.BlockSpec((1, n_pages, PAGE_SIZE), lambda bi, j, pt: (bi, 0, 0)),
                   pl.BlockSpec((1, 1, LANE), lambda bi, j, pt: (bi, 0, 0))],
        scratch_shapes=[pltpu.VMEM((n_pages, PAGE_SIZE), F32)],
    )
    return pl.pallas_call(
        functools.partial(_dsa_select_kernel, topk, n_pages),
        grid_spec=grid_spec,
        out_shape=[jax.ShapeDtypeStruct((b, n_pages, PAGE_SIZE), F32), jax.ShapeDtypeStruct((b, 1, LANE), F32)],
        compiler_params=pltpu.CompilerParams(dimension_semantics=("parallel", "arbitrary"),
                                             vmem_limit_bytes=VMEM_LIMIT),
        name="dsa_decode_select",
    )(page_table, proj3, proj3, *([cache_kit] * DEC_PAGES))


def _dsa_attend_kernel(n_pages, pt_ref, thr_ref, q_ref, knt_ref, vnt_ref, mask_ref, mnew_ref, rb_ref, *rest):
    kt_refs, vt_refs = rest[:DEC_PAGES], rest[DEC_PAGES:2 * DEC_PAGES]
    o_ref, qbd_s, m_s, l_s, acc_s = rest[2 * DEC_PAGES:]
    j = pl.program_id(1)
    past = n_pages * PAGE_SIZE
    eye = lax.broadcasted_iota(jnp.int32, (H_B, LANE), 0) == lax.broadcasted_iota(jnp.int32, (H_B, LANE), 1)

    @pl.when(j == 0)
    def _():
        bd = lax.broadcasted_iota(jnp.int32, (H_B, HD), 1) // DH_B == lax.broadcasted_iota(jnp.int32, (H_B, HD), 0)
        qbd_s[...] = jnp.where(bd, jnp.broadcast_to(q_ref[0], (H_B, HD)), 0.0).astype(BF16)
        m_s[...] = jnp.full_like(m_s, -1e30)
        l_s[...] = jnp.zeros_like(l_s)
        acc_s[...] = jnp.zeros_like(acc_s)

    def attend(kts, vts, ok, first_key):
        n = len(kts)
        lg = jnp.concatenate([jnp.dot(qbd_s[...], kt.astype(BF16), preferred_element_type=F32) for kt in kts], axis=1)
        dist = past - first_key - lax.broadcasted_iota(jnp.int32, (1, n * PAGE_SIZE), 1)
        bias = jnp.broadcast_to(rb_ref[0], lg.shape)
        for bk in range(1, REL_BUCKETS):
            bias = jnp.where(dist >= thr_ref[bk], rb_ref[bk], bias)
        lg = jnp.where(ok, lg * (DH_B ** -0.5) + bias, -1e30)
        m_new = jnp.maximum(m_s[...], jnp.max(lg, axis=1, keepdims=True))
        alpha = jnp.exp(m_s[...] - m_new)
        p = jnp.where(ok, jnp.exp(lg - m_new), 0.0)
        l_s[...] = l_s[...] * alpha + jnp.sum(p, axis=1, keepdims=True)
        m_s[...] = m_new
        pb = p.astype(BF16)
        pv = [lax.dot_general(vts[g].astype(BF16), pb[:, g * PAGE_SIZE:(g + 1) * PAGE_SIZE], NT_DIMS,
                              preferred_element_type=F32) for g in range(n)]
        while len(pv) > 1:
            pv = [pv[i] + pv[i + 1] if i + 1 < len(pv) else pv[i] for i in range(0, len(pv), 2)]
        alpha_row = jnp.sum(jnp.where(eye, alpha, 0.0), axis=0, keepdims=True)
        acc_s[...] = acc_s[...] * alpha_row[:, :H_B] + pv[0]

    rows = mask_ref[0, pl.ds(j * DEC_PAGES, DEC_PAGES), :]
    ok = jnp.concatenate([rows[g:g + 1] for g in range(DEC_PAGES)], axis=1) > 0.0
    attend([r[0, 0].reshape(HD, PAGE_SIZE) for r in kt_refs], [r[0, 0].reshape(HD, PAGE_SIZE) for r in vt_refs],
           ok, j * (DEC_PAGES * PAGE_SIZE))

    @pl.when(j == pl.num_programs(1) - 1)
    def _():
        attend([knt_ref[0]], [vnt_ref[0]], mnew_ref[0] > 0.0, past)
        l_row = jnp.sum(jnp.where(eye, l_s[...], 0.0), axis=0, keepdims=True)
        o = acc_s[...] * (1.0 / l_row[:, :H_B])
        own = (lax.broadcasted_iota(jnp.int32, (HD, H_B), 0) // DH_B
               == lax.broadcasted_iota(jnp.int32, (HD, H_B), 1))
        o_ref[0] = jnp.sum(jnp.where(own, o, 0.0), axis=1, keepdims=True).astype(o_ref.dtype)


def _decode_thresholds(past):
    buckets = rel_bucket(jnp.arange(past + 1, dtype=jnp.int32))
    return jnp.sum(buckets[None, :] < jnp.arange(REL_BUCKETS, dtype=jnp.int32)[:, None], axis=1).astype(jnp.int32)


def dsa_decode_attend(proj, mask, mnew, cache_kt, cache_vt, layer, page_table, rel_bias):
    b = proj.shape[0]
    n_pages = page_table.shape[1]
    past = n_pages * PAGE_SIZE
    thr = _decode_thresholds(past)
    rb = rel_bias.astype(F32)[:, :, None]
    proj3 = proj.reshape(b, 1, AB_COLS_P)
    one_key_page = lambda col: jnp.pad(proj[:, col:col + HD, None], ((0, 0), (0, 0), (0, PAGE_SIZE - 1)))
    page_spec = lambda g: pl.BlockSpec((1, 1, H_B, DH_B, PAGE_SIZE),
                                       lambda bi, j, pt, th: (layer, pt[bi, j * DEC_PAGES + g], 0, 0, 0))
    per_b = lambda shp: pl.BlockSpec((1,) + shp, lambda bi, j, pt, th: (bi, 0, 0))
    grid_spec = pltpu.PrefetchScalarGridSpec(
        num_scalar_prefetch=2,
        grid=(b, n_pages // DEC_PAGES),
        in_specs=[pl.BlockSpec((1, 1, HD), lambda bi, j, pt, th: (bi, 0, COL_QB // HD)),
                  per_b((HD, PAGE_SIZE)), per_b((HD, PAGE_SIZE)), per_b((n_pages, PAGE_SIZE)), per_b((1, LANE)),
                  pl.BlockSpec(rb.shape, lambda bi, j, pt, th: (0, 0, 0))]
                 + [page_spec(g) for g in range(DEC_PAGES)] * 2,
        out_specs=per_b((HD, 1)),
        scratch_shapes=[pltpu.VMEM((H_B, HD), BF16), pltpu.VMEM((H_B, 1), F32), pltpu.VMEM((H_B, 1), F32),
                        pltpu.VMEM((HD, H_B), F32)],
    )
    return pl.pallas_call(
        functools.partial(_dsa_attend_kernel, n_pages),
        grid_spec=grid_spec,
        out_shape=jax.ShapeDtypeStruct((b, HD, 1), F32),
        compiler_params=pltpu.CompilerParams(dimension_semantics=("parallel", "arbitrary"),
                                             vmem_limit_bytes=VMEM_LIMIT),
        name="dsa_decode_attend",
    )(page_table, thr, proj3, one_key_page(COL_KB), one_key_page(COL_VB), mask, mnew, rb,
      *([cache_kt] * DEC_PAGES), *([cache_vt] * DEC_PAGES))


def _online_update(lg, m_s, l_s):
    m_new = jnp.maximum(m_s[...], jnp.max(lg, axis=1, keepdims=True))
    alpha = jnp.exp(m_s[...] - m_new)
    p = jnp.exp(lg - m_new)
    l_s[...] = l_s[...] * alpha + jnp.sum(p, axis=1, keepdims=True)
    m_s[...] = m_new
    return p, alpha


def _mla_decode_kernel(pt_ref, q_ref, kvn_ref, exp_ref, *rest):
    c_refs, kpe_refs = rest[:MLA_DEC_PAGES], rest[MLA_DEC_PAGES:2 * MLA_DEC_PAGES]
    o_ref, m_s, l_s, acc_s = rest[2 * MLA_DEC_PAGES:]
    j = pl.program_id(1)

    @pl.when(j == 0)
    def _():
        m_s[...] = jnp.full_like(m_s, -1e30)
        l_s[...] = jnp.zeros_like(l_s)
        acc_s[...] = jnp.zeros_like(acc_s)

    q = q_ref[0]
    q_lat = q[:, :KV_LORA]
    q_pe = jnp.dot(q[:, KV_LORA:], exp_ref[...], preferred_element_type=F32).astype(BF16)
    cb = jnp.concatenate([r[0, 0] for r in c_refs], axis=0).astype(BF16)
    kpet = jnp.concatenate([r[0, 0] for r in kpe_refs], axis=1).astype(BF16)
    lg = (lax.dot_general(q_lat, cb, NT_DIMS, preferred_element_type=F32)
          + jnp.dot(q_pe, kpet, preferred_element_type=F32)) * MLA_SCALE
    p, alpha = _online_update(lg, m_s, l_s)
    acc_s[...] = acc_s[...] * alpha + jnp.dot(p.astype(BF16), cb, preferred_element_type=F32)

    @pl.when(j == pl.num_programs(1) - 1)
    def _():
        kvn = kvn_ref[0].astype(BF16).astype(F32)
        lgn = jnp.sum(q.astype(F32) * kvn, axis=1, keepdims=True) * MLA_SCALE
        pn, alpha_n = _online_update(lgn, m_s, l_s)
        acc = acc_s[...] * alpha_n + pn.astype(BF16).astype(F32) * kvn[:, :KV_LORA]
        o = acc * (1.0 / l_s[...])
        o_ref[0] = jnp.concatenate([o[h:h + 1] for h in range(H_C)], axis=1).astype(o_ref.dtype)


def mla_decode(q, kvn, cache_c, cache_kpet, layer, page_table):
    b = q.shape[0]
    n_pages = page_table.shape[1]
    src = jnp.concatenate([jnp.tile(jnp.arange(HALF), H_C), HALF + jnp.tile(jnp.arange(HALF), H_C)])
    expand = (src[:, None] == jnp.arange(QK_ROPE)[None, :]).astype(BF16)
    cspec = lambda g: pl.BlockSpec((1, 1, PAGE_SIZE, KV_LORA),
                                   lambda bi, j, pt: (layer, pt[bi, j * MLA_DEC_PAGES + g], 0, 0))
    kspec = lambda g: pl.BlockSpec((1, 1, QK_ROPE, PAGE_SIZE),
                                   lambda bi, j, pt: (layer, pt[bi, j * MLA_DEC_PAGES + g], 0, 0))
    grid_spec = pltpu.PrefetchScalarGridSpec(
        num_scalar_prefetch=1,
        grid=(b, n_pages // MLA_DEC_PAGES),
        in_specs=[pl.BlockSpec((1, H_C, KVW), lambda bi, j, pt: (bi, 0, 0)),
                  pl.BlockSpec((1, 1, KVW), lambda bi, j, pt: (bi, 0, 0)),
                  pl.BlockSpec(expand.shape, lambda bi, j, pt: (0, 0))]
                 + [cspec(g) for g in range(MLA_DEC_PAGES)] + [kspec(g) for g in range(MLA_DEC_PAGES)],
        out_specs=pl.BlockSpec((1, 1, H_C * KV_LORA), lambda bi, j, pt: (bi, 0, 0)),
        scratch_shapes=[pltpu.VMEM((H_C, 1), F32), pltpu.VMEM((H_C, 1), F32), pltpu.VMEM((H_C, KV_LORA), F32)],
    )
    return pl.pallas_call(
        _mla_decode_kernel,
        grid_spec=grid_spec,
        out_shape=jax.ShapeDtypeStruct((b, 1, H_C * KV_LORA), BF16),
        compiler_params=pltpu.CompilerParams(dimension_semantics=("parallel", "arbitrary"),
                                             vmem_limit_bytes=VMEM_LIMIT),
        name="mla_decode",
    )(page_table, q, kvn, expand, *([cache_c] * MLA_DEC_PAGES), *([cache_kpet] * MLA_DEC_PAGES))


def _prep_w_ab(w):
    qkv_a, alpha, braw, gate, q_b, k_b, v_b, qi, wi, ki = split_cols(w, AB_SIZES)
    z = jnp.zeros((w.shape[0], SM_KI - SM_WI - H_IDX), w.dtype)
    return jnp.concatenate([qkv_a, gate, q_b, k_b, v_b, qi, alpha, braw, wi, z, ki], axis=1).astype(BF16)


def _prep_w_c(w_in, w_uq, w_uk):
    cq, ckv, kpe = split_cols(w_in, C_SIZES)
    w_in_p = jnp.concatenate([cq, ckv, jnp.tile(kpe[:, :HALF], (1, H_C)), jnp.tile(kpe[:, HALF:], (1, H_C))], axis=1)
    wq = w_uq.reshape(Q_LORA, H_C, QK_NOPE + QK_ROPE)
    nope = jnp.pad(wq[:, :, :QK_NOPE], ((0, 0), (0, 0), (0, LANE - QK_NOPE))).reshape(Q_LORA, H_C * LANE)
    r1 = wq[:, :, QK_NOPE:QK_NOPE + HALF].reshape(Q_LORA, H_C * HALF)
    r2 = wq[:, :, QK_NOPE + HALF:].reshape(Q_LORA, H_C * HALF)
    w_uq_p = jnp.concatenate([nope, r1, r2], axis=1)
    w_uk_p = jnp.pad(jnp.transpose(w_uk, (1, 2, 0)), ((0, 0), (0, LANE - QK_NOPE), (0, 0)))
    return w_in_p.astype(BF16), w_uq_p.astype(BF16), w_uk_p.astype(BF16)


def _rope_tables(pos):
    inv = ROPE_BASE ** (-jnp.arange(HALF, dtype=jnp.float32) / HALF)
    ang = pos.astype(jnp.float32)[:, None] * inv[None, :]
    return jnp.tile(jnp.cos(ang), (1, LANE // HALF)), jnp.tile(jnp.sin(ang), (1, LANE // HALF))


def _kpe_from_rows(kv):
    return jnp.concatenate([kv[..., KV_LORA:KV_LORA + HALF], kv[..., KV_LORA + LANE:KV_LORA + LANE + HALF]], axis=-1)


def kernel(x_prompt, x_sample, state_a_conv, state_a_ssm, cache_b_k, cache_b_v, cache_b_kidx,
           cache_c_latent, cache_c_kpe, page_table, meta_tokens, rel_bias, norm_mix, norm_ffn, norm_final,
           w_in_ab, conv_a, a_log, dt_bias_a, gnorm_a, w_out_ab, w_in_c, qnorm_c, kvnorm_c,
           w_uq_c, w_uk_c, w_uv_c, w_out_c, w_up, w_down):
    w_up_b = w_up.astype(BF16)
    w_down_b = w_down.astype(BF16)
    w_out_ab_b = w_out_ab.astype(BF16)
    w_out_c_b = w_out_c.astype(BF16)
    w_uv_b = jnp.transpose(w_uv_c, (0, 2, 1, 3)).astype(BF16)
    w_in_ab_p = [_prep_w_ab(w_in_ab[j]) for j in range(w_in_ab.shape[0])]
    w_c_p = [_prep_w_c(w_in_c[j], w_uq_c[j], w_uk_c[j]) for j in range(w_in_c.shape[0])]

    b, seq, d = x_prompt.shape
    lp = N_META + seq
    pad = (-lp) % LANE
    l = pad + lp
    n = b * l
    topk = min(TOPK_MAX, SEQ // 4)
    meta = jnp.broadcast_to(meta_tokens[None].astype(x_prompt.dtype), (b, N_META, d))
    h = jnp.concatenate([jnp.zeros((b, pad, d), x_prompt.dtype), meta, x_prompt], axis=1).reshape(n, d)
    cos_p, sin_p = _rope_tables(jnp.arange(l, dtype=jnp.int32) - pad)
    a_conv_p, a_ssm_p, b_k_p, b_v_p, b_kidx_p, c_lat_p, c_kpe_p = [], [], [], [], [], [], []
    for li in range(DEPTH):
        j = li // 2
        last = li == DEPTH - 1
        if li % 2 == 0:
            proj = rms_linear(h, norm_mix[li], w_in_ab_p[j]).reshape(b, l, AB_COLS_P)
            o_a, s_fin = gdn_prompt(proj, conv_a[j], a_log[j], dt_bias_a[j], gnorm_a[j], pad)
            o_b = dsa_prompt(proj, bias_tiles(rel_bias), pad, topk)
            projs = [(o_a.reshape(n, HD), w_out_ab_b[j, :HD]), (o_b.reshape(n, HD), w_out_ab_b[j, HD:])]
            head_w = None
            a_conv_p.append(proj[:, l - (CONV_W - 1):, COL_QKV:COL_QKV + QKV_A])
            a_ssm_p.append(s_fin)
            b_k_p.append(proj[:, pad:, COL_KB:COL_KB + HD].reshape(b, lp, H_B, DH_B))
            b_v_p.append(proj[:, pad:, COL_VB:COL_VB + HD].reshape(b, lp, H_B, DH_B))
            b_kidx_p.append(proj[:, pad:, COL_SMALL + SM_KI:COL_SMALL + SM_KI + D_IDX])
        else:
            q_all, kv = mla_prep(h.reshape(b, l, d), norm_mix[li], *w_c_p[j][:1], qnorm_c[j], kvnorm_c[j],
                                 *w_c_p[j][1:], cos_p, sin_p)
            o_lat = mla_prompt(q_all, kv, pad)
            projs = [(o_lat.reshape(n, H_C * KV_LORA), w_out_c_b[j])]
            head_w = w_uv_b[j]
            c_lat_p.append(kv[:, pad:, :KV_LORA])
            c_kpe_p.append(_kpe_from_rows(kv[:, pad:]))
        h = proj_mlp(h, projs, norm_ffn[li], w_up_b[li], w_down_b[li], norm_final if last else None, head_w)
    y_p = h.reshape(b, l, d)[:, pad + N_META:]

    bs = x_sample.shape[0]
    n_pages = page_table.shape[1]
    past = n_pages * PAGE_SIZE
    topk_s = min(TOPK_MAX, (past + DEC_SEQ) // 4)
    hs = x_sample.reshape(bs, d)
    cos_s, sin_s = _rope_tables(jnp.full((bs,), past, jnp.int32))
    cache_kit = jnp.transpose(cache_b_kidx, (0, 1, 3, 2))
    cache_kt = jnp.transpose(cache_b_k, (0, 1, 3, 4, 2))
    cache_vt = jnp.transpose(cache_b_v, (0, 1, 3, 4, 2))
    cache_kpet = jnp.transpose(cache_c_kpe, (0, 1, 3, 2))
    a_conv_s, a_ssm_s, b_k_s, b_v_s, b_kidx_s, c_lat_s, c_kpe_s = [], [], [], [], [], [], []
    for li in range(DEPTH):
        j = li // 2
        last = li == DEPTH - 1
        if li % 2 == 0:
            proj = rms_linear(hs, norm_mix[li], w_in_ab_p[j])
            proj3 = proj.reshape(bs, 1, AB_COLS_P)
            o_a, conv_new, s_new = gdn_decode_step(proj, state_a_conv, state_a_ssm, j, conv_a[j], a_log[j],
                                                   dt_bias_a[j], gnorm_a[j])
            mask, mnew = dsa_decode_select(proj3, cache_kit, j, page_table, topk_s)
            o_b = dsa_decode_attend(proj, mask, mnew, cache_kt, cache_vt, j, page_table, rel_bias)
            projs = [(o_a, w_out_ab_b[j, :HD]), (o_b.reshape(bs, HD), w_out_ab_b[j, HD:])]
            head_w = None
            a_conv_s.append(conv_new)
            a_ssm_s.append(s_new)
            b_k_s.append(proj[:, COL_KB:COL_KB + HD].reshape(bs, DEC_SEQ, H_B, DH_B))
            b_v_s.append(proj[:, COL_VB:COL_VB + HD].reshape(bs, DEC_SEQ, H_B, DH_B))
            b_kidx_s.append(proj[:, COL_SMALL + SM_KI:COL_SMALL + SM_KI + D_IDX].reshape(bs, DEC_SEQ, D_IDX))
        else:
            q_all, kv = mla_prep(hs.reshape(1, bs, d), norm_mix[li], *w_c_p[j][:1], qnorm_c[j], kvnorm_c[j],
                                 *w_c_p[j][1:], cos_s, sin_s)
            o_lat = mla_decode(jnp.transpose(q_all[0], (1, 0, 2)), kv.reshape(bs, 1, KVW), cache_c_latent,
                               cache_kpet, j, page_table)
            projs = [(o_lat.reshape(bs, H_C * KV_LORA), w_out_c_b[j])]
            head_w = w_uv_b[j]
            c_lat_s.append(kv[0, :, :KV_LORA].reshape(bs, DEC_SEQ, KV_LORA))
            c_kpe_s.append(_kpe_from_rows(kv[0]).reshape(bs, DEC_SEQ, QK_ROPE))
        hs = proj_mlp(hs, projs, norm_ffn[li], w_up_b[li], w_down_b[li], norm_final if last else None, head_w)
    y_s = hs.reshape(bs, DEC_SEQ, d)

    stack = jnp.stack
    return (y_p, y_s,
            stack(a_conv_p), stack(a_ssm_p), stack(b_k_p), stack(b_v_p), stack(b_kidx_p), stack(c_lat_p), stack(c_kpe_p),
            stack(a_conv_s), stack(a_ssm_s), stack(b_k_s), stack(b_v_s), stack(b_kidx_s), stack(c_lat_s), stack(c_kpe_s))
```

```python
import functools
import math
import jax
import jax.numpy as jnp
from jax import lax
from jax.experimental import pallas as pl
from jax.experimental.pallas import tpu as pltpu

D_MODEL = 1024
BATCH = 8
SEQ = 2048
DEPTH = 2
DEC_BATCH = 32
DEC_SEQ = 1
PAST_LEN = 16384
PAGE_SIZE = 128

N_META = 16
EPS = 1e-6
H_A = 8
DK_A = 64
DV_A = 64
CONV_W = 4
H_B = 8
DH_B = 64
H_IDX = 8
D_IDX = 64
TOPK_MAX = 256
REL_BUCKETS = 32
REL_MAX_DIST = 1024
H_C = 8
Q_LORA = 384
KV_LORA = 256
QK_NOPE = 64
QK_ROPE = 32
V_C = 128
ROPE_BASE = 10000.0
D_FF = 4 * D_MODEL

QKV_A = H_A * (2 * DK_A + DV_A)
AB_SIZES = (QKV_A, H_A, H_A, H_A * DV_A, H_B * DH_B, H_B * DH_B, H_B * DH_B, H_IDX * D_IDX, H_IDX, D_IDX)
C_SIZES = (Q_LORA, KV_LORA, QK_ROPE)
MLA_SCALE = (QK_NOPE + QK_ROPE) ** -0.5

BF16 = jnp.bfloat16
F32 = jnp.float32
LANE = 128
VMEM_LIMIT = 56 * 1024 * 1024
INT_MIN = -(2 ** 31)
NT_DIMS = (((1,), (1,)), ((), ()))
TN_DIMS = (((0,), (0,)), ((), ()))
HI = lax.Precision.HIGHEST
HD = H_A * DK_A
HALF = QK_ROPE // 2

COL_QKV, COL_GATE, COL_QB, COL_KB, COL_VB, COL_QI, COL_SMALL = 0, 1536, 2048, 2560, 3072, 3584, 4096
AB_COLS_P = COL_SMALL + LANE
SM_ALPHA, SM_BRAW, SM_WI, SM_KI = 0, 8, 16, 64
CC_Q, CC_KV, CC_K1, CC_K2, C_COLS_P = 0, 384, 640, 768, 896
KVW = KV_LORA + 2 * LANE
N_BIAS_TILES = 9
DEC_PAGES = 16
MLA_DEC_PAGES = 32
SEL_DEC_PAGES = 32


def split_cols(x, sizes):
    out, off = [], 0
    for s in sizes:
        out.append(x[..., off:off + s])
        off += s
    return out


def rel_bucket(dist):
    dist = jnp.maximum(dist, 0)
    max_exact = REL_BUCKETS // 2
    log_ratio = jnp.log(jnp.maximum(dist, 1).astype(jnp.float32) / max_exact) / math.log(REL_MAX_DIST / max_exact)
    large = max_exact + (log_ratio * (REL_BUCKETS - max_exact)).astype(jnp.int32)
    return jnp.where(dist < max_exact, dist, jnp.minimum(large, REL_BUCKETS - 1))


def _rms(x, w):
    return x * lax.rsqrt(jnp.mean(x * x, axis=-1, keepdims=True) + EPS) * w


def _hdot(a, b):
    return jnp.dot(a, b, precision=HI, preferred_element_type=F32)


def _dot_exact_rhs(a, b_bf16):
    a_hi = a.astype(BF16)
    a_lo = (a - a_hi.astype(F32)).astype(BF16)
    return jnp.dot(a_hi, b_bf16, preferred_element_type=F32) + jnp.dot(a_lo, b_bf16, preferred_element_type=F32)


def _split2(a):
    hi = a.astype(BF16)
    return hi, (a - hi.astype(F32)).astype(BF16)


def _dot3(ap, bp):
    (ah, al), (bh, bl) = ap, bp
    return (jnp.dot(ah, bh, preferred_element_type=F32) + jnp.dot(al, bh, preferred_element_type=F32)
            + jnp.dot(ah, bl, preferred_element_type=F32))


def _sortable(x):
    bits = pltpu.bitcast(x + 0.0, jnp.int32)
    return bits ^ ((bits >> 31) & 0x7FFFFFFF)


def _rows_to_heads(row, width):
    return jnp.concatenate([row[:, h * width:(h + 1) * width] for h in range(row.shape[1] // width)], axis=0)


def _col_reduce(x, op, groups=8):
    l, w = x.shape
    if l % (8 * groups) == 0:
        x = op(op(x.reshape(groups, l // (8 * groups), 8, w), axis=1), axis=0)
    return op(x, axis=0, keepdims=True)


def _causal_variants(nq, max_variants):
    n_var = min(max_variants, nq)
    his = [nq - (n_var - 1 - v) * (nq // n_var) for v in range(n_var)]
    return [(hi, hi * LANE) for hi in his]


def _dispatch_variants(variants, body, *args):
    i = pl.program_id(1)
    lo = 0
    for hi, lc in variants:
        pl.when((i >= lo) & (i < hi))(functools.partial(body, lc, *args))
        lo = hi


def _rms_linear_kernel(x_ref, nw_ref, w_ref, o_ref):
    xn = _rms(x_ref[...], nw_ref[...]).astype(BF16)
    o_ref[...] = jnp.dot(xn, w_ref[...], preferred_element_type=F32)


def rms_linear(x, norm_w, w, tm=256):
    n, d = x.shape
    m = w.shape[1]
    tm = min(tm, n)
    return pl.pallas_call(
        _rms_linear_kernel,
        grid=(pl.cdiv(n, tm),),
        in_specs=[pl.BlockSpec((tm, d), lambda i: (i, 0)), pl.BlockSpec((1, d), lambda i: (0, 0)),
                  pl.BlockSpec((d, m), lambda i: (0, 0))],
        out_specs=pl.BlockSpec((tm, m), lambda i: (i, 0)),
        out_shape=jax.ShapeDtypeStruct((n, m), F32),
        compiler_params=pltpu.CompilerParams(dimension_semantics=("parallel",), vmem_limit_bytes=VMEM_LIMIT),
        name="rms_linear",
    )(x, norm_w.reshape(1, d), w)


def _proj_mlp_kernel(n_proj, n_heads, final, *refs):
    refs = list(refs)
    h_ref = refs.pop(0)
    a_refs = [refs.pop(0) for _ in range(n_proj)]
    w_refs = [refs.pop(0) for _ in range(n_proj)]
    hw_ref = refs.pop(0) if n_heads else None
    nw_ref, wup_ref, wdn_ref = refs.pop(0), refs.pop(0), refs.pop(0)
    nf_ref = refs.pop(0) if final else None
    o_ref, h1_s, xn_s, acc_s = refs
    f = pl.program_id(1)

    @pl.when(f == 0)
    def _():
        h1 = h_ref[...]
        for a_ref, w_ref in zip(a_refs, w_refs):
            a = a_ref[...]
            if n_heads:
                kh = a.shape[1] // n_heads
                a = jnp.concatenate([jnp.dot(a[:, hh * kh:(hh + 1) * kh].astype(BF16), hw_ref[hh],
                                             preferred_element_type=F32) for hh in range(n_heads)], axis=1)
            h1 = h1 + jnp.dot(a.astype(BF16), w_ref[...], preferred_element_type=F32)
        h1_s[...] = h1
        xn_s[...] = _rms(h1, nw_ref[...]).astype(BF16)
        acc_s[...] = jnp.zeros_like(acc_s)

    u = jnp.dot(xn_s[...], wup_ref[...], preferred_element_type=F32)
    u = jnp.square(jnp.maximum(u, 0.0)).astype(BF16)
    acc_s[...] += jnp.dot(u, wdn_ref[...], preferred_element_type=F32)

    @pl.when(f == pl.num_programs(1) - 1)
    def _():
        h2 = h1_s[...] + acc_s[...]
        if final:
            h2 = _rms(h2, nf_ref[...])
        o_ref[...] = h2


def proj_mlp(h, projs, norm_w, w_up, w_down, norm_final=None, head_w=None, tm=1024, tf=1024):
    n, d = h.shape
    ff = w_up.shape[1]
    tm = min(tm, n)
    final = norm_final is not None
    n_heads = 0 if head_w is None else head_w.shape[0]
    row = lambda i, f: (i, 0)
    const = lambda i, f: (0, 0)
    in_specs = [pl.BlockSpec((tm, d), row)]
    in_specs += [pl.BlockSpec((tm, a.shape[1]), row) for a, _ in projs]
    in_specs += [pl.BlockSpec(w.shape, const) for _, w in projs]
    args = [h] + [a for a, _ in projs] + [w for _, w in projs]
    if n_heads:
        in_specs.append(pl.BlockSpec(head_w.shape, lambda i, f: (0, 0, 0)))
        args.append(head_w)
    in_specs += [pl.BlockSpec((1, d), const), pl.BlockSpec((d, tf), lambda i, f: (0, f)),
                 pl.BlockSpec((tf, d), lambda i, f: (f, 0))]
    args += [norm_w.reshape(1, d), w_up, w_down]
    if final:
        in_specs.append(pl.BlockSpec((1, d), const))
        args.append(norm_final.reshape(1, d))
    return pl.pallas_call(
        functools.partial(_proj_mlp_kernel, len(projs), n_heads, final),
        grid=(pl.cdiv(n, tm), ff // tf),
        in_specs=in_specs,
        out_specs=pl.BlockSpec((tm, d), row),
        out_shape=jax.ShapeDtypeStruct((n, d), F32),
        scratch_shapes=[pltpu.VMEM((tm, d), F32), pltpu.VMEM((tm, d), BF16), pltpu.VMEM((tm, d), F32)],
        compiler_params=pltpu.CompilerParams(dimension_semantics=("parallel", "arbitrary"),
                                             vmem_limit_bytes=VMEM_LIMIT),
        name="proj_mlp",
    )(*args)


def _gdn_gates(sm, alog_ref, dtb_ref):
    zs = sm + dtb_ref[...]
    g = -jnp.exp(alog_ref[...]) * (jnp.maximum(zs, 0.0) + jnp.log(1.0 + jnp.exp(-jnp.abs(zs))))
    return g, jax.nn.sigmoid(sm)


def _head_group_matrix():
    gi = lax.broadcasted_iota(jnp.int32, (HD, HD), 0) // DK_A
    gj = lax.broadcasted_iota(jnp.int32, (HD, HD), 1) // DK_A
    return jnp.where(gi == gj, 1.0, 0.0).astype(BF16)


def _gdn_kernel(c_sz, pad, qkv_ref, small_ref, gate_ref, convw_ref, alog_ref, dtb_ref, gnorm_ref,
                o_ref, sfin_ref, s_s, prev_s):
    c = pl.program_id(1)

    @pl.when(c == 0)
    def _():
        s_s[...] = jnp.zeros_like(s_s)
        prev_s[...] = jnp.zeros_like(prev_s)

    row = lax.broadcasted_iota(jnp.int32, (c_sz, 1), 0)
    valid = (row + c * c_sz) >= pad
    x = jnp.where(valid, qkv_ref[0], 0.0)
    prevx = prev_s[...]
    acc = x * convw_ref[CONV_W - 1:CONV_W, :]
    for sft in range(1, CONV_W):
        shifted = jnp.where(row < sft, pltpu.roll(prevx, sft, 0), pltpu.roll(x, sft, 0))
        acc = acc + shifted * convw_ref[CONV_W - 1 - sft:CONV_W - sft, :]
    prev_s[...] = x
    y = acc * jax.nn.sigmoid(acc)
    q, k, v = y[:, :HD], y[:, HD:2 * HD], y[:, 2 * HD:]
    gmat = _head_group_matrix()
    ss = jnp.dot(jnp.concatenate([q * q, k * k], axis=0).astype(BF16), gmat, preferred_element_type=F32)
    q = q * lax.rsqrt(ss[:c_sz] + EPS) * (DK_A ** -0.5)
    k = k * lax.rsqrt(ss[c_sz:] + EPS)

    g, beta = _gdn_gates(small_ref[0], alog_ref, dtb_ref)
    g = jnp.where(valid, g, 0.0)
    beta = jnp.where(valid, beta, 0.0)
    ti = lax.broadcasted_iota(jnp.int32, (c_sz, c_sz), 0)
    tj = lax.broadcasted_iota(jnp.int32, (c_sz, c_sz), 1)
    incl = ti >= tj
    strict = ti > tj
    gc = _hdot(jnp.where(incl, 1.0, 0.0), g)
    gct = gc.T
    ei = lax.broadcasted_iota(jnp.int32, (LANE, HD), 0)
    ej = lax.broadcasted_iota(jnp.int32, (LANE, HD), 1) // DK_A
    gc_w = _hdot(gc, jnp.where(ei == ej + SM_ALPHA, 1.0, 0.0))
    beta_w = _dot_exact_rhs(beta, jnp.where(ei == ej + SM_BRAW, 1.0, 0.0).astype(BF16))
    gl_w = gc_w[c_sz - 1:c_sz, :]
    eg_w = jnp.exp(gc_w)
    q_s = q * eg_w
    k_b = k * (beta_w * eg_w)
    v_b = v * beta_w
    k_l = k * jnp.exp(gl_w - gc_w)
    s_old = s_s[...]
    s_old_b = s_old.astype(BF16)
    n_fac = max(1, (c_sz - 1).bit_length())
    sls = [slice(h * DK_A, (h + 1) * DK_A) for h in range(H_A)]
    k16, q16 = k.astype(BF16), q.astype(BF16)
    decays, ms, xss = [], [], []
    for h, sl in enumerate(sls):
        diff = gc[:, SM_ALPHA + h:SM_ALPHA + h + 1] - gct[SM_ALPHA + h:SM_ALPHA + h + 1, :]
        decay = jnp.exp(jnp.where(incl, diff, -1e30))
        kk = lax.dot_general(k16[:, sl], k16[:, sl], NT_DIMS, preferred_element_type=F32)
        ms.append(-jnp.where(strict, beta[:, SM_BRAW + h:SM_BRAW + h + 1] * kk * decay, 0.0))
        xss.append(jnp.concatenate([v_b[:, sl], k_b[:, sl]], axis=1))
        decays.append(decay)
    for s in range(n_fac):
        msp = [_split2(m) for m in ms]
        xss = [x + _dot3(mp, _split2(x)) for x, mp in zip(xss, msp)]
        if s < n_fac - 1:
            ms = [_dot3(mp, mp) for mp in msp]
    outs, s_upd = [], []
    for h, sl in enumerate(sls):
        xs = xss[h]
        sh = s_old_b[:, sl]
        v_new = xs[:, :DV_A] - jnp.dot(xs[:, DV_A:].astype(BF16), sh, preferred_element_type=F32)
        v_new_b = v_new.astype(BF16)
        qk = lax.dot_general(q16[:, sl], k16[:, sl], NT_DIMS, preferred_element_type=F32) * decays[h]
        outs.append(jnp.dot(q_s[:, sl].astype(BF16), sh, preferred_element_type=F32)
                    + jnp.dot(qk.astype(BF16), v_new_b, preferred_element_type=F32))
        s_upd.append(lax.dot_general(k_l[:, sl].astype(BF16), v_new_b, TN_DIMS, preferred_element_type=F32))
    s_new = s_old * jnp.exp(gl_w) + jnp.concatenate(s_upd, axis=1)
    s_s[...] = s_new
    o = jnp.concatenate(outs, axis=1)
    ms_o = jnp.dot((o * o).astype(BF16), gmat, preferred_element_type=F32) * (1.0 / DV_A)
    gate = gate_ref[0]
    o_ref[0] = (o * lax.rsqrt(ms_o + EPS) * gnorm_ref[...] * (gate * jax.nn.sigmoid(gate))).astype(o_ref.dtype)

    @pl.when(c == pl.num_programs(1) - 1)
    def _():
        for h in range(H_A):
            sfin_ref[0, h] = s_new[:, h * DV_A:(h + 1) * DV_A]


def _gdn_params(a_log, dt_bias, gnorm):
    alog_p = jnp.zeros((1, LANE), F32).at[0, SM_ALPHA:SM_ALPHA + H_A].set(a_log)
    dtb_p = jnp.zeros((1, LANE), F32).at[0, SM_ALPHA:SM_ALPHA + H_A].set(dt_bias)
    return alog_p, dtb_p, jnp.tile(gnorm, H_A).reshape(1, HD)


def gdn_prompt(proj, conv_w, a_log, dt_bias, gnorm, pad, c_sz=LANE):
    b, l, _ = proj.shape
    blk = lambda w, col: pl.BlockSpec((1, c_sz, w), lambda bi, ci: (bi, ci, col // w))
    const = lambda shp: pl.BlockSpec(shp, lambda bi, ci: (0, 0))
    return pl.pallas_call(
        functools.partial(_gdn_kernel, c_sz, pad),
        grid=(b, l // c_sz),
        in_specs=[blk(QKV_A, COL_QKV), blk(LANE, COL_SMALL), blk(HD, COL_GATE), const((CONV_W, QKV_A)),
                  const((1, LANE)), const((1, LANE)), const((1, HD))],
        out_specs=[pl.BlockSpec((1, c_sz, HD), lambda bi, ci: (bi, ci, 0)),
                   pl.BlockSpec((1, H_A, DK_A, DV_A), lambda bi, ci: (bi, 0, 0, 0))],
        out_shape=[jax.ShapeDtypeStruct((b, l, HD), BF16), jax.ShapeDtypeStruct((b, H_A, DK_A, DV_A), F32)],
        scratch_shapes=[pltpu.VMEM((DK_A, HD), F32), pltpu.VMEM((c_sz, QKV_A), F32)],
        compiler_params=pltpu.CompilerParams(dimension_semantics=("parallel", "arbitrary"),
                                             vmem_limit_bytes=VMEM_LIMIT),
        name="gdn_prompt",
    )(proj, proj, proj, conv_w, *_gdn_params(a_log, dt_bias, gnorm))


def bias_tiles(rel_bias):
    d = jnp.arange(-(LANE - 1), N_BIAS_TILES * LANE, dtype=jnp.int32)
    tab = rel_bias[rel_bucket(d)].astype(F32).T
    tiles = []
    for dl in range(N_BIAS_TILES):
        w = jnp.pad(tab[:, dl * LANE:dl * LANE + 2 * LANE - 1], ((0, 0), (0, 1)))
        sh = jnp.tile(w, (1, LANE))[:, :LANE * (2 * LANE - 1)].reshape(-1, LANE, 2 * LANE - 1)
        tiles.append(sh[:, :, LANE - 1:])
    return jnp.stack(tiles, axis=1)


def _select_topk(sc, valid, krow, topk, idx_bits):
    key = jnp.where(valid, _sortable(sc), INT_MIN)
    kf = jnp.float32(topk)

    def count(m):
        return _col_reduce(jnp.where(m, 1.0, 0.0), jnp.sum)

    t0 = jnp.where(count(key >= 0) >= kf, 0, INT_MIN).astype(jnp.int32)

    def tbody(n, t):
        cand = t | jnp.left_shift(jnp.int32(1), 30 - n)
        return jnp.where(count(key >= cand) >= kf, cand, t)

    t = lax.fori_loop(0, 31, tbody, t0)
    gt = key > t
    eq = key == t
    m = kf - count(gt)

    def jbody(n, j):
        cand = j | jnp.left_shift(jnp.int32(1), idx_bits - 1 - n)
        return jnp.where(count(eq & (krow < cand)) <= m, cand, j)

    split = jnp.max(count(eq) - m) > 0.0
    j = lax.cond(split, lambda: lax.fori_loop(0, idx_bits, jbody, jnp.zeros_like(t)),
                 lambda: jnp.full_like(t, 1 << idx_bits))
    return valid & (gt | (eq & (krow < j)))


def _dsa_prompt_body(lc, pad, topk, q_ref, qi_ref, smq_ref, k_ref, v_ref, smk_ref, bias_ref, o_ref):
    i = pl.program_id(1)
    nkb = lc // LANE
    ki = smk_ref[0, :lc, SM_KI:SM_KI + D_IDX].astype(BF16)
    smt = smq_ref[0].T
    sc = jnp.zeros((lc, LANE), F32)
    for h in range(H_IDX):
        qih = qi_ref[0, :, h * D_IDX:(h + 1) * D_IDX].astype(BF16)
        d = lax.dot_general(ki, qih, NT_DIMS, preferred_element_type=F32)
        w = smt[SM_WI + h:SM_WI + h + 1, :] * (H_IDX ** -0.5 * D_IDX ** -0.5)
        sc = sc + jnp.maximum(d, 0.0) * w
    krow = lax.broadcasted_iota(jnp.int32, (lc, LANE), 0)
    qrow = i * LANE + lax.broadcasted_iota(jnp.int32, (lc, LANE), 1)
    valid = (krow <= qrow) & (krow >= pad)
    sel = _select_topk(sc, valid, krow, topk, max(1, (lc - 1).bit_length()))
    outs = []
    for h in range(H_B):
        sl = slice(h * DH_B, (h + 1) * DH_B)
        kh = k_ref[0, :lc, sl].astype(BF16)
        qh = (q_ref[0, :, sl] * (DH_B ** -0.5)).astype(BF16)
        lg = lax.dot_general(kh, qh, NT_DIMS, preferred_element_type=F32)
        bias = jnp.concatenate([bias_ref[h, jnp.clip(i - jb, 0, N_BIAS_TILES - 1)] for jb in range(nkb)], axis=0)
        lg = jnp.where(sel, lg + bias, -1e30)
        p = jnp.exp(lg - _col_reduce(lg, jnp.max))
        p = (p * (1.0 / _col_reduce(p, jnp.sum))).astype(BF16)
        vh = v_ref[0, :lc, sl].astype(BF16)
        outs.append(lax.dot_general(p, vh, TN_DIMS, preferred_element_type=F32))
    o_ref[0] = jnp.concatenate(outs, axis=1).astype(o_ref.dtype)


def _dsa_prompt_kernel(variants, pad, topk, *refs):
    _dispatch_variants(variants, _dsa_prompt_body, pad, topk, *refs)


def dsa_prompt(proj, btiles, pad, topk):
    b, l, _ = proj.shape
    nq = l // LANE
    qblk = lambda w, col: pl.BlockSpec((1, LANE, w), lambda bi, i: (bi, i, col // w))
    full = lambda w, col: pl.BlockSpec((1, l, w), lambda bi, i: (bi, 0, col // w))
    return pl.pallas_call(
        functools.partial(_dsa_prompt_kernel, _causal_variants(nq, 4), pad, topk),
        grid=(b, nq),
        in_specs=[qblk(HD, COL_QB), qblk(HD, COL_QI), qblk(LANE, COL_SMALL),
                  full(HD, COL_KB), full(HD, COL_VB), full(LANE, COL_SMALL),
                  pl.BlockSpec(btiles.shape, lambda bi, i: (0, 0, 0, 0))],
        out_specs=pl.BlockSpec((1, LANE, HD), lambda bi, i: (bi, i, 0)),
        out_shape=jax.ShapeDtypeStruct((b, l, HD), BF16),
        compiler_params=pltpu.CompilerParams(dimension_semantics=("parallel", "arbitrary"),
                                             vmem_limit_bytes=VMEM_LIMIT),
        name="dsa_prompt",
    )(proj, proj, proj, proj, proj, proj, btiles)


def _mla_prep_kernel(h_ref, nm_ref, win_ref, qn_ref, kvn_ref, wuq_ref, wuk_ref, cos_ref, sin_ref, q_ref, kv_ref):
    x = _rms(h_ref[0], nm_ref[...]).astype(BF16)
    p = jnp.dot(x, win_ref[...], preferred_element_type=F32)
    cqn = _rms(p[:, CC_Q:CC_KV], qn_ref[...]).astype(BF16)
    qf = jnp.dot(cqn, wuq_ref[...], preferred_element_type=F32)
    cos, sin = cos_ref[...], sin_ref[...]
    q1, q2 = qf[:, H_C * LANE:(H_C + 1) * LANE], qf[:, (H_C + 1) * LANE:]
    q1r = q1 * cos - q2 * sin
    q2r = q2 * cos + q1 * sin
    head_of_lane = lax.broadcasted_iota(jnp.int32, (1, LANE), 1) // HALF
    for h in range(H_C):
        qlat = jnp.dot(qf[:, h * LANE:(h + 1) * LANE].astype(BF16), wuk_ref[h], preferred_element_type=F32)
        mine = head_of_lane == h
        q_ref[0, h] = jnp.concatenate([qlat, jnp.where(mine, q1r, 0.0), jnp.where(mine, q2r, 0.0)],
                                      axis=1).astype(q_ref.dtype)
    c = _rms(p[:, CC_KV:CC_K1], kvn_ref[...])
    k1, k2 = p[:, CC_K1:CC_K2], p[:, CC_K2:]
    kv_ref[0] = jnp.concatenate([c, k1 * cos - k2 * sin, k2 * cos + k1 * sin], axis=1)


def mla_prep(h, norm_w, w_in_p, qnorm, kvnorm, w_uq_p, w_uk_p, cos, sin):
    b, l, d = h.shape
    tm = min(LANE, l)
    const2 = lambda shp: pl.BlockSpec(shp, lambda bi, i: (0, 0))
    return pl.pallas_call(
        _mla_prep_kernel,
        grid=(b, l // tm),
        in_specs=[pl.BlockSpec((1, tm, d), lambda bi, i: (bi, i, 0)), const2((1, d)), const2(w_in_p.shape),
                  const2((1, Q_LORA)), const2((1, KV_LORA)), const2(w_uq_p.shape),
                  pl.BlockSpec(w_uk_p.shape, lambda bi, i: (0, 0, 0)),
                  pl.BlockSpec((tm, LANE), lambda bi, i: (i, 0)), pl.BlockSpec((tm, LANE), lambda bi, i: (i, 0))],
        out_specs=[pl.BlockSpec((1, H_C, tm, KVW), lambda bi, i: (bi, 0, i, 0)),
                   pl.BlockSpec((1, tm, KVW), lambda bi, i: (bi, i, 0))],
        out_shape=[jax.ShapeDtypeStruct((b, H_C, l, KVW), BF16), jax.ShapeDtypeStruct((b, l, KVW), F32)],
        compiler_params=pltpu.CompilerParams(dimension_semantics=("parallel", "arbitrary"),
                                             vmem_limit_bytes=VMEM_LIMIT),
        name="mla_prep",
    )(h, norm_w.reshape(1, d), w_in_p, qnorm.reshape(1, -1), kvnorm.reshape(1, -1), w_uq_p, w_uk_p, cos, sin)


def _mla_body(lc, pad, q_ref, kv_ref, o_ref):
    i = pl.program_id(1)
    q = q_ref[0].reshape(H_C * LANE, KVW)
    kb = kv_ref[0, :lc, :].astype(BF16)
    s = lax.dot_general(q, kb, NT_DIMS, preferred_element_type=F32) * MLA_SCALE
    s = s.reshape(H_C, LANE, lc)
    krow = lax.broadcasted_iota(jnp.int32, (LANE, lc), 1)
    qrow = i * LANE + lax.broadcasted_iota(jnp.int32, (LANE, lc), 0)
    ok = (krow <= qrow) & (krow >= pad)
    s = jnp.where(ok[None], s, -1e30)
    mx = jnp.max(s, axis=-1, keepdims=True)
    p = jnp.exp(s - mx)
    inv_l = 1.0 / jnp.sum(p, axis=-1, keepdims=True)
    o = jnp.dot(p.astype(BF16).reshape(H_C * LANE, lc), kb[:, :KV_LORA], preferred_element_type=F32)
    o = (o.reshape(H_C, LANE, KV_LORA) * inv_l).astype(o_ref.dtype)
    for h in range(H_C):
        o_ref[0, :, h * KV_LORA:(h + 1) * KV_LORA] = o[h]


def _mla_kernel(variants, pad, *refs):
    _dispatch_variants(variants, _mla_body, pad, *refs)


def mla_prompt(q, kv, pad):
    b, _, l, _ = q.shape
    nq = l // LANE
    return pl.pallas_call(
        functools.partial(_mla_kernel, _causal_variants(nq, 8), pad),
        grid=(b, nq),
        in_specs=[pl.BlockSpec((1, H_C, LANE, KVW), lambda bi, i: (bi, 0, i, 0)),
                  pl.BlockSpec((1, l, KVW), lambda bi, i: (bi, 0, 0))],
        out_specs=pl.BlockSpec((1, LANE, H_C * KV_LORA), lambda bi, i: (bi, i, 0)),
        out_shape=jax.ShapeDtypeStruct((b, l, H_C * KV_LORA), BF16),
        compiler_params=pltpu.CompilerParams(dimension_semantics=("parallel", "arbitrary"),
                                             vmem_limit_bytes=VMEM_LIMIT),
        name="mla_prompt",
    )(q, kv)


def _gdn_decode_kernel(x_ref, conv_ref, small_ref, gate_ref, s_ref, convw_ref, alog_ref, dtb_ref, gnorm_ref,
                       o_ref, convo_ref, so_ref):
    nb = x_ref.shape[0]
    x = x_ref[...]
    cb = conv_ref[0]
    acc = x * convw_ref[CONV_W - 1:CONV_W, :]
    for j in range(CONV_W - 1):
        acc = acc + cb[:, j, :] * convw_ref[j:j + 1, :]
    for j in range(CONV_W - 2):
        convo_ref[:, j, :] = cb[:, j + 1, :]
    convo_ref[:, CONV_W - 2, :] = x
    y = acc * jax.nn.sigmoid(acc)
    q, k, v = y[:, :HD], y[:, HD:2 * HD], y[:, 2 * HD:]
    gmat = _head_group_matrix()
    q = q * lax.rsqrt(_dot_exact_rhs(q * q, gmat) + EPS) * (DK_A ** -0.5)
    k = k * lax.rsqrt(_dot_exact_rhs(k * k, gmat) + EPS)
    g, beta = _gdn_gates(small_ref[...], alog_ref, dtb_ref)
    eg = jnp.exp(g)
    rowid = lax.broadcasted_iota(jnp.int32, (nb, 1), 0)

    def body(bb, o_acc):
        mine = rowid == bb
        outs = []
        for h in range(H_A):
            sl = slice(h * DK_A, (h + 1) * DK_A)
            s = s_ref[0, bb, h]
            egh = eg[:, SM_ALPHA + h:SM_ALPHA + h + 1]
            eg_b = jnp.sum(jnp.where(mine, egh, 0.0), axis=0, keepdims=True)
            delta = (v[:, sl] - _hdot(k[:, sl], s) * egh) * beta[:, SM_BRAW + h:SM_BRAW + h + 1]
            s_new = s * eg_b + lax.dot_general(jnp.where(mine, k[:, sl], 0.0), jnp.where(mine, delta, 0.0), TN_DIMS,
                                               precision=HI, preferred_element_type=F32)
            so_ref[bb, h] = s_new
            outs.append(_hdot(q[:, sl], s_new))
        return jnp.where(mine, jnp.concatenate(outs, axis=1), o_acc)

    o = lax.fori_loop(0, nb, body, jnp.zeros((nb, HD), F32))
    ms_o = _dot_exact_rhs(o * o, gmat) * (1.0 / DV_A)
    gate = gate_ref[...]
    o_ref[...] = (o * lax.rsqrt(ms_o + EPS) * gnorm_ref[...] * (gate * jax.nn.sigmoid(gate))).astype(o_ref.dtype)


def gdn_decode_step(proj, conv_state, ssm_state, layer, conv_w, a_log, dt_bias, gnorm, nb=8):
    b = proj.shape[0]
    blk = lambda w, col: pl.BlockSpec((nb, w), lambda i: (i, col // w))
    const = lambda shp: pl.BlockSpec(shp, lambda i: (0, 0))
    return pl.pallas_call(
        _gdn_decode_kernel,
        grid=(b // nb,),
        in_specs=[blk(QKV_A, COL_QKV), pl.BlockSpec((1, nb, CONV_W - 1, QKV_A), lambda i: (layer, i, 0, 0)),
                  blk(LANE, COL_SMALL), blk(HD, COL_GATE),
                  pl.BlockSpec((1, nb, H_A, DK_A, DV_A), lambda i: (layer, i, 0, 0, 0)),
                  const((CONV_W, QKV_A)), const((1, LANE)), const((1, LANE)), const((1, HD))],
        out_specs=[pl.BlockSpec((nb, HD), lambda i: (i, 0)),
                   pl.BlockSpec((nb, CONV_W - 1, QKV_A), lambda i: (i, 0, 0)),
                   pl.BlockSpec((nb, H_A, DK_A, DV_A), lambda i: (i, 0, 0, 0))],
        out_shape=[jax.ShapeDtypeStruct((b, HD), BF16), jax.ShapeDtypeStruct(conv_state.shape[1:], F32),
                   jax.ShapeDtypeStruct(ssm_state.shape[1:], F32)],
        compiler_params=pltpu.CompilerParams(dimension_semantics=("parallel",), vmem_limit_bytes=VMEM_LIMIT),
        name="gdn_decode",
    )(proj, conv_state, proj, proj, ssm_state, conv_w, *_gdn_params(a_log, dt_bias, gnorm))


def _dsa_select_kernel(topk, n_pages, pt_ref, qi_ref, small_ref, *rest):
    ki_refs = rest[:SEL_DEC_PAGES]
    mask_ref, mnew_ref, sc_s = rest[SEL_DEC_PAGES:]
    j = pl.program_id(1)
    qi8 = _rows_to_heads(qi_ref[0], D_IDX).astype(BF16)
    sm = small_ref[0]
    pick = (lax.broadcasted_iota(jnp.int32, (H_IDX, LANE), 1)
            == lax.broadcasted_iota(jnp.int32, (H_IDX, LANE), 0) + SM_WI)
    w8 = jnp.sum(jnp.where(pick, sm, 0.0), axis=1, keepdims=True) * (H_IDX ** -0.5)
    kit = jnp.concatenate([r[0, 0] for r in ki_refs], axis=1).astype(BF16)
    d = jnp.dot(qi8, kit, preferred_element_type=F32)
    s = jnp.sum(jnp.maximum(d * (D_IDX ** -0.5), 0.0) * w8, axis=0, keepdims=True)
    for g in range(SEL_DEC_PAGES):
        sc_s[pl.ds(j * SEL_DEC_PAGES + g, 1), :] = s[:, g * PAGE_SIZE:(g + 1) * PAGE_SIZE]

    @pl.when(j == pl.num_programs(1) - 1)
    def _():
        past = n_pages * PAGE_SIZE
        ki_new = sm[:, SM_KI:SM_KI + D_IDX].astype(BF16).astype(F32)
        d_new = jnp.sum(qi8.astype(F32) * ki_new, axis=1, keepdims=True)
        s_new = jnp.sum(jnp.maximum(d_new * (D_IDX ** -0.5), 0.0) * w8, axis=0, keepdims=True)
        key = _sortable(sc_s[...])
        key_n = _sortable(s_new)
        idx = lax.broadcasted_iota(jnp.int32, key.shape, 0) * PAGE_SIZE + lax.broadcasted_iota(jnp.int32, key.shape, 1)
        kf = jnp.float32(topk)

        def count(m, mn):
            c = jnp.sum(jnp.sum(jnp.where(m, 1.0, 0.0), axis=0, keepdims=True), axis=1, keepdims=True)
            return c + jnp.where(mn, 1.0, 0.0)

        t0 = jnp.where(count(key >= 0, key_n >= 0) >= kf, 0, INT_MIN).astype(jnp.int32)

        def tbody(n, t):
            cand = t | jnp.left_shift(jnp.int32(1), 30 - n)
            return jnp.where(count(key >= cand, key_n >= cand) >= kf, cand, t)

        t = lax.fori_loop(0, 31, tbody, t0)
        m = kf - count(key > t, key_n > t)
        idx_bits = past.bit_length()

        def jbody(n, jj):
            cand = jj | jnp.left_shift(jnp.int32(1), idx_bits - 1 - n)
            return jnp.where(count((key == t) & (idx < cand), (key_n == t) & (past < cand)) <= m, cand, jj)

        jj = lax.fori_loop(0, idx_bits, jbody, jnp.zeros_like(t))
        mask_ref[0] = jnp.where((key > t) | ((key == t) & (idx < jj)), 1.0, 0.0)
        sel_n = (key_n > t) | ((key_n == t) & (past < jj))
        lane0 = lax.broadcasted_iota(jnp.int32, (1, LANE), 1) == 0
        mnew_ref[0] = jnp.where(lane0 & sel_n, 1.0, 0.0)


def dsa_decode_select(proj3, cache_kit, layer, page_table, topk):
    b = proj3.shape[0]
    n_pages = page_table.shape[1]
    page_spec = lambda g: pl.BlockSpec((1, 1, D_IDX, PAGE_SIZE),
                                       lambda bi, j, pt: (layer, pt[bi, j * SEL_DEC_PAGES + g], 0, 0))
    grid_spec = pltpu.PrefetchScalarGridSpec(
        num_scalar_prefetch=1,
        grid=(b, n_pages // SEL_DEC_PAGES),
        in_specs=[pl.BlockSpec((1, 1, HD), lambda bi, j, pt: (bi, 0, COL_QI // HD)),
                  pl.BlockSpec((1, 1, LANE), lambda bi, j, pt: (bi, 0, COL_SMALL // LANE))]
                 + [page_spec(g) for g in range(SEL_DEC_PAGES)],
        out_specs=[pl.BlockSpec((1, n_pages, PAGE_SIZE), lambda bi, j, pt: (bi, 0, 0)),
                   pl.BlockSpec((1, 1, LANE), lambda bi, j, pt: (bi, 0, 0))],
        scratch_shapes=[pltpu.VMEM((n_pages, PAGE_SIZE), F32)],
    )
    return pl.pallas_call(
        functools.partial(_dsa_select_kernel, topk, n_pages),
        grid_spec=grid_spec,
        out_shape=[jax.ShapeDtypeStruct((b, n_pages, PAGE_SIZE), F32), jax.ShapeDtypeStruct((b, 1, LANE), F32)],
        compiler_params=pltpu.CompilerParams(dimension_semantics=("parallel", "arbitrary"),
                                             vmem_limit_bytes=VMEM_LIMIT),
        name="dsa_decode_select",
    )(page_table, proj3, proj3, *([cache_kit] * SEL_DEC_PAGES))


def _dsa_attend_kernel(n_pages, pt_ref, thr_ref, q_ref, knt_ref, vnt_ref, mask_ref, mnew_ref, rb_ref, *rest):
    kt_refs, vt_refs = rest[:DEC_PAGES], rest[DEC_PAGES:2 * DEC_PAGES]
    o_ref, qbd_s, m_s, l_s, acc_s = rest[2 * DEC_PAGES:]
    j = pl.program_id(1)
    past = n_pages * PAGE_SIZE
    eye = lax.broadcasted_iota(jnp.int32, (H_B, LANE), 0) == lax.broadcasted_iota(jnp.int32, (H_B, LANE), 1)

    @pl.when(j == 0)
    def _():
        bd = lax.broadcasted_iota(jnp.int32, (H_B, HD), 1) // DH_B == lax.broadcasted_iota(jnp.int32, (H_B, HD), 0)
        qbd_s[...] = jnp.where(bd, jnp.broadcast_to(q_ref[0], (H_B, HD)), 0.0).astype(BF16)
        m_s[...] = jnp.full_like(m_s, -1e30)
        l_s[...] = jnp.zeros_like(l_s)
        acc_s[...] = jnp.zeros_like(acc_s)

    def attend(kts, vts, ok, first_key):
        n = len(kts)
        lg = jnp.concatenate([jnp.dot(qbd_s[...], kt.astype(BF16), preferred_element_type=F32) for kt in kts], axis=1)
        dist = past - first_key - lax.broadcasted_iota(jnp.int32, (1, n * PAGE_SIZE), 1)
        bias = jnp.broadcast_to(rb_ref[0], lg.shape)
        for bk in range(1, REL_BUCKETS):
            bias = jnp.where(dist >= thr_ref[bk], rb_ref[bk], bias)
        lg = jnp.where(ok, lg * (DH_B ** -0.5) + bias, -1e30)
        m_new = jnp.maximum(m_s[...], jnp.max(lg, axis=1, keepdims=True))
        alpha = jnp.exp(m_s[...] - m_new)
        p = jnp.where(ok, jnp.exp(lg - m_new), 0.0)
        l_s[...] = l_s[...] * alpha + jnp.sum(p, axis=1, keepdims=True)
        m_s[...] = m_new
        pb = p.astype(BF16)
        pv = [lax.dot_general(vts[g].astype(BF16), pb[:, g * PAGE_SIZE:(g + 1) * PAGE_SIZE], NT_DIMS,
                              preferred_element_type=F32) for g in range(n)]
        while len(pv) > 1:
            pv = [pv[i] + pv[i + 1] if i + 1 < len(pv) else pv[i] for i in range(0, len(pv), 2)]
        alpha_row = jnp.sum(jnp.where(eye, alpha, 0.0), axis=0, keepdims=True)
        acc_s[...] = acc_s[...] * alpha_row[:, :H_B] + pv[0]

    rows = mask_ref[0, pl.ds(j * DEC_PAGES, DEC_PAGES), :]
    ok = jnp.concatenate([rows[g:g + 1] for g in range(DEC_PAGES)], axis=1) > 0.0
    attend([r[0, 0].reshape(HD, PAGE_SIZE) for r in kt_refs], [r[0, 0].reshape(HD, PAGE_SIZE) for r in vt_refs],
           ok, j * (DEC_PAGES * PAGE_SIZE))

    @pl.when(j == pl.num_programs(1) - 1)
    def _():
        attend([knt_ref[0]], [vnt_ref[0]], mnew_ref[0] > 0.0, past)
        l_row = jnp.sum(jnp.where(eye, l_s[...], 0.0), axis=0, keepdims=True)
        o = acc_s[...] * (1.0 / l_row[:, :H_B])
        own = (lax.broadcasted_iota(jnp.int32, (HD, H_B), 0) // DH_B
               == lax.broadcasted_iota(jnp.int32, (HD, H_B), 1))
        o_ref[0] = jnp.sum(jnp.where(own, o, 0.0), axis=1, keepdims=True).astype(o_ref.dtype)


def _decode_thresholds(past):
    buckets = rel_bucket(jnp.arange(past + 1, dtype=jnp.int32))
    return jnp.sum(buckets[None, :] < jnp.arange(REL_BUCKETS, dtype=jnp.int32)[:, None], axis=1).astype(jnp.int32)


def dsa_decode_attend(proj, mask, mnew, cache_kt, cache_vt, layer, page_table, rel_bias):
    b = proj.shape[0]
    n_pages = page_table.shape[1]
    past = n_pages * PAGE_SIZE
    thr = _decode_thresholds(past)
    rb = rel_bias.astype(F32)[:, :, None]
    proj3 = proj.reshape(b, 1, AB_COLS_P)
    one_key_page = lambda col: jnp.pad(proj[:, col:col + HD, None], ((0, 0), (0, 0), (0, PAGE_SIZE - 1)))
    page_spec = lambda g: pl.BlockSpec((1, 1, H_B, DH_B, PAGE_SIZE),
                                       lambda bi, j, pt, th: (layer, pt[bi, j * DEC_PAGES + g], 0, 0, 0))
    per_b = lambda shp: pl.BlockSpec((1,) + shp, lambda bi, j, pt, th: (bi, 0, 0))
    grid_spec = pltpu.PrefetchScalarGridSpec(
        num_scalar_prefetch=2,
        grid=(b, n_pages // DEC_PAGES),
        in_specs=[pl.BlockSpec((1, 1, HD), lambda bi, j, pt, th: (bi, 0, COL_QB // HD)),
                  per_b((HD, PAGE_SIZE)), per_b((HD, PAGE_SIZE)), per_b((n_pages, PAGE_SIZE)), per_b((1, LANE)),
                  pl.BlockSpec(rb.shape, lambda bi, j, pt, th: (0, 0, 0))]
                 + [page_spec(g) for g in range(DEC_PAGES)] * 2,
        out_specs=per_b((HD, 1)),
        scratch_shapes=[pltpu.VMEM((H_B, HD), BF16), pltpu.VMEM((H_B, 1), F32), pltpu.VMEM((H_B, 1), F32),
                        pltpu.VMEM((HD, H_B), F32)],
    )
    return pl.pallas_call(
        functools.partial(_dsa_attend_kernel, n_pages),
        grid_spec=grid_spec,
        out_shape=jax.ShapeDtypeStruct((b, HD, 1), F32),
        compiler_params=pltpu.CompilerParams(dimension_semantics=("parallel", "arbitrary"),
                                             vmem_limit_bytes=VMEM_LIMIT),
        name="dsa_decode_attend",
    )(page_table, thr, proj3, one_key_page(COL_KB), one_key_page(COL_VB), mask, mnew, rb,
      *([cache_kt] * DEC_PAGES), *([cache_vt] * DEC_PAGES))


def _online_update(lg, m_s, l_s):
    m_new = jnp.maximum(m_s[...], jnp.max(lg, axis=1, keepdims=True))
    alpha = jnp.exp(m_s[...] - m_new)
    p = jnp.exp(lg - m_new)
    l_s[...] = l_s[...] * alpha + jnp.sum(p, axis=1, keepdims=True)
    m_s[...] = m_new
    return p, alpha


def _mla_decode_kernel(pt_ref, q_ref, kvn_ref, exp_ref, *rest):
    c_refs, kpe_refs = rest[:MLA_DEC_PAGES], rest[MLA_DEC_PAGES:2 * MLA_DEC_PAGES]
    o_ref, m_s, l_s, acc_s = rest[2 * MLA_DEC_PAGES:]
    j = pl.program_id(1)

    @pl.when(j == 0)
    def _():
        m_s[...] = jnp.full_like(m_s, -1e30)
        l_s[...] = jnp.zeros_like(l_s)
        acc_s[...] = jnp.zeros_like(acc_s)

    q = q_ref[0]
    q_lat = q[:, :KV_LORA]
    q_pe = jnp.dot(q[:, KV_LORA:], exp_ref[...], preferred_element_type=F32).astype(BF16)
    cb = jnp.concatenate([r[0, 0] for r in c_refs], axis=0).astype(BF16)
    kpet = jnp.concatenate([r[0, 0] for r in kpe_refs], axis=1).astype(BF16)
    lg = (lax.dot_general(q_lat, cb, NT_DIMS, preferred_element_type=F32)
          + jnp.dot(q_pe, kpet, preferred_element_type=F32)) * MLA_SCALE
    p, alpha = _online_update(lg, m_s, l_s)
    acc_s[...] = acc_s[...] * alpha + jnp.dot(p.astype(BF16), cb, preferred_element_type=F32)

    @pl.when(j == pl.num_programs(1) - 1)
    def _():
        kvn = kvn_ref[0].astype(BF16).astype(F32)
        lgn = jnp.sum(q.astype(F32) * kvn, axis=1, keepdims=True) * MLA_SCALE
        pn, alpha_n = _online_update(lgn, m_s, l_s)
        acc = acc_s[...] * alpha_n + pn.astype(BF16).astype(F32) * kvn[:, :KV_LORA]
        o = acc * (1.0 / l_s[...])
        o_ref[0] = jnp.concatenate([o[h:h + 1] for h in range(H_C)], axis=1).astype(o_ref.dtype)


def mla_decode(q, kvn, cache_c, cache_kpet, layer, page_table):
    b = q.shape[0]
    n_pages = page_table.shape[1]
    src = jnp.concatenate([jnp.tile(jnp.arange(HALF), H_C), HALF + jnp.tile(jnp.arange(HALF), H_C)])
    expand = (src[:, None] == jnp.arange(QK_ROPE)[None, :]).astype(BF16)
    cspec = lambda g: pl.BlockSpec((1, 1, PAGE_SIZE, KV_LORA),
                                   lambda bi, j, pt: (layer, pt[bi, j * MLA_DEC_PAGES + g], 0, 0))
    kspec = lambda g: pl.BlockSpec((1, 1, QK_ROPE, PAGE_SIZE),
                                   lambda bi, j, pt: (layer, pt[bi, j * MLA_DEC_PAGES + g], 0, 0))
    grid_spec = pltpu.PrefetchScalarGridSpec(
        num_scalar_prefetch=1,
        grid=(b, n_pages // MLA_DEC_PAGES),
        in_specs=[pl.BlockSpec((1, H_C, KVW), lambda bi, j, pt: (bi, 0, 0)),
                  pl.BlockSpec((1, 1, KVW), lambda bi, j, pt: (bi, 0, 0)),
                  pl.BlockSpec(expand.shape, lambda bi, j, pt: (0, 0))]
                 + [cspec(g) for g in range(MLA_DEC_PAGES)] + [kspec(g) for g in range(MLA_DEC_PAGES)],
        out_specs=pl.BlockSpec((1, 1, H_C * KV_LORA), lambda bi, j, pt: (bi, 0, 0)),
        scratch_shapes=[pltpu.VMEM((H_C, 1), F32), pltpu.VMEM((H_C, 1), F32), pltpu.VMEM((H_C, KV_LORA), F32)],
    )
    return pl.pallas_call(
        _mla_decode_kernel,
        grid_spec=grid_spec,
        out_shape=jax.ShapeDtypeStruct((b, 1, H_C * KV_LORA), BF16),
        compiler_params=pltpu.CompilerParams(dimension_semantics=("parallel", "arbitrary"),
                                             vmem_limit_bytes=VMEM_LIMIT),
        name="mla_decode",
    )(page_table, q, kvn, expand, *([cache_c] * MLA_DEC_PAGES), *([cache_kpet] * MLA_DEC_PAGES))


def _prep_w_ab(w):
    qkv_a, alpha, braw, gate, q_b, k_b, v_b, qi, wi, ki = split_cols(w, AB_SIZES)
    z = jnp.zeros((w.shape[0], SM_KI - SM_WI - H_IDX), w.dtype)
    return jnp.concatenate([qkv_a, gate, q_b, k_b, v_b, qi, alpha, braw, wi, z, ki], axis=1).astype(BF16)


def _prep_w_c(w_in, w_uq, w_uk):
    cq, ckv, kpe = split_cols(w_in, C_SIZES)
    w_in_p = jnp.concatenate([cq, ckv, jnp.tile(kpe[:, :HALF], (1, H_C)), jnp.tile(kpe[:, HALF:], (1, H_C))], axis=1)
    wq = w_uq.reshape(Q_LORA, H_C, QK_NOPE + QK_ROPE)
    nope = jnp.pad(wq[:, :, :QK_NOPE], ((0, 0), (0, 0), (0, LANE - QK_NOPE))).reshape(Q_LORA, H_C * LANE)
    r1 = wq[:, :, QK_NOPE:QK_NOPE + HALF].reshape(Q_LORA, H_C * HALF)
    r2 = wq[:, :, QK_NOPE + HALF:].reshape(Q_LORA, H_C * HALF)
    w_uq_p = jnp.concatenate([nope, r1, r2], axis=1)
    w_uk_p = jnp.pad(jnp.transpose(w_uk, (1, 2, 0)), ((0, 0), (0, LANE - QK_NOPE), (0, 0)))
    return w_in_p.astype(BF16), w_uq_p.astype(BF16), w_uk_p.astype(BF16)


def _rope_tables(pos):
    inv = ROPE_BASE ** (-jnp.arange(HALF, dtype=jnp.float32) / HALF)
    ang = pos.astype(jnp.float32)[:, None] * inv[None, :]
    return jnp.tile(jnp.cos(ang), (1, LANE // HALF)), jnp.tile(jnp.sin(ang), (1, LANE // HALF))


def _kpe_from_rows(kv):
    return jnp.concatenate([kv[..., KV_LORA:KV_LORA + HALF], kv[..., KV_LORA + LANE:KV_LORA + LANE + HALF]], axis=-1)


def kernel(x_prompt, x_sample, state_a_conv, state_a_ssm, cache_b_k, cache_b_v, cache_b_kidx,
           cache_c_latent, cache_c_kpe, page_table, meta_tokens, rel_bias, norm_mix, norm_ffn, norm_final,
           w_in_ab, conv_a, a_log, dt_bias_a, gnorm_a, w_out_ab, w_in_c, qnorm_c, kvnorm_c,
           w_uq_c, w_uk_c, w_uv_c, w_out_c, w_up, w_down):
    w_up_b = w_up.astype(BF16)
    w_down_b = w_down.astype(BF16)
    w_out_ab_b = w_out_ab.astype(BF16)
    w_out_c_b = w_out_c.astype(BF16)
    w_uv_b = jnp.transpose(w_uv_c, (0, 2, 1, 3)).astype(BF16)
    w_in_ab_p = [_prep_w_ab(w_in_ab[j]) for j in range(w_in_ab.shape[0])]
    w_c_p = [_prep_w_c(w_in_c[j], w_uq_c[j], w_uk_c[j]) for j in range(w_in_c.shape[0])]

    b, seq, d = x_prompt.shape
    lp = N_META + seq
    pad = (-lp) % LANE
    l = pad + lp
    n = b * l
    topk = min(TOPK_MAX, SEQ // 4)
    meta = jnp.broadcast_to(meta_tokens[None].astype(x_prompt.dtype), (b, N_META, d))
    h = jnp.concatenate([jnp.zeros((b, pad, d), x_prompt.dtype), meta, x_prompt], axis=1).reshape(n, d)
    cos_p, sin_p = _rope_tables(jnp.arange(l, dtype=jnp.int32) - pad)
    a_conv_p, a_ssm_p, b_k_p, b_v_p, b_kidx_p, c_lat_p, c_kpe_p = [], [], [], [], [], [], []
    for li in range(DEPTH):
        j = li // 2
        last = li == DEPTH - 1
        if li % 2 == 0:
            proj = rms_linear(h, norm_mix[li], w_in_ab_p[j]).reshape(b, l, AB_COLS_P)
            o_a, s_fin = gdn_prompt(proj, conv_a[j], a_log[j], dt_bias_a[j], gnorm_a[j], pad)
            o_b = dsa_prompt(proj, bias_tiles(rel_bias), pad, topk)
            projs = [(o_a.reshape(n, HD), w_out_ab_b[j, :HD]), (o_b.reshape(n, HD), w_out_ab_b[j, HD:])]
            head_w = None
            a_conv_p.append(proj[:, l - (CONV_W - 1):, COL_QKV:COL_QKV + QKV_A])
            a_ssm_p.append(s_fin)
            b_k_p.append(proj[:, pad:, COL_KB:COL_KB + HD].reshape(b, lp, H_B, DH_B))
            b_v_p.append(proj[:, pad:, COL_VB:COL_VB + HD].reshape(b, lp, H_B, DH_B))
            b_kidx_p.append(proj[:, pad:, COL_SMALL + SM_KI:COL_SMALL + SM_KI + D_IDX])
        else:
            q_all, kv = mla_prep(h.reshape(b, l, d), norm_mix[li], *w_c_p[j][:1], qnorm_c[j], kvnorm_c[j],
                                 *w_c_p[j][1:], cos_p, sin_p)
            o_lat = mla_prompt(q_all, kv, pad)
            projs = [(o_lat.reshape(n, H_C * KV_LORA), w_out_c_b[j])]
            head_w = w_uv_b[j]
            c_lat_p.append(kv[:, pad:, :KV_LORA])
            c_kpe_p.append(_kpe_from_rows(kv[:, pad:]))
        h = proj_mlp(h, projs, norm_ffn[li], w_up_b[li], w_down_b[li], norm_final if last else None, head_w)
    y_p = h.reshape(b, l, d)[:, pad + N_META:]

    bs = x_sample.shape[0]
    n_pages = page_table.shape[1]
    past = n_pages * PAGE_SIZE
    topk_s = min(TOPK_MAX, (past + DEC_SEQ) // 4)
    hs = x_sample.reshape(bs, d)
    cos_s, sin_s = _rope_tables(jnp.full((bs,), past, jnp.int32))
    cache_kit = jnp.transpose(cache_b_kidx, (0, 1, 3, 2))
    cache_kt = jnp.transpose(cache_b_k, (0, 1, 3, 4, 2))
    cache_vt = jnp.transpose(cache_b_v, (0, 1, 3, 4, 2))
    cache_kpet = jnp.transpose(cache_c_kpe, (0, 1, 3, 2))
    a_conv_s, a_ssm_s, b_k_s, b_v_s, b_kidx_s, c_lat_s, c_kpe_s = [], [], [], [], [], [], []
    for li in range(DEPTH):
        j = li // 2
        last = li == DEPTH - 1
        if li % 2 == 0:
            proj = rms_linear(hs, norm_mix[li], w_in_ab_p[j])
            proj3 = proj.reshape(bs, 1, AB_COLS_P)
            o_a, conv_new, s_new = gdn_decode_step(proj, state_a_conv, state_a_ssm, j, conv_a[j], a_log[j],
                                                   dt_bias_a[j], gnorm_a[j])
            mask, mnew = dsa_decode_select(proj3, cache_kit, j, page_table, topk_s)
            o_b = dsa_decode_attend(proj, mask, mnew, cache_kt, cache_vt, j, page_table, rel_bias)
            projs = [(o_a, w_out_ab_b[j, :HD]), (o_b.reshape(bs, HD), w_out_ab_b[j, HD:])]
            head_w = None
            a_conv_s.append(conv_new)
            a_ssm_s.append(s_new)
            b_k_s.append(proj[:, COL_KB:COL_KB + HD].reshape(bs, DEC_SEQ, H_B, DH_B))
            b_v_s.append(proj[:, COL_VB:COL_VB + HD].reshape(bs, DEC_SEQ, H_B, DH_B))
            b_kidx_s.append(proj[:, COL_SMALL + SM_KI:COL_SMALL + SM_KI + D_IDX].reshape(bs, DEC_SEQ, D_IDX))
        else:
            q_all, kv = mla_prep(hs.reshape(1, bs, d), norm_mix[li], *w_c_p[j][:1], qnorm_c[j], kvnorm_c[j],
                                 *w_c_p[j][1:], cos_s, sin_s)
            o_lat = mla_decode(jnp.transpose(q_all[0], (1, 0, 2)), kv.reshape(bs, 1, KVW), cache_c_latent,
                               cache_kpet, j, page_table)
            projs = [(o_lat.reshape(bs, H_C * KV_LORA), w_out_c_b[j])]
            head_w = w_uv_b[j]
            c_lat_s.append(kv[0, :, :KV_LORA].reshape(bs, DEC_SEQ, KV_LORA))
            c_kpe_s.append(_kpe_from_rows(kv[0]).reshape(bs, DEC_SEQ, QK_ROPE))
        hs = proj_mlp(hs, projs, norm_ffn[li], w_up_b[li], w_down_b[li], norm_final if last else None, head_w)
    y_s = hs.reshape(bs, DEC_SEQ, d)

    stack = jnp.stack
    return (y_p, y_s,
            stack(a_conv_p), stack(a_ssm_p), stack(b_k_p), stack(b_v_p), stack(b_kidx_p), stack(c_lat_p), stack(c_kpe_p),
            stack(a_conv_s), stack(a_ssm_s), stack(b_k_s), stack(b_v_s), stack(b_kidx_s), stack(c_lat_s), stack(c_kpe_s))
```

```python
import functools
import math
import jax
import jax.numpy as jnp
from jax import lax
from jax.experimental import pallas as pl
from jax.experimental.pallas import tpu as pltpu

D_MODEL = 1024
BATCH = 8
SEQ = 2048
DEPTH = 2
DEC_BATCH = 32
DEC_SEQ = 1
PAST_LEN = 16384
PAGE_SIZE = 128

N_META = 16
EPS = 1e-6
H_A = 8
DK_A = 64
DV_A = 64
CONV_W = 4
H_B = 8
DH_B = 64
H_IDX = 8
D_IDX = 64
TOPK_MAX = 256
REL_BUCKETS = 32
REL_MAX_DIST = 1024
H_C = 8
Q_LORA = 384
KV_LORA = 256
QK_NOPE = 64
QK_ROPE = 32
V_C = 128
ROPE_BASE = 10000.0
D_FF = 4 * D_MODEL

QKV_A = H_A * (2 * DK_A + DV_A)
AB_SIZES = (QKV_A, H_A, H_A, H_A * DV_A, H_B * DH_B, H_B * DH_B, H_B * DH_B, H_IDX * D_IDX, H_IDX, D_IDX)
C_SIZES = (Q_LORA, KV_LORA, QK_ROPE)
MLA_SCALE = (QK_NOPE + QK_ROPE) ** -0.5

BF16 = jnp.bfloat16
F32 = jnp.float32
LANE = 128
VMEM_LIMIT = 56 * 1024 * 1024
INT_MIN = -(2 ** 31)
NT_DIMS = (((1,), (1,)), ((), ()))
TN_DIMS = (((0,), (0,)), ((), ()))
HI = lax.Precision.HIGHEST
HD = H_A * DK_A
HALF = QK_ROPE // 2

COL_QKV, COL_GATE, COL_QB, COL_KB, COL_VB, COL_QI, COL_SMALL = 0, 1536, 2048, 2560, 3072, 3584, 4096
AB_COLS_P = COL_SMALL + LANE
SM_ALPHA, SM_BRAW, SM_WI, SM_KI = 0, 8, 16, 64
CC_Q, CC_KV, CC_K1, CC_K2, C_COLS_P = 0, 384, 640, 768, 896
KVW = KV_LORA + 2 * LANE
N_BIAS_TILES = 9
DEC_PAGES = 16
MLA_DEC_PAGES = 32
SEL_DEC_PAGES = 32


def split_cols(x, sizes):
    out, off = [], 0
    for s in sizes:
        out.append(x[..., off:off + s])
        off += s
    return out


def rel_bucket(dist):
    dist = jnp.maximum(dist, 0)
    max_exact = REL_BUCKETS // 2
    log_ratio = jnp.log(jnp.maximum(dist, 1).astype(jnp.float32) / max_exact) / math.log(REL_MAX_DIST / max_exact)
    large = max_exact + (log_ratio * (REL_BUCKETS - max_exact)).astype(jnp.int32)
    return jnp.where(dist < max_exact, dist, jnp.minimum(large, REL_BUCKETS - 1))


def _rms(x, w):
    return x * lax.rsqrt(jnp.mean(x * x, axis=-1, keepdims=True) + EPS) * w


def _hdot(a, b):
    return jnp.dot(a, b, precision=HI, preferred_element_type=F32)


def _dot_exact_rhs(a, b_bf16):
    a_hi = a.astype(BF16)
    a_lo = (a - a_hi.astype(F32)).astype(BF16)
    return jnp.dot(a_hi, b_bf16, preferred_element_type=F32) + jnp.dot(a_lo, b_bf16, preferred_element_type=F32)


def _split2(a):
    hi = a.astype(BF16)
    return hi, (a - hi.astype(F32)).astype(BF16)


def _dot3(ap, bp):
    (ah, al), (bh, bl) = ap, bp
    return (jnp.dot(ah, bh, preferred_element_type=F32) + jnp.dot(al, bh, preferred_element_type=F32)
            + jnp.dot(ah, bl, preferred_element_type=F32))


def _sortable(x):
    bits = pltpu.bitcast(x + 0.0, jnp.int32)
    return bits ^ ((bits >> 31) & 0x7FFFFFFF)


def _rows_to_heads(row, width):
    return jnp.concatenate([row[:, h * width:(h + 1) * width] for h in range(row.shape[1] // width)], axis=0)


def _col_reduce(x, op, groups=8):
    l, w = x.shape
    if l % (8 * groups) == 0:
        x = op(op(x.reshape(groups, l // (8 * groups), 8, w), axis=1), axis=0)
    return op(x, axis=0, keepdims=True)


def _causal_variants(nq, max_variants):
    n_var = min(max_variants, nq)
    his = [nq - (n_var - 1 - v) * (nq // n_var) for v in range(n_var)]
    return [(hi, hi * LANE) for hi in his]


def _dispatch_variants(variants, body, *args):
    i = pl.program_id(1)
    lo = 0
    for hi, lc in variants:
        pl.when((i >= lo) & (i < hi))(functools.partial(body, lc, *args))
        lo = hi


def _rms_linear_kernel(x_ref, nw_ref, w_ref, o_ref):
    xn = _rms(x_ref[...], nw_ref[...]).astype(BF16)
    o_ref[...] = jnp.dot(xn, w_ref[...], preferred_element_type=F32)


def rms_linear(x, norm_w, w, tm=256):
    n, d = x.shape
    m = w.shape[1]
    tm = min(tm, n)
    return pl.pallas_call(
        _rms_linear_kernel,
        grid=(pl.cdiv(n, tm),),
        in_specs=[pl.BlockSpec((tm, d), lambda i: (i, 0)), pl.BlockSpec((1, d), lambda i: (0, 0)),
                  pl.BlockSpec((d, m), lambda i: (0, 0))],
        out_specs=pl.BlockSpec((tm, m), lambda i: (i, 0)),
        out_shape=jax.ShapeDtypeStruct((n, m), F32),
        compiler_params=pltpu.CompilerParams(dimension_semantics=("parallel",), vmem_limit_bytes=VMEM_LIMIT),
        name="rms_linear",
    )(x, norm_w.reshape(1, d), w)


def _proj_mlp_kernel(n_proj, n_heads, final, *refs):
    refs = list(refs)
    h_ref = refs.pop(0)
    a_refs = [refs.pop(0) for _ in range(n_proj)]
    w_refs = [refs.pop(0) for _ in range(n_proj)]
    hw_ref = refs.pop(0) if n_heads else None
    nw_ref, wup_ref, wdn_ref = refs.pop(0), refs.pop(0), refs.pop(0)
    nf_ref = refs.pop(0) if final else None
    o_ref, h1_s, xn_s, acc_s = refs
    f = pl.program_id(1)

    @pl.when(f == 0)
    def _():
        h1 = h_ref[...]
        for a_ref, w_ref in zip(a_refs, w_refs):
            a = a_ref[...]
            if n_heads:
                kh = a.shape[1] // n_heads
                a = jnp.concatenate([jnp.dot(a[:, hh * kh:(hh + 1) * kh].astype(BF16), hw_ref[hh],
                                             preferred_element_type=F32) for hh in range(n_heads)], axis=1)
            h1 = h1 + jnp.dot(a.astype(BF16), w_ref[...], preferred_element_type=F32)
        h1_s[...] = h1
        xn_s[...] = _rms(h1, nw_ref[...]).astype(BF16)
        acc_s[...] = jnp.zeros_like(acc_s)

    u = jnp.dot(xn_s[...], wup_ref[...], preferred_element_type=F32)
    u = jnp.square(jnp.maximum(u, 0.0)).astype(BF16)
    acc_s[...] += jnp.dot(u, wdn_ref[...], preferred_element_type=F32)

    @pl.when(f == pl.num_programs(1) - 1)
    def _():
        h2 = h1_s[...] + acc_s[...]
        if final:
            h2 = _rms(h2, nf_ref[...])
        o_ref[...] = h2


def proj_mlp(h, projs, norm_w, w_up, w_down, norm_final=None, head_w=None, tm=1024, tf=1024):
    n, d = h.shape
    ff = w_up.shape[1]
    tm = min(tm, n)
    final = norm_final is not None
    n_heads = 0 if head_w is None else head_w.shape[0]
    row = lambda i, f: (i, 0)
    const = lambda i, f: (0, 0)
    in_specs = [pl.BlockSpec((tm, d), row)]
    in_specs += [pl.BlockSpec((tm, a.shape[1]), row) for a, _ in projs]
    in_specs += [pl.BlockSpec(w.shape, const) for _, w in projs]
    args = [h] + [a for a, _ in projs] + [w for _, w in projs]
    if n_heads:
        in_specs.append(pl.BlockSpec(head_w.shape, lambda i, f: (0, 0, 0)))
        args.append(head_w)
    in_specs += [pl.BlockSpec((1, d), const), pl.BlockSpec((d, tf), lambda i, f: (0, f)),
                 pl.BlockSpec((tf, d), lambda i, f: (f, 0))]
    args += [norm_w.reshape(1, d), w_up, w_down]
    if final:
        in_specs.append(pl.BlockSpec((1, d), const))
        args.append(norm_final.reshape(1, d))
    return pl.pallas_call(
        functools.partial(_proj_mlp_kernel, len(projs), n_heads, final),
        grid=(pl.cdiv(n, tm), ff // tf),
        in_specs=in_specs,
        out_specs=pl.BlockSpec((tm, d), row),
        out_shape=jax.ShapeDtypeStruct((n, d), F32),
        scratch_shapes=[pltpu.VMEM((tm, d), F32), pltpu.VMEM((tm, d), BF16), pltpu.VMEM((tm, d), F32)],
        compiler_params=pltpu.CompilerParams(dimension_semantics=("parallel", "arbitrary"),
                                             vmem_limit_bytes=VMEM_LIMIT),
        name="proj_mlp",
    )(*args)


def _gdn_gates(sm, alog_ref, dtb_ref):
    zs = sm + dtb_ref[...]
    g = -jnp.exp(alog_ref[...]) * (jnp.maximum(zs, 0.0) + jnp.log(1.0 + jnp.exp(-jnp.abs(zs))))
    return g, jax.nn.sigmoid(sm)


def _head_group_matrix():
    gi = lax.broadcasted_iota(jnp.int32, (HD, HD), 0) // DK_A
    gj = lax.broadcasted_iota(jnp.int32, (HD, HD), 1) // DK_A
    return jnp.where(gi == gj, 1.0, 0.0).astype(BF16)


def _gdn_kernel(c_sz, pad, qkv_ref, small_ref, gate_ref, convw_ref, alog_ref, dtb_ref, gnorm_ref,
                o_ref, sfin_ref, s_s, prev_s):
    c = pl.program_id(1)

    @pl.when(c == 0)
    def _():
        s_s[...] = jnp.zeros_like(s_s)
        prev_s[...] = jnp.zeros_like(prev_s)

    row = lax.broadcasted_iota(jnp.int32, (c_sz, 1), 0)
    valid = (row + c * c_sz) >= pad
    x = jnp.where(valid, qkv_ref[0], 0.0)
    prevx = prev_s[...]
    acc = x * convw_ref[CONV_W - 1:CONV_W, :]
    for sft in range(1, CONV_W):
        shifted = jnp.where(row < sft, pltpu.roll(prevx, sft, 0), pltpu.roll(x, sft, 0))
        acc = acc + shifted * convw_ref[CONV_W - 1 - sft:CONV_W - sft, :]
    prev_s[...] = x
    y = acc * jax.nn.sigmoid(acc)
    q, k, v = y[:, :HD], y[:, HD:2 * HD], y[:, 2 * HD:]
    gmat = _head_group_matrix()
    ss = jnp.dot(jnp.concatenate([q * q, k * k], axis=0).astype(BF16), gmat, preferred_element_type=F32)
    q = q * lax.rsqrt(ss[:c_sz] + EPS) * (DK_A ** -0.5)
    k = k * lax.rsqrt(ss[c_sz:] + EPS)

    g, beta = _gdn_gates(small_ref[0], alog_ref, dtb_ref)
    g = jnp.where(valid, g, 0.0)
    beta = jnp.where(valid, beta, 0.0)
    ti = lax.broadcasted_iota(jnp.int32, (c_sz, c_sz), 0)
    tj = lax.broadcasted_iota(jnp.int32, (c_sz, c_sz), 1)
    incl = ti >= tj
    strict = ti > tj
    gc = _hdot(jnp.where(incl, 1.0, 0.0), g)
    gct = gc.T
    ei = lax.broadcasted_iota(jnp.int32, (LANE, HD), 0)
    ej = lax.broadcasted_iota(jnp.int32, (LANE, HD), 1) // DK_A
    gc_w = _hdot(gc, jnp.where(ei == ej + SM_ALPHA, 1.0, 0.0))
    beta_w = _dot_exact_rhs(beta, jnp.where(ei == ej + SM_BRAW, 1.0, 0.0).astype(BF16))
    gl_w = gc_w[c_sz - 1:c_sz, :]
    eg_w = jnp.exp(gc_w)
    q_s = q * eg_w
    k_b = k * (beta_w * eg_w)
    v_b = v * beta_w
    k_l = k * jnp.exp(gl_w - gc_w)
    s_old = s_s[...]
    s_old_b = s_old.astype(BF16)
    n_fac = max(1, (c_sz - 1).bit_length())
    sls = [slice(h * DK_A, (h + 1) * DK_A) for h in range(H_A)]
    k16, q16 = k.astype(BF16), q.astype(BF16)
    decays, ms, xss = [], [], []
    for h, sl in enumerate(sls):
        diff = gc[:, SM_ALPHA + h:SM_ALPHA + h + 1] - gct[SM_ALPHA + h:SM_ALPHA + h + 1, :]
        decay = jnp.exp(jnp.where(incl, diff, -1e30))
        kk = lax.dot_general(k16[:, sl], k16[:, sl], NT_DIMS, preferred_element_type=F32)
        ms.append(-jnp.where(strict, beta[:, SM_BRAW + h:SM_BRAW + h + 1] * kk * decay, 0.0))
        xss.append(jnp.concatenate([v_b[:, sl], k_b[:, sl]], axis=1))
        decays.append(decay)
    for s in range(n_fac):
        msp = [_split2(m) for m in ms]
        xss = [x + _dot3(mp, _split2(x)) for x, mp in zip(xss, msp)]
        if s < n_fac - 1:
            ms = [_dot3(mp, mp) for mp in msp]
    outs, s_upd = [], []
    for h, sl in enumerate(sls):
        xs = xss[h]
        sh = s_old_b[:, sl]
        v_new = xs[:, :DV_A] - jnp.dot(xs[:, DV_A:].astype(BF16), sh, preferred_element_type=F32)
        v_new_b = v_new.astype(BF16)
        qk = lax.dot_general(q16[:, sl], k16[:, sl], NT_DIMS, preferred_element_type=F32) * decays[h]
        outs.append(jnp.dot(q_s[:, sl].astype(BF16), sh, preferred_element_type=F32)
                    + jnp.dot(qk.astype(BF16), v_new_b, preferred_element_type=F32))
        s_upd.append(lax.dot_general(k_l[:, sl].astype(BF16), v_new_b, TN_DIMS, preferred_element_type=F32))
    s_new = s_old * jnp.exp(gl_w) + jnp.concatenate(s_upd, axis=1)
    s_s[...] = s_new
    o = jnp.concatenate(outs, axis=1)
    ms_o = jnp.dot((o * o).astype(BF16), gmat, preferred_element_type=F32) * (1.0 / DV_A)
    gate = gate_ref[0]
    o_ref[0] = (o * lax.rsqrt(ms_o + EPS) * gnorm_ref[...] * (gate * jax.nn.sigmoid(gate))).astype(o_ref.dtype)

    @pl.when(c == pl.num_programs(1) - 1)
    def _():
        for h in range(H_A):
            sfin_ref[0, h] = s_new[:, h * DV_A:(h + 1) * DV_A]


def _gdn_params(a_log, dt_bias, gnorm):
    alog_p = jnp.zeros((1, LANE), F32).at[0, SM_ALPHA:SM_ALPHA + H_A].set(a_log)
    dtb_p = jnp.zeros((1, LANE), F32).at[0, SM_ALPHA:SM_ALPHA + H_A].set(dt_bias)
    return alog_p, dtb_p, jnp.tile(gnorm, H_A).reshape(1, HD)


def gdn_prompt(proj, conv_w, a_log, dt_bias, gnorm, pad, c_sz=LANE):
    b, l, _ = proj.shape
    blk = lambda w, col: pl.BlockSpec((1, c_sz, w), lambda bi, ci: (bi, ci, col // w))
    const = lambda shp: pl.BlockSpec(shp, lambda bi, ci: (0, 0))
    return pl.pallas_call(
        functools.partial(_gdn_kernel, c_sz, pad),
        grid=(b, l // c_sz),
        in_specs=[blk(QKV_A, COL_QKV), blk(LANE, COL_SMALL), blk(HD, COL_GATE), const((CONV_W, QKV_A)),
                  const((1, LANE)), const((1, LANE)), const((1, HD))],
        out_specs=[pl.BlockSpec((1, c_sz, HD), lambda bi, ci: (bi, ci, 0)),
                   pl.BlockSpec((1, H_A, DK_A, DV_A), lambda bi, ci: (bi, 0, 0, 0))],
        out_shape=[jax.ShapeDtypeStruct((b, l, HD), BF16), jax.ShapeDtypeStruct((b, H_A, DK_A, DV_A), F32)],
        scratch_shapes=[pltpu.VMEM((DK_A, HD), F32), pltpu.VMEM((c_sz, QKV_A), F32)],
        compiler_params=pltpu.CompilerParams(dimension_semantics=("parallel", "arbitrary"),
                                             vmem_limit_bytes=VMEM_LIMIT),
        name="gdn_prompt",
    )(proj, proj, proj, conv_w, *_gdn_params(a_log, dt_bias, gnorm))


def bias_tiles(rel_bias):
    d = jnp.arange(-(LANE - 1), N_BIAS_TILES * LANE, dtype=jnp.int32)
    tab = rel_bias[rel_bucket(d)].astype(F32).T
    tiles = []
    for dl in range(N_BIAS_TILES):
        w = jnp.pad(tab[:, dl * LANE:dl * LANE + 2 * LANE - 1], ((0, 0), (0, 1)))
        sh = jnp.tile(w, (1, LANE))[:, :LANE * (2 * LANE - 1)].reshape(-1, LANE, 2 * LANE - 1)
        tiles.append(sh[:, :, LANE - 1:])
    return jnp.stack(tiles, axis=1)


def _select_topk(sc, valid, krow, topk, idx_bits):
    key = jnp.where(valid, _sortable(sc), INT_MIN)
    kf = jnp.float32(topk)

    def count(m):
        return _col_reduce(jnp.where(m, 1.0, 0.0), jnp.sum)

    t0 = jnp.where(count(key >= 0) >= kf, 0, INT_MIN).astype(jnp.int32)

    def tbody(n, t):
        cand = t | jnp.left_shift(jnp.int32(1), 30 - n)
        return jnp.where(count(key >= cand) >= kf, cand, t)

    t = lax.fori_loop(0, 31, tbody, t0)
    gt = key > t
    eq = key == t
    m = kf - count(gt)

    def jbody(n, j):
        cand = j | jnp.left_shift(jnp.int32(1), idx_bits - 1 - n)
        return jnp.where(count(eq & (krow < cand)) <= m, cand, j)

    split = jnp.max(count(eq) - m) > 0.0
    j = lax.cond(split, lambda: lax.fori_loop(0, idx_bits, jbody, jnp.zeros_like(t)),
                 lambda: jnp.full_like(t, 1 << idx_bits))
    return valid & (gt | (eq & (krow < j)))


def _dsa_prompt_body(lc, pad, topk, q_ref, qi_ref, smq_ref, k_ref, v_ref, smk_ref, bias_ref, o_ref):
    i = pl.program_id(1)
    nkb = lc // LANE
    ki = smk_ref[0, :lc, SM_KI:SM_KI + D_IDX].astype(BF16)
    smt = smq_ref[0].T
    sc = jnp.zeros((lc, LANE), F32)
    for h in range(H_IDX):
        qih = qi_ref[0, :, h * D_IDX:(h + 1) * D_IDX].astype(BF16)
        d = lax.dot_general(ki, qih, NT_DIMS, preferred_element_type=F32)
        w = smt[SM_WI + h:SM_WI + h + 1, :] * (H_IDX ** -0.5 * D_IDX ** -0.5)
        sc = sc + jnp.maximum(d, 0.0) * w
    krow = lax.broadcasted_iota(jnp.int32, (lc, LANE), 0)
    qrow = i * LANE + lax.broadcasted_iota(jnp.int32, (lc, LANE), 1)
    valid = (krow <= qrow) & (krow >= pad)
    sel = _select_topk(sc, valid, krow, topk, max(1, (lc - 1).bit_length()))
    outs = []
    for h in range(H_B):
        sl = slice(h * DH_B, (h + 1) * DH_B)
        kh = k_ref[0, :lc, sl].astype(BF16)
        qh = (q_ref[0, :, sl] * (DH_B ** -0.5)).astype(BF16)
        lg = lax.dot_general(kh, qh, NT_DIMS, preferred_element_type=F32)
        bias = jnp.concatenate([bias_ref[h, jnp.clip(i - jb, 0, N_BIAS_TILES - 1)] for jb in range(nkb)], axis=0)
        lg = jnp.where(sel, lg + bias, -1e30)
        p = jnp.exp(lg - _col_reduce(lg, jnp.max))
        p = (p * (1.0 / _col_reduce(p, jnp.sum))).astype(BF16)
        vh = v_ref[0, :lc, sl].astype(BF16)
        outs.append(lax.dot_general(p, vh, TN_DIMS, preferred_element_type=F32))
    o_ref[0] = jnp.concatenate(outs, axis=1).astype(o_ref.dtype)


def _dsa_prompt_kernel(variants, pad, topk, *refs):
    _dispatch_variants(variants, _dsa_prompt_body, pad, topk, *refs)


def dsa_prompt(proj, btiles, pad, topk):
    b, l, _ = proj.shape
    nq = l // LANE
    qblk = lambda w, col: pl.BlockSpec((1, LANE, w), lambda bi, i: (bi, i, col // w))
    full = lambda w, col: pl.BlockSpec((1, l, w), lambda bi, i: (bi, 0, col // w))
    return pl.pallas_call(
        functools.partial(_dsa_prompt_kernel, _causal_variants(nq, 4), pad, topk),
        grid=(b, nq),
        in_specs=[qblk(HD, COL_QB), qblk(HD, COL_QI), qblk(LANE, COL_SMALL),
                  full(HD, COL_KB), full(HD, COL_VB), full(LANE, COL_SMALL),
                  pl.BlockSpec(btiles.shape, lambda bi, i: (0, 0, 0, 0))],
        out_specs=pl.BlockSpec((1, LANE, HD), lambda bi, i: (bi, i, 0)),
        out_shape=jax.ShapeDtypeStruct((b, l, HD), BF16),
        compiler_params=pltpu.CompilerParams(dimension_semantics=("parallel", "arbitrary"),
                                             vmem_limit_bytes=VMEM_LIMIT),
        name="dsa_prompt",
    )(proj, proj, proj, proj, proj, proj, btiles)


def _mla_prep_kernel(h_ref, nm_ref, win_ref, qn_ref, kvn_ref, wuq_ref, wuk_ref, cos_ref, sin_ref, q_ref, kv_ref):
    x = _rms(h_ref[0], nm_ref[...]).astype(BF16)
    p = jnp.dot(x, win_ref[...], preferred_element_type=F32)
    cqn = _rms(p[:, CC_Q:CC_KV], qn_ref[...]).astype(BF16)
    qf = jnp.dot(cqn, wuq_ref[...], preferred_element_type=F32)
    cos, sin = cos_ref[...], sin_ref[...]
    q1, q2 = qf[:, H_C * LANE:(H_C + 1) * LANE], qf[:, (H_C + 1) * LANE:]
    q1r = q1 * cos - q2 * sin
    q2r = q2 * cos + q1 * sin
    head_of_lane = lax.broadcasted_iota(jnp.int32, (1, LANE), 1) // HALF
    for h in range(H_C):
        qlat = jnp.dot(qf[:, h * LANE:(h + 1) * LANE].astype(BF16), wuk_ref[h], preferred_element_type=F32)
        mine = head_of_lane == h
        q_ref[0, h] = jnp.concatenate([qlat, jnp.where(mine, q1r, 0.0), jnp.where(mine, q2r, 0.0)],
                                      axis=1).astype(q_ref.dtype)
    c = _rms(p[:, CC_KV:CC_K1], kvn_ref[...])
    k1, k2 = p[:, CC_K1:CC_K2], p[:, CC_K2:]
    kv_ref[0] = jnp.concatenate([c, k1 * cos - k2 * sin, k2 * cos + k1 * sin], axis=1)


def mla_prep(h, norm_w, w_in_p, qnorm, kvnorm, w_uq_p, w_uk_p, cos, sin):
    b, l, d = h.shape
    tm = min(LANE, l)
    const2 = lambda shp: pl.BlockSpec(shp, lambda bi, i: (0, 0))
    return pl.pallas_call(
        _mla_prep_kernel,
        grid=(b, l // tm),
        in_specs=[pl.BlockSpec((1, tm, d), lambda bi, i: (bi, i, 0)), const2((1, d)), const2(w_in_p.shape),
                  const2((1, Q_LORA)), const2((1, KV_LORA)), const2(w_uq_p.shape),
                  pl.BlockSpec(w_uk_p.shape, lambda bi, i: (0, 0, 0)),
                  pl.BlockSpec((tm, LANE), lambda bi, i: (i, 0)), pl.BlockSpec((tm, LANE), lambda bi, i: (i, 0))],
        out_specs=[pl.BlockSpec((1, H_C, tm, KVW), lambda bi, i: (bi, 0, i, 0)),
                   pl.BlockSpec((1, tm, KVW), lambda bi, i: (bi, i, 0))],
        out_shape=[jax.ShapeDtypeStruct((b, H_C, l, KVW), BF16), jax.ShapeDtypeStruct((b, l, KVW), F32)],
        compiler_params=pltpu.CompilerParams(dimension_semantics=("parallel", "arbitrary"),
                                             vmem_limit_bytes=VMEM_LIMIT),
        name="mla_prep",
    )(h, norm_w.reshape(1, d), w_in_p, qnorm.reshape(1, -1), kvnorm.reshape(1, -1), w_uq_p, w_uk_p, cos, sin)


def _mla_body(lc, pad, q_ref, kv_ref, o_ref):
    i = pl.program_id(1)
    q = q_ref[0].reshape(H_C * LANE, KVW)
    kb = kv_ref[0, :lc, :].astype(BF16)
    s = lax.dot_general(q, kb, NT_DIMS, preferred_element_type=F32) * MLA_SCALE
    s = s.reshape(H_C, LANE, lc)
    krow = lax.broadcasted_iota(jnp.int32, (LANE, lc), 1)
    qrow = i * LANE + lax.broadcasted_iota(jnp.int32, (LANE, lc), 0)
    ok = (krow <= qrow) & (krow >= pad)
    s = jnp.where(ok[None], s, -1e30)
    mx = jnp.max(s, axis=-1, keepdims=True)
    p = jnp.exp(s - mx)
    inv_l = 1.0 / jnp.sum(p, axis=-1, keepdims=True)
    o = jnp.dot(p.astype(BF16).reshape(H_C * LANE, lc), kb[:, :KV_LORA], preferred_element_type=F32)
    o = (o.reshape(H_C, LANE, KV_LORA) * inv_l).astype(o_ref.dtype)
    for h in range(H_C):
        o_ref[0, :, h * KV_LORA:(h + 1) * KV_LORA] = o[h]


def _mla_kernel(variants, pad, *refs):
    _dispatch_variants(variants, _mla_body, pad, *refs)


def mla_prompt(q, kv, pad):
    b, _, l, _ = q.shape
    nq = l // LANE
    return pl.pallas_call(
        functools.partial(_mla_kernel, _causal_variants(nq, 8), pad),
        grid=(b, nq),
        in_specs=[pl.BlockSpec((1, H_C, LANE, KVW), lambda bi, i: (bi, 0, i, 0)),
                  pl.BlockSpec((1, l, KVW), lambda bi, i: (bi, 0, 0))],
        out_specs=pl.BlockSpec((1, LANE, H_C * KV_LORA), lambda bi, i: (bi, i, 0)),
        out_shape=jax.ShapeDtypeStruct((b, l, H_C * KV_LORA), BF16),
        compiler_params=pltpu.CompilerParams(dimension_semantics=("parallel", "arbitrary"),
                                             vmem_limit_bytes=VMEM_LIMIT),
        name="mla_prompt",
    )(q, kv)


def _gdn_decode_kernel(x_ref, conv_ref, small_ref, gate_ref, s_ref, convw_ref, alog_ref, dtb_ref, gnorm_ref,
                       o_ref, convo_ref, so_ref):
    nb = x_ref.shape[0]
    x = x_ref[...]
    cb = conv_ref[0]
    acc = x * convw_ref[CONV_W - 1:CONV_W, :]
    for j in range(CONV_W - 1):
        acc = acc + cb[:, j, :] * convw_ref[j:j + 1, :]
    for j in range(CONV_W - 2):
        convo_ref[:, j, :] = cb[:, j + 1, :]
    convo_ref[:, CONV_W - 2, :] = x
    y = acc * jax.nn.sigmoid(acc)
    q, k, v = y[:, :HD], y[:, HD:2 * HD], y[:, 2 * HD:]
    gmat = _head_group_matrix()
    q = q * lax.rsqrt(_dot_exact_rhs(q * q, gmat) + EPS) * (DK_A ** -0.5)
    k = k * lax.rsqrt(_dot_exact_rhs(k * k, gmat) + EPS)
    g, beta = _gdn_gates(small_ref[...], alog_ref, dtb_ref)
    eg = jnp.exp(g)
    rowid = lax.broadcasted_iota(jnp.int32, (nb, 1), 0)

    def body(bb, o_acc):
        mine = rowid == bb
        outs = []
        for h in range(H_A):
            sl = slice(h * DK_A, (h + 1) * DK_A)
            s = s_ref[0, bb, h]
            egh = eg[:, SM_ALPHA + h:SM_ALPHA + h + 1]
            eg_b = jnp.sum(jnp.where(mine, egh, 0.0), axis=0, keepdims=True)
            delta = (v[:, sl] - _hdot(k[:, sl], s) * egh) * beta[:, SM_BRAW + h:SM_BRAW + h + 1]
            s_new = s * eg_b + lax.dot_general(jnp.where(mine, k[:, sl], 0.0), jnp.where(mine, delta, 0.0), TN_DIMS,
                                               precision=HI, preferred_element_type=F32)
            so_ref[bb, h] = s_new
            outs.append(_hdot(q[:, sl], s_new))
        return jnp.where(mine, jnp.concatenate(outs, axis=1), o_acc)

    o = lax.fori_loop(0, nb, body, jnp.zeros((nb, HD), F32))
    ms_o = _dot_exact_rhs(o * o, gmat) * (1.0 / DV_A)
    gate = gate_ref[...]
    o_ref[...] = (o * lax.rsqrt(ms_o + EPS) * gnorm_ref[...] * (gate * jax.nn.sigmoid(gate))).astype(o_ref.dtype)


def gdn_decode_step(proj, conv_state, ssm_state, layer, conv_w, a_log, dt_bias, gnorm, nb=8):
    b = proj.shape[0]
    blk = lambda w, col: pl.BlockSpec((nb, w), lambda i: (i, col // w))
    const = lambda shp: pl.BlockSpec(shp, lambda i: (0, 0))
    return pl.pallas_call(
        _gdn_decode_kernel,
        grid=(b // nb,),
        in_specs=[blk(QKV_A, COL_QKV), pl.BlockSpec((1, nb, CONV_W - 1, QKV_A), lambda i: (layer, i, 0, 0)),
                  blk(LANE, COL_SMALL), blk(HD, COL_GATE),
                  pl.BlockSpec((1, nb, H_A, DK_A, DV_A), lambda i: (layer, i, 0, 0, 0)),
                  const((CONV_W, QKV_A)), const((1, LANE)), const((1, LANE)), const((1, HD))],
        out_specs=[pl.BlockSpec((nb, HD), lambda i: (i, 0)),
                   pl.BlockSpec((nb, CONV_W - 1, QKV_A), lambda i: (i, 0, 0)),
                   pl.BlockSpec((nb, H_A, DK_A, DV_A), lambda i: (i, 0, 0, 0))],
        out_shape=[jax.ShapeDtypeStruct((b, HD), BF16), jax.ShapeDtypeStruct(conv_state.shape[1:], F32),
                   jax.ShapeDtypeStruct(ssm_state.shape[1:], F32)],
        compiler_params=pltpu.CompilerParams(dimension_semantics=("parallel",), vmem_limit_bytes=VMEM_LIMIT),
        name="gdn_decode",
    )(proj, conv_state, proj, proj, ssm_state, conv_w, *_gdn_params(a_log, dt_bias, gnorm))


def _dsa_scores_kernel(pt_ref, qi_ref, small_ref, *rest):
    ki_refs = rest[:SEL_DEC_PAGES]
    sc_ref, snew_ref = rest[SEL_DEC_PAGES:]
    j = pl.program_id(1)
    qi8 = _rows_to_heads(qi_ref[0], D_IDX).astype(BF16)
    sm = small_ref[0]
    pick = (lax.broadcasted_iota(jnp.int32, (H_IDX, LANE), 1)
            == lax.broadcasted_iota(jnp.int32, (H_IDX, LANE), 0) + SM_WI)
    w8 = jnp.sum(jnp.where(pick, sm, 0.0), axis=1, keepdims=True) * (H_IDX ** -0.5)
    kit = jnp.concatenate([r[0, 0] for r in ki_refs], axis=1).astype(BF16)
    d = jnp.dot(qi8, kit, preferred_element_type=F32)
    s = jnp.sum(jnp.maximum(d * (D_IDX ** -0.5), 0.0) * w8, axis=0, keepdims=True)
    for g in range(SEL_DEC_PAGES):
        sc_ref[0, pl.ds(j * SEL_DEC_PAGES + g, 1), :] = s[:, g * PAGE_SIZE:(g + 1) * PAGE_SIZE]

    @pl.when(j == pl.num_programs(1) - 1)
    def _():
        ki_new = sm[:, SM_KI:SM_KI + D_IDX].astype(BF16).astype(F32)
        d_new = jnp.sum(qi8.astype(F32) * ki_new, axis=1, keepdims=True)
        s_new = jnp.sum(jnp.maximum(d_new * (D_IDX ** -0.5), 0.0) * w8, axis=0, keepdims=True)
        snew_ref[0] = jnp.broadcast_to(s_new, (1, LANE))


def _dsa_pick_kernel(topk, sc_ref, snew_ref, mask_ref, mnew_ref):
    nb, n_pages, _ = sc_ref.shape
    past = n_pages * PAGE_SIZE
    key = _sortable(sc_ref[...])
    key_n = _sortable(snew_ref[...][:, :, 0:1])
    idx = (lax.broadcasted_iota(jnp.int32, key.shape, 1) * PAGE_SIZE
           + lax.broadcasted_iota(jnp.int32, key.shape, 2))
    kf = jnp.float32(topk)

    def count(m, mn):
        c = jnp.sum(jnp.sum(jnp.where(m, 1.0, 0.0), axis=1, keepdims=True), axis=2, keepdims=True)
        return c + jnp.where(mn, 1.0, 0.0)

    t0 = jnp.where(count(key >= 0, key_n >= 0) >= kf, 0, INT_MIN).astype(jnp.int32)

    def tbody(n, t):
        cand = t | jnp.left_shift(jnp.int32(1), 30 - n)
        return jnp.where(count(key >= cand, key_n >= cand) >= kf, cand, t)

    t = lax.fori_loop(0, 31, tbody, t0)
    m = kf - count(key > t, key_n > t)
    idx_bits = past.bit_length()

    def jbody(n, jj):
        cand = jj | jnp.left_shift(jnp.int32(1), idx_bits - 1 - n)
        return jnp.where(count((key == t) & (idx < cand), (key_n == t) & (past < cand)) <= m, cand, jj)

    jj = lax.fori_loop(0, idx_bits, jbody, jnp.zeros_like(t))
    mask_ref[...] = jnp.where((key > t) | ((key == t) & (idx < jj)), 1.0, 0.0)
    sel_n = (key_n > t) | ((key_n == t) & (past < jj))
    lane0 = lax.broadcasted_iota(jnp.int32, (nb, 1, LANE), 2) == 0
    mnew_ref[...] = jnp.where(lane0 & sel_n, 1.0, 0.0)


def dsa_decode_select(proj3, cache_kit, layer, page_table, topk):
    b = proj3.shape[0]
    n_pages = page_table.shape[1]
    page_spec = lambda g: pl.BlockSpec((1, 1, D_IDX, PAGE_SIZE),
                                       lambda bi, j, pt: (layer, pt[bi, j * SEL_DEC_PAGES + g], 0, 0))
    out_specs = [pl.BlockSpec((1, n_pages, PAGE_SIZE), lambda bi, j, pt: (bi, 0, 0)),
                 pl.BlockSpec((1, 1, LANE), lambda bi, j, pt: (bi, 0, 0))]
    out_shape = [jax.ShapeDtypeStruct((b, n_pages, PAGE_SIZE), F32), jax.ShapeDtypeStruct((b, 1, LANE), F32)]
    grid_spec = pltpu.PrefetchScalarGridSpec(
        num_scalar_prefetch=1,
        grid=(b, n_pages // SEL_DEC_PAGES),
        in_specs=[pl.BlockSpec((1, 1, HD), lambda bi, j, pt: (bi, 0, COL_QI // HD)),
                  pl.BlockSpec((1, 1, LANE), lambda bi, j, pt: (bi, 0, COL_SMALL // LANE))]
                 + [page_spec(g) for g in range(SEL_DEC_PAGES)],
        out_specs=out_specs,
    )
    scores, s_new = pl.pallas_call(
        _dsa_scores_kernel,
        grid_spec=grid_spec,
        out_shape=out_shape,
        compiler_params=pltpu.CompilerParams(dimension_semantics=("parallel", "arbitrary"),
                                             vmem_limit_bytes=VMEM_LIMIT),
        name="dsa_decode_scores",
    )(page_table, proj3, proj3, *([cache_kit] * SEL_DEC_PAGES))
    whole = lambda shp: pl.BlockSpec(shp, lambda i: (0, 0, 0))
    return pl.pallas_call(
        functools.partial(_dsa_pick_kernel, topk),
        grid=(1,),
        in_specs=[whole(scores.shape), whole(s_new.shape)],
        out_specs=[whole(scores.shape), whole(s_new.shape)],
        out_shape=out_shape,
        compiler_params=pltpu.CompilerParams(dimension_semantics=("arbitrary",), vmem_limit_bytes=VMEM_LIMIT),
        name="dsa_decode_pick",
    )(scores, s_new)


def _dsa_attend_kernel(n_pages, pt_ref, thr_ref, q_ref, knt_ref, vnt_ref, mask_ref, mnew_ref, rb_ref, *rest):
    kt_refs, vt_refs = rest[:DEC_PAGES], rest[DEC_PAGES:2 * DEC_PAGES]
    o_ref, qbd_s, m_s, l_s, acc_s = rest[2 * DEC_PAGES:]
    j = pl.program_id(1)
    past = n_pages * PAGE_SIZE
    eye = lax.broadcasted_iota(jnp.int32, (H_B, LANE), 0) == lax.broadcasted_iota(jnp.int32, (H_B, LANE), 1)

    @pl.when(j == 0)
    def _():
        bd = lax.broadcasted_iota(jnp.int32, (H_B, HD), 1) // DH_B == lax.broadcasted_iota(jnp.int32, (H_B, HD), 0)
        qbd_s[...] = jnp.where(bd, jnp.broadcast_to(q_ref[0], (H_B, HD)), 0.0).astype(BF16)
        m_s[...] = jnp.full_like(m_s, -1e30)
        l_s[...] = jnp.zeros_like(l_s)
        acc_s[...] = jnp.zeros_like(acc_s)

    def attend(kts, vts, ok, first_key):
        n = len(kts)
        lg = jnp.concatenate([jnp.dot(qbd_s[...], kt.astype(BF16), preferred_element_type=F32) for kt in kts], axis=1)
        dist = past - first_key - lax.broadcasted_iota(jnp.int32, (1, n * PAGE_SIZE), 1)
        bias = jnp.broadcast_to(rb_ref[0], lg.shape)
        for bk in range(1, REL_BUCKETS):
            bias = jnp.where(dist >= thr_ref[bk], rb_ref[bk], bias)
        lg = jnp.where(ok, lg * (DH_B ** -0.5) + bias, -1e30)
        m_new = jnp.maximum(m_s[...], jnp.max(lg, axis=1, keepdims=True))
        alpha = jnp.exp(m_s[...] - m_new)
        p = jnp.where(ok, jnp.exp(lg - m_new), 0.0)
        l_s[...] = l_s[...] * alpha + jnp.sum(p, axis=1, keepdims=True)
        m_s[...] = m_new
        pb = p.astype(BF16)
        pv = [lax.dot_general(vts[g].astype(BF16), pb[:, g * PAGE_SIZE:(g + 1) * PAGE_SIZE], NT_DIMS,
                              preferred_element_type=F32) for g in range(n)]
        while len(pv) > 1:
            pv = [pv[i] + pv[i + 1] if i + 1 < len(pv) else pv[i] for i in range(0, len(pv), 2)]
        alpha_row = jnp.sum(jnp.where(eye, alpha, 0.0), axis=0, keepdims=True)
        acc_s[...] = acc_s[...] * alpha_row[:, :H_B] + pv[0]

    rows = mask_ref[0, pl.ds(j * DEC_PAGES, DEC_PAGES), :]
    ok = jnp.concatenate([rows[g:g + 1] for g in range(DEC_PAGES)], axis=1) > 0.0
    attend([r[0, 0].reshape(HD, PAGE_SIZE) for r in kt_refs], [r[0, 0].reshape(HD, PAGE_SIZE) for r in vt_refs],
           ok, j * (DEC_PAGES * PAGE_SIZE))

    @pl.when(j == pl.num_programs(1) - 1)
    def _():
        attend([knt_ref[0]], [vnt_ref[0]], mnew_ref[0] > 0.0, past)
        l_row = jnp.sum(jnp.where(eye, l_s[...], 0.0), axis=0, keepdims=True)
        o = acc_s[...] * (1.0 / l_row[:, :H_B])
        own = (lax.broadcasted_iota(jnp.int32, (HD, H_B), 0) // DH_B
               == lax.broadcasted_iota(jnp.int32, (HD, H_B), 1))
        o_ref[0] = jnp.sum(jnp.where(own, o, 0.0), axis=1, keepdims=True).astype(o_ref.dtype)


def _decode_thresholds(past):
    buckets = rel_bucket(jnp.arange(past + 1, dtype=jnp.int32))
    return jnp.sum(buckets[None, :] < jnp.arange(REL_BUCKETS, dtype=jnp.int32)[:, None], axis=1).astype(jnp.int32)


def dsa_decode_attend(proj, mask, mnew, cache_kt, cache_vt, layer, page_table, rel_bias):
    b = proj.shape[0]
    n_pages = page_table.shape[1]
    past = n_pages * PAGE_SIZE
    thr = _decode_thresholds(past)
    rb = rel_bias.astype(F32)[:, :, None]
    proj3 = proj.reshape(b, 1, AB_COLS_P)
    one_key_page = lambda col: jnp.pad(proj[:, col:col + HD, None], ((0, 0), (0, 0), (0, PAGE_SIZE - 1)))
    page_spec = lambda g: pl.BlockSpec((1, 1, H_B, DH_B, PAGE_SIZE),
                                       lambda bi, j, pt, th: (layer, pt[bi, j * DEC_PAGES + g], 0, 0, 0))
    per_b = lambda shp: pl.BlockSpec((1,) + shp, lambda bi, j, pt, th: (bi, 0, 0))
    grid_spec = pltpu.PrefetchScalarGridSpec(
        num_scalar_prefetch=2,
        grid=(b, n_pages // DEC_PAGES),
        in_specs=[pl.BlockSpec((1, 1, HD), lambda bi, j, pt, th: (bi, 0, COL_QB // HD)),
                  per_b((HD, PAGE_SIZE)), per_b((HD, PAGE_SIZE)), per_b((n_pages, PAGE_SIZE)), per_b((1, LANE)),
                  pl.BlockSpec(rb.shape, lambda bi, j, pt, th: (0, 0, 0))]
                 + [page_spec(g) for g in range(DEC_PAGES)] * 2,
        out_specs=per_b((HD, 1)),
        scratch_shapes=[pltpu.VMEM((H_B, HD), BF16), pltpu.VMEM((H_B, 1), F32), pltpu.VMEM((H_B, 1), F32),
                        pltpu.VMEM((HD, H_B), F32)],
    )
    return pl.pallas_call(
        functools.partial(_dsa_attend_kernel, n_pages),
        grid_spec=grid_spec,
        out_shape=jax.ShapeDtypeStruct((b, HD, 1), F32),
        compiler_params=pltpu.CompilerParams(dimension_semantics=("parallel", "arbitrary"),
                                             vmem_limit_bytes=VMEM_LIMIT),
        name="dsa_decode_attend",
    )(page_table, thr, proj3, one_key_page(COL_KB), one_key_page(COL_VB), mask, mnew, rb,
      *([cache_kt] * DEC_PAGES), *([cache_vt] * DEC_PAGES))


def _online_update(lg, m_s, l_s):
    m_new = jnp.maximum(m_s[...], jnp.max(lg, axis=1, keepdims=True))
    alpha = jnp.exp(m_s[...] - m_new)
    p = jnp.exp(lg - m_new)
    l_s[...] = l_s[...] * alpha + jnp.sum(p, axis=1, keepdims=True)
    m_s[...] = m_new
    return p, alpha


def _mla_decode_kernel(pt_ref, q_ref, kvn_ref, exp_ref, *rest):
    c_refs, kpe_refs = rest[:MLA_DEC_PAGES], rest[MLA_DEC_PAGES:2 * MLA_DEC_PAGES]
    o_ref, m_s, l_s, acc_s = rest[2 * MLA_DEC_PAGES:]
    j = pl.program_id(1)

    @pl.when(j == 0)
    def _():
        m_s[...] = jnp.full_like(m_s, -1e30)
        l_s[...] = jnp.zeros_like(l_s)
        acc_s[...] = jnp.zeros_like(acc_s)

    q = q_ref[0]
    q_lat = q[:, :KV_LORA]
    q_pe = jnp.dot(q[:, KV_LORA:], exp_ref[...], preferred_element_type=F32).astype(BF16)
    cb = jnp.concatenate([r[0, 0] for r in c_refs], axis=0).astype(BF16)
    kpet = jnp.concatenate([r[0, 0] for r in kpe_refs], axis=1).astype(BF16)
    lg = (lax.dot_general(q_lat, cb, NT_DIMS, preferred_element_type=F32)
          + jnp.dot(q_pe, kpet, preferred_element_type=F32)) * MLA_SCALE
    p, alpha = _online_update(lg, m_s, l_s)
    acc_s[...] = acc_s[...] * alpha + jnp.dot(p.astype(BF16), cb, preferred_element_type=F32)

    @pl.when(j == pl.num_programs(1) - 1)
    def _():
        kvn = kvn_ref[0].astype(BF16).astype(F32)
        lgn = jnp.sum(q.astype(F32) * kvn, axis=1, keepdims=True) * MLA_SCALE
        pn, alpha_n = _online_update(lgn, m_s, l_s)
        acc = acc_s[...] * alpha_n + pn.astype(BF16).astype(F32) * kvn[:, :KV_LORA]
        o = acc * (1.0 / l_s[...])
        o_ref[0] = jnp.concatenate([o[h:h + 1] for h in range(H_C)], axis=1).astype(o_ref.dtype)


def mla_decode(q, kvn, cache_c, cache_kpet, layer, page_table):
    b = q.shape[0]
    n_pages = page_table.shape[1]
    src = jnp.concatenate([jnp.tile(jnp.arange(HALF), H_C), HALF + jnp.tile(jnp.arange(HALF), H_C)])
    expand = (src[:, None] == jnp.arange(QK_ROPE)[None, :]).astype(BF16)
    cspec = lambda g: pl.BlockSpec((1, 1, PAGE_SIZE, KV_LORA),
                                   lambda bi, j, pt: (layer, pt[bi, j * MLA_DEC_PAGES + g], 0, 0))
    kspec = lambda g: pl.BlockSpec((1, 1, QK_ROPE, PAGE_SIZE),
                                   lambda bi, j, pt: (layer, pt[bi, j * MLA_DEC_PAGES + g], 0, 0))
    grid_spec = pltpu.PrefetchScalarGridSpec(
        num_scalar_prefetch=1,
        grid=(b, n_pages // MLA_DEC_PAGES),
        in_specs=[pl.BlockSpec((1, H_C, KVW), lambda bi, j, pt: (bi, 0, 0)),
                  pl.BlockSpec((1, 1, KVW), lambda bi, j, pt: (bi, 0, 0)),
                  pl.BlockSpec(expand.shape, lambda bi, j, pt: (0, 0))]
                 + [cspec(g) for g in range(MLA_DEC_PAGES)] + [kspec(g) for g in range(MLA_DEC_PAGES)],
        out_specs=pl.BlockSpec((1, 1, H_C * KV_LORA), lambda bi, j, pt: (bi, 0, 0)),
        scratch_shapes=[pltpu.VMEM((H_C, 1), F32), pltpu.VMEM((H_C, 1), F32), pltpu.VMEM((H_C, KV_LORA), F32)],
    )
    return pl.pallas_call(
        _mla_decode_kernel,
        grid_spec=grid_spec,
        out_shape=jax.ShapeDtypeStruct((b, 1, H_C * KV_LORA), BF16),
        compiler_params=pltpu.CompilerParams(dimension_semantics=("parallel", "arbitrary"),
                                             vmem_limit_bytes=VMEM_LIMIT),
        name="mla_decode",
    )(page_table, q, kvn, expand, *([cache_c] * MLA_DEC_PAGES), *([cache_kpet] * MLA_DEC_PAGES))


def _prep_w_ab(w):
    qkv_a, alpha, braw, gate, q_b, k_b, v_b, qi, wi, ki = split_cols(w, AB_SIZES)
    z = jnp.zeros((w.shape[0], SM_KI - SM_WI - H_IDX), w.dtype)
    return jnp.concatenate([qkv_a, gate, q_b, k_b, v_b, qi, alpha, braw, wi, z, ki], axis=1).astype(BF16)


def _prep_w_c(w_in, w_uq, w_uk):
    cq, ckv, kpe = split_cols(w_in, C_SIZES)
    w_in_p = jnp.concatenate([cq, ckv, jnp.tile(kpe[:, :HALF], (1, H_C)), jnp.tile(kpe[:, HALF:], (1, H_C))], axis=1)
    wq = w_uq.reshape(Q_LORA, H_C, QK_NOPE + QK_ROPE)
    nope = jnp.pad(wq[:, :, :QK_NOPE], ((0, 0), (0, 0), (0, LANE - QK_NOPE))).reshape(Q_LORA, H_C * LANE)
    r1 = wq[:, :, QK_NOPE:QK_NOPE + HALF].reshape(Q_LORA, H_C * HALF)
    r2 = wq[:, :, QK_NOPE + HALF:].reshape(Q_LORA, H_C * HALF)
    w_uq_p = jnp.concatenate([nope, r1, r2], axis=1)
    w_uk_p = jnp.pad(jnp.transpose(w_uk, (1, 2, 0)), ((0, 0), (0, LANE - QK_NOPE), (0, 0)))
    return w_in_p.astype(BF16), w_uq_p.astype(BF16), w_uk_p.astype(BF16)


def _rope_tables(pos):
    inv = ROPE_BASE ** (-jnp.arange(HALF, dtype=jnp.float32) / HALF)
    ang = pos.astype(jnp.float32)[:, None] * inv[None, :]
    return jnp.tile(jnp.cos(ang), (1, LANE // HALF)), jnp.tile(jnp.sin(ang), (1, LANE // HALF))


def _kpe_from_rows(kv):
    return jnp.concatenate([kv[..., KV_LORA:KV_LORA + HALF], kv[..., KV_LORA + LANE:KV_LORA + LANE + HALF]], axis=-1)


def kernel(x_prompt, x_sample, state_a_conv, state_a_ssm, cache_b_k, cache_b_v, cache_b_kidx,
           cache_c_latent, cache_c_kpe, page_table, meta_tokens, rel_bias, norm_mix, norm_ffn, norm_final,
           w_in_ab, conv_a, a_log, dt_bias_a, gnorm_a, w_out_ab, w_in_c, qnorm_c, kvnorm_c,
           w_uq_c, w_uk_c, w_uv_c, w_out_c, w_up, w_down):
    w_up_b = w_up.astype(BF16)
    w_down_b = w_down.astype(BF16)
    w_out_ab_b = w_out_ab.astype(BF16)
    w_out_c_b = w_out_c.astype(BF16)
    w_uv_b = jnp.transpose(w_uv_c, (0, 2, 1, 3)).astype(BF16)
    w_in_ab_p = [_prep_w_ab(w_in_ab[j]) for j in range(w_in_ab.shape[0])]
    w_c_p = [_prep_w_c(w_in_c[j], w_uq_c[j], w_uk_c[j]) for j in range(w_in_c.shape[0])]

    b, seq, d = x_prompt.shape
    lp = N_META + seq
    pad = (-lp) % LANE
    l = pad + lp
    n = b * l
    topk = min(TOPK_MAX, SEQ // 4)
    meta = jnp.broadcast_to(meta_tokens[None].astype(x_prompt.dtype), (b, N_META, d))
    h = jnp.concatenate([jnp.zeros((b, pad, d), x_prompt.dtype), meta, x_prompt], axis=1).reshape(n, d)
    cos_p, sin_p = _rope_tables(jnp.arange(l, dtype=jnp.int32) - pad)
    a_conv_p, a_ssm_p, b_k_p, b_v_p, b_kidx_p, c_lat_p, c_kpe_p = [], [], [], [], [], [], []
    for li in range(DEPTH):
        j = li // 2
        last = li == DEPTH - 1
        if li % 2 == 0:
            proj = rms_linear(h, norm_mix[li], w_in_ab_p[j]).reshape(b, l, AB_COLS_P)
            o_a, s_fin = gdn_prompt(proj, conv_a[j], a_log[j], dt_bias_a[j], gnorm_a[j], pad)
            o_b = dsa_prompt(proj, bias_tiles(rel_bias), pad, topk)
            projs = [(o_a.reshape(n, HD), w_out_ab_b[j, :HD]), (o_b.reshape(n, HD), w_out_ab_b[j, HD:])]
            head_w = None
            a_conv_p.append(proj[:, l - (CONV_W - 1):, COL_QKV:COL_QKV + QKV_A])
            a_ssm_p.append(s_fin)
            b_k_p.append(proj[:, pad:, COL_KB:COL_KB + HD].reshape(b, lp, H_B, DH_B))
            b_v_p.append(proj[:, pad:, COL_VB:COL_VB + HD].reshape(b, lp, H_B, DH_B))
            b_kidx_p.append(proj[:, pad:, COL_SMALL + SM_KI:COL_SMALL + SM_KI + D_IDX])
        else:
            q_all, kv = mla_prep(h.reshape(b, l, d), norm_mix[li], *w_c_p[j][:1], qnorm_c[j], kvnorm_c[j],
                                 *w_c_p[j][1:], cos_p, sin_p)
            o_lat = mla_prompt(q_all, kv, pad)
            projs = [(o_lat.reshape(n, H_C * KV_LORA), w_out_c_b[j])]
            head_w = w_uv_b[j]
            c_lat_p.append(kv[:, pad:, :KV_LORA])
            c_kpe_p.append(_kpe_from_rows(kv[:, pad:]))
        h = proj_mlp(h, projs, norm_ffn[li], w_up_b[li], w_down_b[li], norm_final if last else None, head_w)
    y_p = h.reshape(b, l, d)[:, pad + N_META:]

    bs = x_sample.shape[0]
    n_pages = page_table.shape[1]
    past = n_pages * PAGE_SIZE
    topk_s = min(TOPK_MAX, (past + DEC_SEQ) // 4)
    hs = x_sample.reshape(bs, d)
    cos_s, sin_s = _rope_tables(jnp.full((bs,), past, jnp.int32))
    cache_kit = jnp.transpose(cache_b_kidx, (0, 1, 3, 2))
    cache_kt = jnp.transpose(cache_b_k, (0, 1, 3, 4, 2))
    cache_vt = jnp.transpose(cache_b_v, (0, 1, 3, 4, 2))
    cache_kpet = jnp.transpose(cache_c_kpe, (0, 1, 3, 2))
    a_conv_s, a_ssm_s, b_k_s, b_v_s, b_kidx_s, c_lat_s, c_kpe_s = [], [], [], [], [], [], []
    for li in range(DEPTH):
        j = li // 2
        last = li == DEPTH - 1
        if li % 2 == 0:
            proj = rms_linear(hs, norm_mix[li], w_in_ab_p[j])
            proj3 = proj.reshape(bs, 1, AB_COLS_P)
            o_a, conv_new, s_new = gdn_decode_step(proj, state_a_conv, state_a_ssm, j, conv_a[j], a_log[j],
                                                   dt_bias_a[j], gnorm_a[j])
            mask, mnew = dsa_decode_select(proj3, cache_kit, j, page_table, topk_s)
            o_b = dsa_decode_attend(proj, mask, mnew, cache_kt, cache_vt, j, page_table, rel_bias)
            projs = [(o_a, w_out_ab_b[j, :HD]), (o_b.reshape(bs, HD), w_out_ab_b[j, HD:])]
            head_w = None
            a_conv_s.append(conv_new)
            a_ssm_s.append(s_new)
            b_k_s.append(proj[:, COL_KB:COL_KB + HD].reshape(bs, DEC_SEQ, H_B, DH_B))
            b_v_s.append(proj[:, COL_VB:COL_VB + HD].reshape(bs, DEC_SEQ, H_B, DH_B))
            b_kidx_s.append(proj[:, COL_SMALL + SM_KI:COL_SMALL + SM_KI + D_IDX].reshape(bs, DEC_SEQ, D_IDX))
        else:
            q_all, kv = mla_prep(hs.reshape(1, bs, d), norm_mix[li], *w_c_p[j][:1], qnorm_c[j], kvnorm_c[j],
                                 *w_c_p[j][1:], cos_s, sin_s)
            o_lat = mla_decode(jnp.transpose(q_all[0], (1, 0, 2)), kv.reshape(bs, 1, KVW), cache_c_latent,
                               cache_kpet, j, page_table)
            projs = [(o_lat.reshape(bs, H_C * KV_LORA), w_out_c_b[j])]
            head_w = w_uv_b[j]
            c_lat_s.append(kv[0, :, :KV_LORA].reshape(bs, DEC_SEQ, KV_LORA))
            c_kpe_s.append(_kpe_from_rows(kv[0]).reshape(bs, DEC_SEQ, QK_ROPE))
        hs = proj_mlp(hs, projs, norm_ffn[li], w_up_b[li], w_down_b[li], norm_final if last else None, head_w)
    y_s = hs.reshape(bs, DEC_SEQ, d)

    stack = jnp.stack
    return (y_p, y_s,
            stack(a_conv_p), stack(a_ssm_p), stack(b_k_p), stack(b_v_p), stack(b_kidx_p), stack(c_lat_p), stack(c_kpe_p),
            stack(a_conv_s), stack(a_ssm_s), stack(b_k_s), stack(b_v_s), stack(b_kidx_s), stack(c_lat_s), stack(c_kpe_s))
```
